```python
import math
import jax, jax.numpy as jnp
from jax import lax
import numpy as np

D_MODEL = 2048
BATCH = 8
SEQ = 2048
DEPTH = 1

M_HEADS = 4
M_DH = 256
M_W = M_HEADS * M_DH
CONV_K = 4
CHUNK = 64
A_HEADS = 8
A_NOPE = 128
A_ROPE = 64
A_DV = 128
A_DQK = A_NOPE + A_ROPE
A_W = A_HEADS * A_DV
Q_LORA = 512
KV_LORA = 512
ROPE_THETA = 10000.0
Q_BLOCK = 128
N_MEM = 256
C_HEADS = 4
C_DH = 256
C_W = C_HEADS * C_DH
N_BRANCH = 3
EPS = 1e-6

IN_SPLITS = (2 * M_W, M_W, M_W, M_W, M_HEADS, M_HEADS,
             Q_LORA, KV_LORA, A_ROPE, A_W,
             C_W, C_W,
             N_BRANCH * D_MODEL)
IN_DIM = sum(IN_SPLITS)

kernel_name = "hybrid_mlstm_mla_memory_gated"


def rms_norm(x, g):
    xf = x.astype(jnp.float32)
    y = xf * lax.rsqrt(jnp.mean(xf * xf, axis=-1, keepdims=True) + EPS)
    return (y * g.astype(jnp.float32)).astype(x.dtype)


def split_cols(z):
    idx = np.cumsum(IN_SPLITS)[:-1].tolist()
    return jnp.split(z, idx, axis=-1)


def causal_depthwise_conv(x, w, b):
    S = x.shape[1]
    xp = jnp.pad(x, ((0, 0), (CONV_K - 1, 0), (0, 0)))
    y = b
    for j in range(CONV_K):
        y = y + w[j] * xp[:, j:j + S]
    return y


def apply_rope(x, cos, sin):
    xf = x.astype(jnp.float32)
    half = xf.shape[-1] // 2
    x1, x2 = xf[..., :half], xf[..., half:]
    return jnp.concatenate([x1 * cos - x2 * sin, x2 * cos + x1 * sin], axis=-1).astype(x.dtype)


def mlstm_chunkwise(q, k, v, ig, lf):
    B, S, H, d = q.shape
    nc = S // CHUNK

    def chunks4(t):
        return t.reshape(B, nc, CHUNK, H, t.shape[-1]).transpose(1, 0, 3, 2, 4)

    def chunks3(t):
        return t.reshape(B, nc, CHUNK, H).transpose(1, 0, 3, 2)

    tril = jnp.tril(jnp.ones((CHUNK, CHUNK), dtype=bool))

    def step(carry, xs):
        C, n, m = carry
        qc, kc, vc, ic, fc = xs
        b = jnp.cumsum(fc, axis=-1)
        g = b[..., -1]
        dmat = jnp.where(tril, b[..., :, None] - b[..., None, :] + ic[..., None, :], -jnp.inf)
        inter = b + m[..., None]
        m_t = jnp.maximum(inter, jnp.max(dmat, axis=-1))
        w_intra = jnp.exp(dmat - m_t[..., None])
        w_inter = jnp.exp(inter - m_t)
        s = jnp.einsum('bhtd,bhsd->bhts', qc, kc) * w_intra
        num = (w_inter[..., None] * jnp.einsum('bhtk,bhkv->bhtv', qc, C)
               + jnp.einsum('bhts,bhsv->bhtv', s, vc))
        den = w_inter * jnp.einsum('bhtk,bhk->bht', qc, n) + jnp.sum(s, axis=-1)
        h = num / jnp.maximum(jnp.abs(den), jnp.exp(-m_t))[..., None]
        a = g[..., None] - b + ic
        m_new = jnp.maximum(g + m, jnp.max(a, axis=-1))
        w_s = jnp.exp(a - m_new[..., None])
        decay = jnp.exp(g + m - m_new)
        C_new = decay[..., None, None] * C + jnp.einsum('bhs,bhsk,bhsv->bhkv', w_s, kc, vc)
        n_new = decay[..., None] * n + jnp.einsum('bhs,bhsk->bhk', w_s, kc)
        return (C_new, n_new, m_new), h

    init = (jnp.zeros((B, H, d, d), jnp.float32),
            jnp.zeros((B, H, d), jnp.float32),
            jnp.zeros((B, H), jnp.float32))
    xs = (chunks4(q), chunks4(k), chunks4(v), chunks3(ig), chunks3(lf))
    _, h = lax.scan(step, init, xs)
    return h.transpose(1, 0, 3, 2, 4).reshape(B, S, H, d)


def causal_block_attention(q, k, v):
    B, S, H, dqk = q.shape
    dv = v.shape[-1]
    nb = S // Q_BLOCK
    scale = 1.0 / math.sqrt(dqk)
    qb = q.reshape(B, nb, Q_BLOCK, H, dqk).transpose(1, 0, 3, 2, 4)
    k_pos = jnp.arange(S)

    def attend(args):
        qi, blk = args
        s = jnp.einsum('bhqd,bshd->bhqs', qi, k).astype(jnp.float32) * scale
        q_pos = blk * Q_BLOCK + jnp.arange(Q_BLOCK)
        s = jnp.where(k_pos[None, :] <= q_pos[:, None], s, -jnp.inf)
        p = jax.nn.softmax(s, axis=-1)
        return jnp.einsum('bhqs,bshd->bqhd', p.astype(v.dtype), v)

    out = lax.map(attend, (qb, jnp.arange(nb)))
    return out.transpose(1, 0, 2, 3, 4).reshape(B, S, H * dv)


def memory_attention(q, k, v):
    B, S, H, d = q.shape
    s = jnp.einsum('bshd,bmhd->bhsm', q, k).astype(jnp.float32) * (1.0 / math.sqrt(d))
    p = jax.nn.softmax(s, axis=-1)
    return jnp.einsum('bhsm,bmhd->bshd', p.astype(v.dtype), v).reshape(B, S, H * d)


def setup_inputs(seed: int = 0) -> dict:
    key = jax.random.key(seed)
    ks = jax.random.split(key, 24)
    f32 = jnp.float32

    def nrm(k, shape, fan_in):
        return jax.random.normal(k, shape, f32) * (fan_in ** -0.5)

    def gain(k, shape):
        return 1.0 + 0.02 * jax.random.normal(k, shape, f32)

    x = jax.random.normal(ks[0], (BATCH, SEQ, D_MODEL), f32)
    mem = jax.random.normal(ks[1], (BATCH, N_MEM, D_MODEL), f32)
    start = jax.random.randint(ks[2], (BATCH, 1), 0, 4096, dtype=jnp.int32)
    positions = start + jnp.arange(SEQ, dtype=jnp.int32)[None, :]
    return {
        "x": x,
        "mem": mem,
        "positions": positions,
        "w_in": nrm(ks[3], (DEPTH, D_MODEL, IN_DIM), D_MODEL),
        "b_igate": 0.1 * jax.random.normal(ks[4], (DEPTH, M_HEADS), f32),
        "b_fgate": 3.0 + 3.0 * jax.random.uniform(ks[5], (DEPTH, M_HEADS), f32),
        "conv_w": nrm(ks[6], (DEPTH, CONV_K, 2 * M_W), CONV_K),
        "conv_b": 0.02 * jax.random.normal(ks[7], (DEPTH, 2 * M_W), f32),
        "mh_norm": gain(ks[8], (DEPTH, M_W)),
        "cq_norm": gain(ks[9], (DEPTH, Q_LORA)),
        "w_uq": nrm(ks[10], (DEPTH, Q_LORA, A_HEADS * A_DQK), Q_LORA),
        "ckv_norm": gain(ks[11], (DEPTH, KV_LORA)),
        "w_ukv": nrm(ks[12], (DEPTH, KV_LORA, A_HEADS * (A_NOPE + A_DV)), KV_LORA),
        "mem_norm": gain(ks[13], (DEPTH, D_MODEL)),
        "w_mem_kv": nrm(ks[14], (DEPTH, D_MODEL, 2 * C_W), D_MODEL),
        "w_br_m": nrm(ks[15], (DEPTH, M_W, D_MODEL), M_W),
        "w_br_a": nrm(ks[16], (DEPTH, A_W, D_MODEL), A_W),
        "w_br_c": nrm(ks[17], (DEPTH, C_W, D_MODEL), C_W),
        "w_out": nrm(ks[18], (DEPTH, D_MODEL, D_MODEL), D_MODEL),
        "norm": gain(ks[19], (DEPTH, D_MODEL)),
        "final_norm": gain(ks[20], (D_MODEL,)),
    }


def reference(x, mem, positions, w_in, b_igate, b_fgate, conv_w, conv_b, mh_norm, cq_norm, w_uq,
              ckv_norm, w_ukv, mem_norm, w_mem_kv, w_br_m, w_br_a, w_br_c, w_out, norm, final_norm):
    B, S, _ = x.shape
    f32 = jnp.float32
    inv_freq = ROPE_THETA ** (-jnp.arange(0, A_ROPE, 2, dtype=f32) / A_ROPE)
    ang = positions.astype(f32)[..., None] * inv_freq
    cos, sin = jnp.cos(ang), jnp.sin(ang)

    for l in range(DEPTH):
        h = rms_norm(x, norm[l])
        proj = h @ w_in[l]
        (m_qk, m_v, m_o, m_z, m_i, m_f,
         a_cq, a_ckv, a_kr, a_z,
         c_q, c_z, gates) = split_cols(proj)

        qk = jax.nn.silu(causal_depthwise_conv(m_qk, conv_w[l], conv_b[l]))
        mq, mk = jnp.split(qk, 2, axis=-1)
        mq = mq.reshape(B, S, M_HEADS, M_DH).astype(f32)
        mk = mk.reshape(B, S, M_HEADS, M_DH).astype(f32) * (M_DH ** -0.5)
        mv = m_v.reshape(B, S, M_HEADS, M_DH).astype(f32)
        ig = (m_i + b_igate[l]).astype(f32)
        lf = jax.nn.log_sigmoid((m_f + b_fgate[l]).astype(f32))
        hm = mlstm_chunkwise(mq, mk, mv, ig, lf)
        hm = rms_norm(hm, mh_norm[l].reshape(M_HEADS, M_DH)).reshape(B, S, M_W).astype(x.dtype)
        hm = hm * jax.nn.sigmoid(m_o) * jax.nn.silu(m_z)

        q_all = (rms_norm(a_cq, cq_norm[l]) @ w_uq[l]).reshape(B, S, A_HEADS, A_DQK)
        q_nope, q_rope = q_all[..., :A_NOPE], q_all[..., A_NOPE:]
        q_rope = apply_rope(q_rope, cos[:, :, None, :], sin[:, :, None, :])
        kv = (rms_norm(a_ckv, ckv_norm[l]) @ w_ukv[l]).reshape(B, S, A_HEADS, A_NOPE + A_DV)
        k_nope, v_a = kv[..., :A_NOPE], kv[..., A_NOPE:]
        k_rope = apply_rope(a_kr, cos, sin)
        qa = jnp.concatenate([q_nope, q_rope], axis=-1)
        ka = jnp.concatenate([k_nope, jnp.broadcast_to(k_rope[:, :, None, :], (B, S, A_HEADS, A_ROPE))], axis=-1)
        ha = causal_block_attention(qa, ka, v_a) * jax.nn.silu(a_z)

        mkv = (rms_norm(mem, mem_norm[l]) @ w_mem_kv[l]).reshape(B, N_MEM, 2, C_HEADS, C_DH)
        hc = memory_attention(c_q.reshape(B, S, C_HEADS, C_DH), mkv[:, :, 0], mkv[:, :, 1])
        hc = hc * jax.nn.silu(c_z)

        g_m = jax.nn.sigmoid(gates[..., 0:D_MODEL])
        g_a = jax.nn.sigmoid(gates[..., D_MODEL:2 * D_MODEL])
        g_c = jax.nn.sigmoid(gates[..., 2 * D_MODEL:3 * D_MODEL])
        merged = g_m * (hm @ w_br_m[l]) + g_a * (ha @ w_br_a[l]) + g_c * (hc @ w_br_c[l])
        x = x + merged @ w_out[l]

    return rms_norm(x, final_norm)
```

```python
import functools
import math

import jax
import jax.numpy as jnp
from jax import lax
from jax.experimental import pallas as pl
from jax.experimental.pallas import tpu as pltpu

F32 = jnp.float32
BF16 = jnp.bfloat16

D_MODEL = 2048
M_HEADS, M_DH = 4, 256
M_W = M_HEADS * M_DH
CONV_K = 4
A_HEADS, A_NOPE, A_ROPE, A_DV = 8, 128, 64, 128
A_DQK = A_NOPE + A_ROPE
A_W = A_HEADS * A_DV
Q_LORA = KV_LORA = 512
ROPE_THETA = 10000.0
N_MEM = 256
C_HEADS, C_DH = 4, 256
C_W = C_HEADS * C_DH
EPS = 1e-6

LANES = 128
SUBLANES = 8
A_QK_PAD = 2 * LANES
VMEM_LIMIT = 56 * 1024 * 1024

IN_SPLITS = (2 * M_W, M_W, M_W, M_W, M_HEADS, M_HEADS, Q_LORA, KV_LORA, A_ROPE, A_W, C_W, C_W,
             3 * D_MODEL)


def _const_spec(shape):
    nd = len(shape)
    return pl.BlockSpec(shape, lambda *_: (0,) * nd, pipeline_mode=pl.Buffered(1))


def _params(*sem):
    return pltpu.CompilerParams(dimension_semantics=sem, vmem_limit_bytes=VMEM_LIMIT)


def _dot(a, b):
    return jnp.dot(a, b, preferred_element_type=F32)


def _dot_nt(a, b):
    return lax.dot_general(a, b, (((1,), (1,)), ((), ())), preferred_element_type=F32)


def _rms(x, g):
    return x * lax.rsqrt(jnp.mean(x * x, axis=-1, keepdims=True) + EPS) * g


def _sigmoid(x):
    return 1.0 / (1.0 + jnp.exp(-x))


def _silu(x):
    return x * _sigmoid(x)


MP_TM = 512
MP_CH = 512


def _mlstm_proj_kernel(tiles_per_seq, x_ref, ng_ref, wqk_ref, cw_ref, cb_ref, wv_ref, wo_ref,
                       wz_ref, wif_ref, h_ref, q_ref, k_ref, v_ref, g_ref, if_ref,
                       carry_ref, xbuf_ref):
    tm = x_ref.shape[0]

    @pl.when(pl.program_id(0) % tiles_per_seq == 0)
    def _():
        carry_ref[...] = jnp.zeros_like(carry_ref)

    h = _rms(x_ref[...], ng_ref[...]).astype(BF16)
    h_ref[...] = h

    for c in range(2 * M_W // MP_CH):
        cs = slice(c * MP_CH, (c + 1) * MP_CH)
        acc = _dot(h, wqk_ref[:, cs])
        xbuf_ref[0:SUBLANES, :] = carry_ref[:, cs]
        xbuf_ref[SUBLANES:SUBLANES + tm, :] = acc
        carry_ref[:, cs] = acc[tm - SUBLANES:tm, :]
        y = cb_ref[:, cs] + cw_ref[CONV_K - 1:CONV_K, cs] * acc
        for j in range(CONV_K - 1):
            back = CONV_K - 1 - j
            y = y + cw_ref[j:j + 1, cs] * xbuf_ref[SUBLANES - back:SUBLANES - back + tm, :]
        y = _silu(y)
        if c < M_W // MP_CH:
            q_ref[:, cs] = y.astype(BF16)
        else:
            ks = slice(c * MP_CH - M_W, (c + 1) * MP_CH - M_W)
            k_ref[:, ks] = (y * (M_DH ** -0.5)).astype(BF16)

    for c in range(M_W // MP_CH):
        cs = slice(c * MP_CH, (c + 1) * MP_CH)
        v_ref[:, cs] = _dot(h, wv_ref[:, cs]).astype(BF16)
        o = _dot(h, wo_ref[:, cs])
        z = _dot(h, wz_ref[:, cs])
        g_ref[:, cs] = (_sigmoid(o) * _silu(z)).astype(BF16)

    if_ref[...] = _dot(h, wif_ref[...])


def _mlstm_proj(x2, norm_g, wqk, conv_w, conv_b, wv, wo, wz, wif, seq):
    T = x2.shape[0]
    tm = MP_TM
    row = lambda w: pl.BlockSpec((tm, w), lambda i: (i, 0))
    out_shape = (
        jax.ShapeDtypeStruct((T, D_MODEL), BF16),
        jax.ShapeDtypeStruct((T, M_W), BF16),
        jax.ShapeDtypeStruct((T, M_W), BF16),
        jax.ShapeDtypeStruct((T, M_W), BF16),
        jax.ShapeDtypeStruct((T, M_W), BF16),
        jax.ShapeDtypeStruct((T, LANES), F32),
    )
    return pl.pallas_call(
        functools.partial(_mlstm_proj_kernel, seq // tm),
        grid=(T // tm,),
        in_specs=[row(D_MODEL), _const_spec((1, D_MODEL)), _const_spec(wqk.shape),
                  _const_spec(conv_w.shape), _const_spec(conv_b.shape), _const_spec(wv.shape),
                  _const_spec(wo.shape), _const_spec(wz.shape), _const_spec(wif.shape)],
        out_specs=(row(D_MODEL), row(M_W), row(M_W), row(M_W), row(M_W), row(LANES)),
        out_shape=out_shape,
        scratch_shapes=[pltpu.VMEM((SUBLANES, 2 * M_W), F32),
                        pltpu.VMEM((tm + SUBLANES, MP_CH), F32)],
        compiler_params=_params("arbitrary"),
        name="mlstm_proj",
    )(x2, norm_g, wqk, conv_w, conv_b, wv, wo, wz, wif)


def _lane_scan(x, op, fill):
    n = x.shape[-1]
    lane = lax.broadcasted_iota(jnp.int32, x.shape, x.ndim - 1)
    d = 1
    while d < n:
        shifted = pltpu.roll(x, d, x.ndim - 1)
        x = op(x, jnp.where(lane >= d, shifted, fill))
        d *= 2
    return x


def _gate_scan_kernel(if_ref, bias_ref, row_ref, col_ref):
    S = if_ref.shape[1]
    pre = if_ref[0] + bias_ref[...]
    t8 = pre.T[0:SUBLANES, :]
    lf = jnp.minimum(t8, 0.0) - jnp.log(1.0 + jnp.exp(-jnp.abs(t8)))
    b = _lane_scan(lf, jnp.add, 0.0)
    b = pltpu.roll(b, M_HEADS, 0)
    a = t8 - b
    mx = jnp.maximum(_lane_scan(a, jnp.maximum, -jnp.inf), 0.0)
    nb = -b - mx
    sub = lax.broadcasted_iota(jnp.int32, (SUBLANES, S), 0)
    zeros = jnp.zeros((LANES - SUBLANES, S), F32)
    for p in range(M_HEADS // 2):
        up = (SUBLANES - 2 * p) % SUBLANES
        a_pair = pltpu.roll(a, up, 0) if up else a
        for j in range(S // DA_T):
            row_ref[0, p, j] = a_pair[:, j * DA_T:(j + 1) * DA_T]
        m_pair = pltpu.roll(mx, up, 0) if up else mx
        nb_pair = pltpu.roll(nb, (up + 2) % SUBLANES, 0)
        stack = jnp.concatenate([jnp.where(sub < 2, m_pair, nb_pair), zeros], axis=0)
        col_ref[0, p] = stack.T


def _gate_scan(pre_if, bias_row, batch, seq):
    n_pair = M_HEADS // 2
    n_tile = seq // DA_T
    return pl.pallas_call(
        _gate_scan_kernel,
        grid=(batch,),
        in_specs=[pl.BlockSpec((1, seq, LANES), lambda b: (b, 0, 0)), _const_spec((1, LANES))],
        out_specs=(pl.BlockSpec((1, n_pair, n_tile, SUBLANES, DA_T), lambda b: (b, 0, 0, 0, 0)),
                   pl.BlockSpec((1, n_pair, seq, LANES), lambda b: (b, 0, 0, 0))),
        out_shape=(jax.ShapeDtypeStruct((batch, n_pair, n_tile, SUBLANES, DA_T), F32),
                   jax.ShapeDtypeStruct((batch, n_pair, seq, LANES), F32)),
        compiler_params=_params("parallel"),
        name="gate_scan",
    )(pre_if.reshape(batch, seq, LANES), bias_row)


DA_T = 256


def _decay_attn_kernel(q_ref, k_ref, v_ref, g_ref, row_ref, col_ref, gain_ref, o_ref):
    S = q_ref.shape[1]
    t = DA_T
    n_head = q_ref.shape[2] // M_DH
    tri = (lax.broadcasted_iota(jnp.int32, (t, t), 1) <= lax.broadcasted_iota(jnp.int32, (t, t), 0))

    def q_step(qi, _):
        qs = pl.ds(pl.multiple_of(qi * t, t), t)
        for hh in range(n_head):
            hs = slice(hh * M_DH, (hh + 1) * M_DH)
            q = q_ref[0, qs, hs]
            m_col = col_ref[0, 0, qs, hh:hh + 1]
            nb_col = col_ref[0, 0, qs, n_head + hh:n_head + hh + 1]

            def block(kj, masked):
                ks = pl.ds(pl.multiple_of(kj * t, t), t)
                s = _dot_nt(q, k_ref[0, ks, hs])
                arg = row_ref[0, 0, kj, hh:hh + 1, :] - m_col
                if masked:
                    arg = jnp.where(tri, arg, -jnp.inf)
                p = s * jnp.exp(arg)
                return p[:, :LANES] + p[:, LANES:], _dot(p.astype(BF16), v_ref[0, ks, hs])

            def k_step(kj, carry):
                den, num = carry
                d, n = block(kj, False)
                return den + d, num + n

            den, num = lax.fori_loop(0, qi, k_step,
                                     (jnp.zeros((t, LANES), F32), jnp.zeros((t, M_DH), F32)))
            d, n = block(qi, True)
            den = jnp.sum(den + d, axis=-1, keepdims=True)
            hv = (num + n) / jnp.maximum(jnp.abs(den), jnp.exp(nb_col))
            hv = _rms(hv, gain_ref[:, hs])
            o_ref[0, qs, hs] = (hv * g_ref[0, qs, hs].astype(F32)).astype(BF16)
        return 0

    lax.fori_loop(0, S // t, q_step, 0)


def _decay_attn(q, k, v, gate, rowf, colf, gain, batch, seq):
    n_pair = M_HEADS // 2
    w = 2 * M_DH
    blk = pl.BlockSpec((1, seq, w), lambda b, p: (b, 0, p))
    q3, k3, v3, g3 = (a.reshape(batch, seq, M_W) for a in (q, k, v, gate))
    return pl.pallas_call(
        _decay_attn_kernel,
        grid=(batch, n_pair),
        in_specs=[blk, blk, blk, blk,
                  pl.BlockSpec((1, 1, seq // DA_T, SUBLANES, DA_T), lambda b, p: (b, p, 0, 0, 0)),
                  pl.BlockSpec((1, 1, seq, LANES), lambda b, p: (b, p, 0, 0)),
                  pl.BlockSpec((1, w), lambda b, p: (0, p))],
        out_specs=blk,
        out_shape=jax.ShapeDtypeStruct((batch, seq, M_W), BF16),
        compiler_params=_params("parallel", "parallel"),
        name="decay_attn",
    )(q3, k3, v3, g3, rowf, colf, gain).reshape(batch * seq, M_W)


LP_TM = 512


def _mla_proj_kernel(h_ref, pos_ref, invf_ref, sgn_ref, gq_ref, gkv_ref, wcq_ref, wckv_ref,
                     wkr_ref, waz_ref, wqn_ref, wqr_ref, wkn_ref, wv_ref,
                     q_ref, k_ref, v_ref, g_ref):
    h = h_ref[...]
    scale = 1.0 / math.sqrt(A_DQK)
    ang = pos_ref[...] * invf_ref[...]
    cos = jnp.cos(ang)
    sin = jnp.sin(ang) * sgn_ref[...]

    def rope(r):
        return r * cos + pltpu.roll(r, LANES // 2, 1) * sin

    cqn = _rms(_dot(h, wcq_ref[...]), gq_ref[...]).astype(BF16)
    qn = _dot(cqn, wqn_ref[...]) * scale
    qr = _dot(cqn, wqr_ref[...]) * scale
    for hd in range(A_HEADS):
        base = hd * A_QK_PAD
        ls = slice(hd * LANES, (hd + 1) * LANES)
        q_ref[:, base:base + LANES] = qn[:, ls].astype(BF16)
        q_ref[:, base + LANES:base + A_QK_PAD] = rope(qr[:, ls]).astype(BF16)

    kr = rope(_dot(h, wkr_ref[...])).astype(BF16)
    ckvn = _rms(_dot(h, wckv_ref[...]), gkv_ref[...]).astype(BF16)
    kn = _dot(ckvn, wkn_ref[...])
    for hd in range(A_HEADS):
        base = hd * A_QK_PAD
        k_ref[:, base:base + LANES] = kn[:, hd * LANES:(hd + 1) * LANES].astype(BF16)
        k_ref[:, base + LANES:base + A_QK_PAD] = kr
    v_ref[...] = _dot(ckvn, wv_ref[...]).astype(BF16)
    g_ref[...] = _silu(_dot(h, waz_ref[...])).astype(BF16)


def _mla_proj(h, pos_col, invf, sgn, gq, gkv, wcq, wckv, wkr, waz, wqn, wqr, wkn, wv):
    T = h.shape[0]
    tm = LP_TM
    row = lambda w: pl.BlockSpec((tm, w), lambda i: (i, 0))
    consts = (invf, sgn, gq, gkv, wcq, wckv, wkr, waz, wqn, wqr, wkn, wv)
    return pl.pallas_call(
        _mla_proj_kernel,
        grid=(T // tm,),
        in_specs=[row(D_MODEL), row(1)] + [_const_spec(c.shape) for c in consts],
        out_specs=(row(A_HEADS * A_QK_PAD), row(A_HEADS * A_QK_PAD), row(A_W), row(A_W)),
        out_shape=(jax.ShapeDtypeStruct((T, A_HEADS * A_QK_PAD), BF16),
                   jax.ShapeDtypeStruct((T, A_HEADS * A_QK_PAD), BF16),
                   jax.ShapeDtypeStruct((T, A_W), BF16),
                   jax.ShapeDtypeStruct((T, A_W), BF16)),
        compiler_params=_params("parallel"),
        name="mla_proj",
    )(h, pos_col, *consts)


FA_T = 256
FA_HP = 2


def _mla_attn_kernel(q_ref, k_ref, v_ref, g_ref, o_ref):
    S = q_ref.shape[1]
    t = FA_T
    tri = (lax.broadcasted_iota(jnp.int32, (t, t), 1) <= lax.broadcasted_iota(jnp.int32, (t, t), 0))

    def q_step(qi, _):
        qs = pl.ds(pl.multiple_of(qi * t, t), t)
        for hh in range(FA_HP):
            hq = slice(hh * A_QK_PAD, (hh + 1) * A_QK_PAD)
            hv = slice(hh * A_DV, (hh + 1) * A_DV)
            q = q_ref[0, qs, hq]

            def block(carry, ks, masked):
                m, l, acc = carry
                s = _dot_nt(q, k_ref[0, ks, hq])
                if masked:
                    s = jnp.where(tri, s, -jnp.inf)
                m_new = jnp.maximum(m, jnp.max(s, axis=-1, keepdims=True))
                p = jnp.exp(s - m_new)
                alpha = jnp.exp(m - m_new)
                l = alpha * l + jnp.sum(p, axis=-1, keepdims=True)
                acc = alpha * acc + _dot(p.astype(BF16), v_ref[0, ks, hv])
                return m_new, l, acc

            def k_step(kj, carry):
                return block(carry, pl.ds(pl.multiple_of(kj * t, t), t), False)

            init = (jnp.full((t, 1), -jnp.inf, F32), jnp.zeros((t, 1), F32),
                    jnp.zeros((t, A_DV), F32))
            carry = lax.fori_loop(0, qi, k_step, init)
            _, l, acc = block(carry, qs, True)
            o_ref[0, qs, hv] = (acc / l * g_ref[0, qs, hv].astype(F32)).astype(BF16)
        return 0

    lax.fori_loop(0, S // t, q_step, 0)


def _mla_attn(q, k, v, gate, batch, seq):
    qk_blk = pl.BlockSpec((1, seq, FA_HP * A_QK_PAD), lambda b, p: (b, 0, p))
    v_blk = pl.BlockSpec((1, seq, FA_HP * A_DV), lambda b, p: (b, 0, p))
    q3 = q.reshape(batch, seq, A_HEADS * A_QK_PAD)
    k3 = k.reshape(batch, seq, A_HEADS * A_QK_PAD)
    v3 = v.reshape(batch, seq, A_W)
    g3 = gate.reshape(batch, seq, A_W)
    return pl.pallas_call(
        _mla_attn_kernel,
        grid=(batch, A_HEADS // FA_HP),
        in_specs=[qk_blk, qk_blk, v_blk, v_blk],
        out_specs=v_blk,
        out_shape=jax.ShapeDtypeStruct((batch, seq, A_W), BF16),
        compiler_params=_params("parallel", "parallel"),
        name="mla_attn",
    )(q3, k3, v3, g3).reshape(batch * seq, A_W)


def _mem_kv_kernel(mem_ref, g_ref, w_ref, kv_ref):
    kv_ref[0] = _dot(_rms(mem_ref[0], g_ref[...]).astype(BF16), w_ref[...]).astype(BF16)


def _mem_kv(mem, gain, w):
    batch = mem.shape[0]
    return pl.pallas_call(
        _mem_kv_kernel,
        grid=(batch,),
        in_specs=[pl.BlockSpec((1, N_MEM, D_MODEL), lambda b: (b, 0, 0)),
                  _const_spec((1, D_MODEL)), _const_spec(w.shape)],
        out_specs=pl.BlockSpec((1, N_MEM, 2 * C_W), lambda b: (b, 0, 0)),
        out_shape=jax.ShapeDtypeStruct((batch, N_MEM, 2 * C_W), BF16),
        compiler_params=_params("parallel"),
        name="mem_kv",
    )(mem, gain, w)


MA_TM = 512


def _mem_attn_kernel(h_ref, kv_ref, wq_ref, wz_ref, o_ref):
    h = h_ref[...]
    cq = (_dot(h, wq_ref[...]) * (C_DH ** -0.5)).astype(BF16)
    cz = _dot(h, wz_ref[...])
    for hd in range(C_HEADS):
        hs = slice(hd * C_DH, (hd + 1) * C_DH)
        s = _dot_nt(cq[:, hs], kv_ref[0, :, hs])
        p = jnp.exp(s - jnp.max(s, axis=-1, keepdims=True))
        l = jnp.sum(p, axis=-1, keepdims=True)
        o = _dot(p.astype(BF16), kv_ref[0, :, C_W + hd * C_DH:C_W + (hd + 1) * C_DH]) / l
        o_ref[:, hs] = (o * _silu(cz[:, hs])).astype(BF16)


def _mem_attn(h, kv, wq, wz, seq):
    T = h.shape[0]
    tm = MA_TM
    per_seq = seq // tm
    return pl.pallas_call(
        _mem_attn_kernel,
        grid=(T // tm,),
        in_specs=[pl.BlockSpec((tm, D_MODEL), lambda i: (i, 0)),
                  pl.BlockSpec((1, N_MEM, 2 * C_W), lambda i: (i // per_seq, 0, 0)),
                  _const_spec(wq.shape), _const_spec(wz.shape)],
        out_specs=pl.BlockSpec((tm, C_W), lambda i: (i, 0)),
        out_shape=jax.ShapeDtypeStruct((T, C_W), BF16),
        compiler_params=_params("parallel"),
        name="mem_attn",
    )(h, kv, wq, wz)


MG_TM = 1024
MG_TN = 256


def _merge_kernel(h_ref, hm_ref, ha_ref, hc_ref, wgm_ref, wga_ref, wgc_ref,
                  wbm_ref, wba_ref, wbc_ref, o_ref):
    h = h_ref[...]
    acc = _sigmoid(_dot(h, wgm_ref[...])) * _dot(hm_ref[...], wbm_ref[...])
    acc = acc + _sigmoid(_dot(h, wga_ref[...])) * _dot(ha_ref[...], wba_ref[...])
    acc = acc + _sigmoid(_dot(h, wgc_ref[...])) * _dot(hc_ref[...], wbc_ref[...])
    o_ref[...] = acc.astype(BF16)


def _merge(h, hm, ha, hc, wgm, wga, wgc, wbm, wba, wbc):
    T = h.shape[0]
    tm, tn = MG_TM, MG_TN
    row = lambda w: pl.BlockSpec((tm, w), lambda i, j: (i, 0))
    col = lambda kdim: pl.BlockSpec((kdim, tn), lambda i, j: (0, j))
    return pl.pallas_call(
        _merge_kernel,
        grid=(T // tm, D_MODEL // tn),
        in_specs=[row(D_MODEL), row(M_W), row(A_W), row(C_W),
                  col(D_MODEL), col(D_MODEL), col(D_MODEL), col(M_W), col(A_W), col(C_W)],
        out_specs=pl.BlockSpec((tm, tn), lambda i, j: (i, j)),
        out_shape=jax.ShapeDtypeStruct((T, D_MODEL), BF16),
        compiler_params=_params("parallel", "arbitrary"),
        name="merge",
    )(h, hm, ha, hc, wgm, wga, wgc, wbm, wba, wbc)


OP_TM = 512


def _out_proj_kernel(final, x_ref, m_ref, w_ref, g_ref, o_ref):
    y = x_ref[...] + _dot(m_ref[...], w_ref[...])
    o_ref[...] = _rms(y, g_ref[...]) if final else y


def _out_proj(x2, merged, w, gain, final):
    T = x2.shape[0]
    tm = OP_TM
    row = pl.BlockSpec((tm, D_MODEL), lambda i: (i, 0))
    return pl.pallas_call(
        functools.partial(_out_proj_kernel, final),
        grid=(T // tm,),
        in_specs=[row, row, _const_spec(w.shape), _const_spec((1, D_MODEL))],
        out_specs=row,
        out_shape=jax.ShapeDtypeStruct((T, D_MODEL), F32),
        compiler_params=_params("parallel"),
        name="out_proj",
    )(x2, merged, w, gain)


def _rope_lanes(a):
    half = A_ROPE // 2
    z = jnp.zeros(a.shape[:-1] + (half,), a.dtype)
    return jnp.concatenate([a[..., :half], z, a[..., half:], z], axis=-1)


def _layer(x2, pos_col, kv_mem_in, l, final, batch, seq, w_in, b_igate, b_fgate, conv_w, conv_b,
           mh_norm, cq_norm, w_uq, ckv_norm, w_ukv, mem_norm, w_mem_kv, w_br_m, w_br_a, w_br_c,
           w_out, norm, final_norm):
    offs = [0]
    for s in IN_SPLITS:
        offs.append(offs[-1] + s)
    wl = w_in[l]
    seg = lambda i: wl[:, offs[i]:offs[i + 1]]
    bf = lambda a: a.astype(BF16)
    row = lambda a: a.reshape(1, -1).astype(F32)

    wqk, wv, wo, wz = bf(seg(0)), bf(seg(1)), bf(seg(2)), bf(seg(3))
    wif = bf(jnp.pad(wl[:, offs[4]:offs[6]], ((0, 0), (0, LANES - 2 * M_HEADS))))
    wcq, wckv, waz = bf(seg(6)), bf(seg(7)), bf(seg(9))
    wkr = bf(_rope_lanes(seg(8)))
    wcqm, wczm = bf(seg(10)), bf(seg(11))
    gates = seg(12)
    wgm, wga, wgc = (bf(gates[:, i * D_MODEL:(i + 1) * D_MODEL]) for i in range(3))
    uq = w_uq[l].reshape(Q_LORA, A_HEADS, A_DQK)
    wqn = bf(uq[:, :, :A_NOPE].reshape(Q_LORA, A_HEADS * A_NOPE))
    wqr = bf(_rope_lanes(uq[:, :, A_NOPE:]).reshape(Q_LORA, A_HEADS * LANES))
    ukv = w_ukv[l].reshape(KV_LORA, A_HEADS, A_NOPE + A_DV)
    wkn = bf(ukv[:, :, :A_NOPE].reshape(KV_LORA, A_HEADS * A_NOPE))
    wvv = bf(ukv[:, :, A_NOPE:].reshape(KV_LORA, A_HEADS * A_DV))
    bias_row = jnp.pad(jnp.concatenate([b_igate[l], b_fgate[l]]).astype(F32),
                       (0, LANES - 2 * M_HEADS)).reshape(1, LANES)
    inv_freq = ROPE_THETA ** (-jnp.arange(0, A_ROPE, 2, dtype=F32) / A_ROPE)
    invf = _rope_lanes(jnp.concatenate([inv_freq, inv_freq])).reshape(1, LANES)
    ones = jnp.ones((A_ROPE // 2,), F32)
    sgn = _rope_lanes(jnp.concatenate([-ones, ones])).reshape(1, LANES)

    h, mq, mk, mv, mgate, pre_if = _mlstm_proj(x2, row(norm[l]), wqk, conv_w[l].astype(F32),
                                               row(conv_b[l]), wv, wo, wz, wif, seq)
    rowf, colf = _gate_scan(pre_if, bias_row, batch, seq)
    hm = _decay_attn(mq, mk, mv, mgate, rowf, colf, row(mh_norm[l]), batch, seq)

    aq, ak, av, agate = _mla_proj(h, pos_col, invf, sgn, row(cq_norm[l]), row(ckv_norm[l]),
                                  wcq, wckv, wkr, waz, wqn, wqr, wkn, wvv)
    ha = _mla_attn(aq, ak, av, agate, batch, seq)

    kv_mem = _mem_kv(kv_mem_in, row(mem_norm[l]), bf(w_mem_kv[l]))
    hc = _mem_attn(h, kv_mem, wcqm, wczm, seq)

    merged = _merge(h, hm, ha, hc, wgm, wga, wgc, bf(w_br_m[l]), bf(w_br_a[l]), bf(w_br_c[l]))
    return _out_proj(x2, merged, bf(w_out[l]), row(final_norm), final)


def kernel(x, mem, positions, w_in, b_igate, b_fgate, conv_w, conv_b, mh_norm, cq_norm, w_uq,
           ckv_norm, w_ukv, mem_norm, w_mem_kv, w_br_m, w_br_a, w_br_c, w_out, norm, final_norm):
    batch, seq, d = x.shape
    depth = w_in.shape[0]
    assert d == D_MODEL and seq % MG_TM == 0 and w_in.shape[2] == sum(IN_SPLITS)
    x2 = x.reshape(batch * seq, d)
    pos_col = positions.astype(F32).reshape(batch * seq, 1)
    for l in range(depth):
        x2 = _layer(x2, pos_col, mem, l, l == depth - 1, batch, seq, w_in, b_igate, b_fgate,
                    conv_w, conv_b, mh_norm, cq_norm, w_uq, ckv_norm, w_ukv, mem_norm, w_mem_kv,
                    w_br_m, w_br_a, w_br_c, w_out, norm, final_norm)
    return x2.reshape(batch, seq, d)
```

```python
import functools
import math

import jax
import jax.numpy as jnp
from jax import lax
from jax.experimental import pallas as pl
from jax.experimental.pallas import tpu as pltpu

F32 = jnp.float32
BF16 = jnp.bfloat16

D_MODEL = 2048
M_HEADS, M_DH = 4, 256
M_W = M_HEADS * M_DH
CONV_K = 4
A_HEADS, A_NOPE, A_ROPE, A_DV = 8, 128, 64, 128
A_DQK = A_NOPE + A_ROPE
A_W = A_HEADS * A_DV
Q_LORA = KV_LORA = 512
ROPE_THETA = 10000.0
N_MEM = 256
C_HEADS, C_DH = 4, 256
C_W = C_HEADS * C_DH
EPS = 1e-6

LANES = 128
SUBLANES = 8
A_QK_PAD = 2 * LANES
VMEM_LIMIT = 56 * 1024 * 1024

SEQ_TILE = 512
HEADS_PER_STEP = 2

IN_SPLITS = (2 * M_W, M_W, M_W, M_W, M_HEADS, M_HEADS, Q_LORA, KV_LORA, A_ROPE, A_W, C_W, C_W,
             3 * D_MODEL)


def _const_spec(shape):
    nd = len(shape)
    return pl.BlockSpec(shape, lambda *_: (0,) * nd, pipeline_mode=pl.Buffered(1))


def _params(*sem):
    return pltpu.CompilerParams(dimension_semantics=sem, vmem_limit_bytes=VMEM_LIMIT)


def _dot(a, b):
    return jnp.dot(a, b, preferred_element_type=F32)


def _dot_nt(a, b):
    return lax.dot_general(a, b, (((1,), (1,)), ((), ())), preferred_element_type=F32)


def _rms(x, g):
    return x * lax.rsqrt(jnp.mean(x * x, axis=-1, keepdims=True) + EPS) * g


def _sigmoid(x):
    return 1.0 / (1.0 + jnp.exp(-x))


def _silu(x):
    return x * _sigmoid(x)


def _keys_le_queries(t):
    return lax.broadcasted_iota(jnp.int32, (t, t), 0) <= lax.broadcasted_iota(jnp.int32, (t, t), 1)


MP_CH = 512


def _mlstm_proj_kernel(tiles_per_seq, x_ref, ng_ref, wqk_ref, cw_ref, cb_ref, wvt_ref, wo_ref,
                       wz_ref, wif_ref, h_ref, q_ref, k_ref, vt_ref, g_ref, if_ref,
                       carry_ref, xbuf_ref):
    tm = x_ref.shape[0]

    @pl.when(pl.program_id(0) % tiles_per_seq == 0)
    def _():
        carry_ref[...] = jnp.zeros_like(carry_ref)

    h = _rms(x_ref[...], ng_ref[...]).astype(BF16)
    h_ref[...] = h

    for c in range(2 * M_W // MP_CH):
        cs = slice(c * MP_CH, (c + 1) * MP_CH)
        acc = _dot(h, wqk_ref[:, cs])
        xbuf_ref[0:SUBLANES, :] = carry_ref[:, cs]
        xbuf_ref[SUBLANES:SUBLANES + tm, :] = acc
        carry_ref[:, cs] = acc[tm - SUBLANES:tm, :]
        y = cb_ref[:, cs] + cw_ref[CONV_K - 1:CONV_K, cs] * acc
        for j in range(CONV_K - 1):
            back = CONV_K - 1 - j
            y = y + cw_ref[j:j + 1, cs] * xbuf_ref[SUBLANES - back:SUBLANES - back + tm, :]
        y = _silu(y)
        if c < M_W // MP_CH:
            q_ref[:, cs] = y.astype(BF16)
        else:
            ks = slice(c * MP_CH - M_W, (c + 1) * MP_CH - M_W)
            k_ref[:, ks] = (y * (M_DH ** -0.5)).astype(BF16)

    for c in range(M_W // MP_CH):
        cs = slice(c * MP_CH, (c + 1) * MP_CH)
        vt_ref[0, 0, cs, :] = _dot_nt(wvt_ref[cs, :], h).astype(BF16)
        o = _dot(h, wo_ref[:, cs])
        z = _dot(h, wz_ref[:, cs])
        g_ref[:, cs] = (_sigmoid(o) * _silu(z)).astype(BF16)

    if_ref[...] = _dot(h, wif_ref[...])


def _mlstm_proj(x2, norm_g, wqk, conv_w, conv_b, wvt, wo, wz, wif, batch, seq):
    T = x2.shape[0]
    tm = SEQ_TILE
    per_seq = seq // tm
    row = lambda w: pl.BlockSpec((tm, w), lambda i: (i, 0))
    out_shape = (
        jax.ShapeDtypeStruct((T, D_MODEL), BF16),
        jax.ShapeDtypeStruct((T, M_W), BF16),
        jax.ShapeDtypeStruct((T, M_W), BF16),
        jax.ShapeDtypeStruct((batch, per_seq, M_W, tm), BF16),
        jax.ShapeDtypeStruct((T, M_W), BF16),
        jax.ShapeDtypeStruct((T, LANES), F32),
    )
    vt_spec = pl.BlockSpec((1, 1, M_W, tm), lambda i: (i // per_seq, i % per_seq, 0, 0))
    return pl.pallas_call(
        functools.partial(_mlstm_proj_kernel, per_seq),
        grid=(T // tm,),
        in_specs=[row(D_MODEL), _const_spec((1, D_MODEL)), _const_spec(wqk.shape),
                  _const_spec(conv_w.shape), _const_spec(conv_b.shape), _const_spec(wvt.shape),
                  _const_spec(wo.shape), _const_spec(wz.shape), _const_spec(wif.shape)],
        out_specs=(row(D_MODEL), row(M_W), row(M_W), vt_spec, row(M_W), row(LANES)),
        out_shape=out_shape,
        scratch_shapes=[pltpu.VMEM((SUBLANES, 2 * M_W), F32),
                        pltpu.VMEM((tm + SUBLANES, MP_CH), F32)],
        compiler_params=_params("arbitrary"),
        name="mlstm_proj",
    )(x2, norm_g, wqk, conv_w, conv_b, wvt, wo, wz, wif)


def _lane_scan(x, op, fill):
    n = x.shape[-1]
    lane = lax.broadcasted_iota(jnp.int32, x.shape, x.ndim - 1)
    d = 1
    while d < n:
        shifted = pltpu.roll(x, d, x.ndim - 1)
        x = op(x, jnp.where(lane >= d, shifted, fill))
        d *= 2
    return x


def _gate_scan_kernel(if_ref, bias_ref, row_ref, col_ref):
    S = if_ref.shape[1]
    t = SEQ_TILE
    pre = if_ref[0] + bias_ref[...]
    t8 = pre.T[0:SUBLANES, :]
    lf = jnp.minimum(t8, 0.0) - jnp.log(1.0 + jnp.exp(-jnp.abs(t8)))
    b = _lane_scan(lf, jnp.add, 0.0)
    b = pltpu.roll(b, M_HEADS, 0)
    a = t8 - b
    mx = jnp.maximum(_lane_scan(a, jnp.maximum, -jnp.inf), 0.0)
    nb = -b - mx
    sub = lax.broadcasted_iota(jnp.int32, (SUBLANES, S), 0)
    zeros = jnp.zeros((LANES - SUBLANES, S), F32)
    n_h = HEADS_PER_STEP
    for p in range(M_HEADS // n_h):
        up = (SUBLANES - n_h * p) % SUBLANES
        m_grp = pltpu.roll(mx, up, 0) if up else mx
        nb_grp = pltpu.roll(nb, (up + n_h) % SUBLANES, 0)
        stack = jnp.where(sub < n_h, m_grp, nb_grp)
        for j in range(S // t):
            row_ref[0, p, j] = stack[:, j * t:(j + 1) * t]
        a_grp = pltpu.roll(a, up, 0) if up else a
        col_ref[0, p] = jnp.concatenate([a_grp, zeros], axis=0).T


def _gate_scan(pre_if, bias_row, batch, seq):
    n_grp = M_HEADS // HEADS_PER_STEP
    n_tile = seq // SEQ_TILE
    return pl.pallas_call(
        _gate_scan_kernel,
        grid=(batch,),
        in_specs=[pl.BlockSpec((1, seq, LANES), lambda b: (b, 0, 0)), _const_spec((1, LANES))],
        out_specs=(pl.BlockSpec((1, n_grp, n_tile, SUBLANES, SEQ_TILE), lambda b: (b, 0, 0, 0, 0)),
                   pl.BlockSpec((1, n_grp, seq, LANES), lambda b: (b, 0, 0, 0))),
        out_shape=(jax.ShapeDtypeStruct((batch, n_grp, n_tile, SUBLANES, SEQ_TILE), F32),
                   jax.ShapeDtypeStruct((batch, n_grp, seq, LANES), F32)),
        compiler_params=_params("parallel"),
        name="gate_scan",
    )(pre_if.reshape(batch, seq, LANES), bias_row)


def _decay_attn_kernel(q_ref, k_ref, vt_ref, g_ref, row_ref, col_ref, gain_ref, o_ref, arep_ref):
    S = q_ref.shape[1]
    t = SEQ_TILE
    n_head = HEADS_PER_STEP
    keep = _keys_le_queries(t)
    hs = [slice(hh * M_DH, (hh + 1) * M_DH) for hh in range(n_head)]
    for hh in range(n_head):
        arep_ref[hh] = jnp.broadcast_to(col_ref[0, 0, :, hh:hh + 1], (S, LANES))

    def q_step(qi, _):
        qs = pl.ds(pl.multiple_of(qi * t, t), t)
        qh = [q_ref[0, qs, hs[hh]] for hh in range(n_head)]
        m_row = [row_ref[0, 0, qi, hh:hh + 1, :] for hh in range(n_head)]
        nb_row = [row_ref[0, 0, qi, n_head + hh:n_head + hh + 1, :] for hh in range(n_head)]

        def chunks(carry, kj, masked):
            ks = pl.ds(pl.multiple_of(kj * t, t), t)
            out = []
            for hh in range(n_head):
                den, num = carry[hh]
                st = _dot_nt(k_ref[0, ks, hs[hh]], qh[hh])
                a = arep_ref[hh, ks, :]
                arg = jnp.concatenate([a] * (t // LANES), axis=1) - m_row[hh]
                if masked:
                    arg = jnp.where(keep, arg, -jnp.inf)
                p = st * jnp.exp(arg)
                out.append((den + jnp.sum(p, axis=0, keepdims=True),
                            num + _dot(vt_ref[0, kj, hs[hh], :], p.astype(BF16))))
            return tuple(out)

        init = tuple((jnp.zeros((1, t), F32), jnp.zeros((M_DH, t), F32)) for _ in range(n_head))
        carry = lax.fori_loop(0, qi, lambda kj, c: chunks(c, kj, False), init)
        carry = chunks(carry, qi, True)
        for hh in range(n_head):
            den, num = carry[hh]
            hv = num / jnp.maximum(jnp.abs(den), jnp.exp(nb_row[hh]))
            hv = hv * lax.rsqrt(jnp.mean(hv * hv, axis=0, keepdims=True) + EPS)
            o_ref[0, qs, hs[hh]] = (hv.T * gain_ref[:, hs[hh]]
                                    * g_ref[0, qs, hs[hh]].astype(F32)).astype(BF16)
        return 0

    lax.fori_loop(0, S // t, q_step, 0)


def _decay_attn(q, k, vt, gate, rowf, colf, gain, batch, seq):
    n_grp = M_HEADS // HEADS_PER_STEP
    n_tile = seq // SEQ_TILE
    w = HEADS_PER_STEP * M_DH
    blk = pl.BlockSpec((1, seq, w), lambda b, p: (b, 0, p))
    q3, k3, g3 = (a.reshape(batch, seq, M_W) for a in (q, k, gate))
    return pl.pallas_call(
        _decay_attn_kernel,
        grid=(batch, n_grp),
        in_specs=[blk, blk,
                  pl.BlockSpec((1, n_tile, w, SEQ_TILE), lambda b, p: (b, 0, p, 0)),
                  blk,
                  pl.BlockSpec((1, 1, n_tile, SUBLANES, SEQ_TILE), lambda b, p: (b, p, 0, 0, 0)),
                  pl.BlockSpec((1, 1, seq, LANES), lambda b, p: (b, p, 0, 0)),
                  pl.BlockSpec((1, w), lambda b, p: (0, p))],
        out_specs=blk,
        out_shape=jax.ShapeDtypeStruct((batch, seq, M_W), BF16),
        scratch_shapes=[pltpu.VMEM((HEADS_PER_STEP, seq, LANES), F32)],
        compiler_params=_params("parallel", "parallel"),
        name="decay_attn",
    )(q3, k3, vt, g3, rowf, colf, gain).reshape(batch * seq, M_W)


def _mla_proj_kernel(h_ref, pos_ref, invf_ref, sgn_ref, gq_ref, gkv_ref, wcq_ref, wckv_ref,
                     wkr_ref, waz_ref, wqn_ref, wqr_ref, wkn_ref, wvt_ref,
                     q_ref, k_ref, vt_ref, g_ref):
    h = h_ref[...]
    scale = 1.0 / math.sqrt(A_DQK)
    ang = pos_ref[...] * invf_ref[...]
    cos = jnp.cos(ang)
    sin = jnp.sin(ang) * sgn_ref[...]

    def rope(r):
        return r * cos + pltpu.roll(r, LANES // 2, 1) * sin

    cqn = _rms(_dot(h, wcq_ref[...]), gq_ref[...]).astype(BF16)
    qn = _dot(cqn, wqn_ref[...]) * scale
    qr = _dot(cqn, wqr_ref[...]) * scale
    for hd in range(A_HEADS):
        base = hd * A_QK_PAD
        ls = slice(hd * LANES, (hd + 1) * LANES)
        q_ref[:, base:base + LANES] = qn[:, ls].astype(BF16)
        q_ref[:, base + LANES:base + A_QK_PAD] = rope(qr[:, ls]).astype(BF16)

    kr = rope(_dot(h, wkr_ref[...])).astype(BF16)
    ckvn = _rms(_dot(h, wckv_ref[...]), gkv_ref[...]).astype(BF16)
    kn = _dot(ckvn, wkn_ref[...])
    for hd in range(A_HEADS):
        base = hd * A_QK_PAD
        k_ref[:, base:base + LANES] = kn[:, hd * LANES:(hd + 1) * LANES].astype(BF16)
        k_ref[:, base + LANES:base + A_QK_PAD] = kr
    vt_ref[0, 0] = _dot_nt(wvt_ref[...], ckvn).astype(BF16)
    g_ref[...] = _silu(_dot(h, waz_ref[...])).astype(BF16)


def _mla_proj(h, pos_col, invf, sgn, gq, gkv, wcq, wckv, wkr, waz, wqn, wqr, wkn, wvt, batch, seq):
    T = h.shape[0]
    tm = SEQ_TILE
    per_seq = seq // tm
    row = lambda w: pl.BlockSpec((tm, w), lambda i: (i, 0))
    consts = (invf, sgn, gq, gkv, wcq, wckv, wkr, waz, wqn, wqr, wkn, wvt)
    vt_spec = pl.BlockSpec((1, 1, A_W, tm), lambda i: (i // per_seq, i % per_seq, 0, 0))
    return pl.pallas_call(
        _mla_proj_kernel,
        grid=(T // tm,),
        in_specs=[row(D_MODEL), row(1)] + [_const_spec(c.shape) for c in consts],
        out_specs=(row(A_HEADS * A_QK_PAD), row(A_HEADS * A_QK_PAD), vt_spec, row(A_W)),
        out_shape=(jax.ShapeDtypeStruct((T, A_HEADS * A_QK_PAD), BF16),
                   jax.ShapeDtypeStruct((T, A_HEADS * A_QK_PAD), BF16),
                   jax.ShapeDtypeStruct((batch, per_seq, A_W, tm), BF16),
                   jax.ShapeDtypeStruct((T, A_W), BF16)),
        compiler_params=_params("parallel"),
        name="mla_proj",
    )(h, pos_col, *consts)


def _mla_attn_kernel(q_ref, k_ref, vt_ref, g_ref, o_ref):
    S = q_ref.shape[1]
    t = SEQ_TILE
    n_head = HEADS_PER_STEP
    keep = _keys_le_queries(t)
    hq = [slice(hh * A_QK_PAD, (hh + 1) * A_QK_PAD) for hh in range(n_head)]
    hv = [slice(hh * A_DV, (hh + 1) * A_DV) for hh in range(n_head)]

    def q_step(qi, _):
        qs = pl.ds(pl.multiple_of(qi * t, t), t)
        qh = [q_ref[0, qs, hq[hh]] for hh in range(n_head)]

        def chunks(carry, kj, masked):
            ks = pl.ds(pl.multiple_of(kj * t, t), t)
            out = []
            for hh in range(n_head):
                m, l, acc = carry[hh]
                st = _dot_nt(k_ref[0, ks, hq[hh]], qh[hh])
                if masked:
                    st = jnp.where(keep, st, -jnp.inf)
                m_new = jnp.maximum(m, jnp.max(st, axis=0, keepdims=True))
                p = jnp.exp(st - m_new)
                alpha = jnp.exp(m - m_new)
                l = alpha * l + jnp.sum(p, axis=0, keepdims=True)
                acc = alpha * acc + _dot(vt_ref[0, kj, hv[hh], :], p.astype(BF16))
                out.append((m_new, l, acc))
            return tuple(out)

        init = tuple((jnp.full((1, t), -jnp.inf, F32), jnp.zeros((1, t), F32),
                      jnp.zeros((A_DV, t), F32)) for _ in range(n_head))
        carry = lax.fori_loop(0, qi, lambda kj, c: chunks(c, kj, False), init)
        carry = chunks(carry, qi, True)
        for hh in range(n_head):
            _, l, acc = carry[hh]
            o_ref[0, qs, hv[hh]] = ((acc / l).T * g_ref[0, qs, hv[hh]].astype(F32)).astype(BF16)
        return 0

    lax.fori_loop(0, S // t, q_step, 0)


def _mla_attn(q, k, vt, gate, batch, seq):
    n_tile = seq // SEQ_TILE
    qk_blk = pl.BlockSpec((1, seq, HEADS_PER_STEP * A_QK_PAD), lambda b, p: (b, 0, p))
    v_blk = pl.BlockSpec((1, seq, HEADS_PER_STEP * A_DV), lambda b, p: (b, 0, p))
    vt_blk = pl.BlockSpec((1, n_tile, HEADS_PER_STEP * A_DV, SEQ_TILE), lambda b, p: (b, 0, p, 0))
    q3 = q.reshape(batch, seq, A_HEADS * A_QK_PAD)
    k3 = k.reshape(batch, seq, A_HEADS * A_QK_PAD)
    g3 = gate.reshape(batch, seq, A_W)
    return pl.pallas_call(
        _mla_attn_kernel,
        grid=(batch, A_HEADS // HEADS_PER_STEP),
        in_specs=[qk_blk, qk_blk, vt_blk, v_blk],
        out_specs=v_blk,
        out_shape=jax.ShapeDtypeStruct((batch, seq, A_W), BF16),
        compiler_params=_params("parallel", "parallel"),
        name="mla_attn",
    )(q3, k3, vt, g3).reshape(batch * seq, A_W)


def _mem_kv_kernel(mem_ref, g_ref, w_ref, kv_ref):
    kv_ref[0] = _dot(_rms(mem_ref[0], g_ref[...]).astype(BF16), w_ref[...]).astype(BF16)


def _mem_kv(mem, gain, w):
    batch = mem.shape[0]
    return pl.pallas_call(
        _mem_kv_kernel,
        grid=(batch,),
        in_specs=[pl.BlockSpec((1, N_MEM, D_MODEL), lambda b: (b, 0, 0)),
                  _const_spec((1, D_MODEL)), _const_spec(w.shape)],
        out_specs=pl.BlockSpec((1, N_MEM, 2 * C_W), lambda b: (b, 0, 0)),
        out_shape=jax.ShapeDtypeStruct((batch, N_MEM, 2 * C_W), BF16),
        compiler_params=_params("parallel"),
        name="mem_kv",
    )(mem, gain, w)


def _mem_attn_kernel(h_ref, kv_ref, wq_ref, wz_ref, o_ref):
    h = h_ref[...]
    cq = (_dot(h, wq_ref[...]) * (C_DH ** -0.5)).astype(BF16)
    cz = _dot(h, wz_ref[...])
    for hd in range(C_HEADS):
        hs = slice(hd * C_DH, (hd + 1) * C_DH)
        s = _dot_nt(cq[:, hs], kv_ref[0, :, hs])
        p = jnp.exp(s - jnp.max(s, axis=-1, keepdims=True))
        l = jnp.sum(p, axis=-1, keepdims=True)
        o = _dot(p.astype(BF16), kv_ref[0, :, C_W + hd * C_DH:C_W + (hd + 1) * C_DH]) / l
        o_ref[:, hs] = (o * _silu(cz[:, hs])).astype(BF16)


def _mem_attn(h, kv, wq, wz, seq):
    T = h.shape[0]
    tm = SEQ_TILE
    per_seq = seq // tm
    return pl.pallas_call(
        _mem_attn_kernel,
        grid=(T // tm,),
        in_specs=[pl.BlockSpec((tm, D_MODEL), lambda i: (i, 0)),
                  pl.BlockSpec((1, N_MEM, 2 * C_W), lambda i: (i // per_seq, 0, 0)),
                  _const_spec(wq.shape), _const_spec(wz.shape)],
        out_specs=pl.BlockSpec((tm, C_W), lambda i: (i, 0)),
        out_shape=jax.ShapeDtypeStruct((T, C_W), BF16),
        compiler_params=_params("parallel"),
        name="mem_attn",
    )(h, kv, wq, wz)


MG_TM = 1024
MG_TN = 256


def _merge_kernel(h_ref, hm_ref, ha_ref, hc_ref, wgm_ref, wga_ref, wgc_ref,
                  wbm_ref, wba_ref, wbc_ref, o_ref):
    h = h_ref[...]
    acc = _sigmoid(_dot(h, wgm_ref[...])) * _dot(hm_ref[...], wbm_ref[...])
    acc = acc + _sigmoid(_dot(h, wga_ref[...])) * _dot(ha_ref[...], wba_ref[...])
    acc = acc + _sigmoid(_dot(h, wgc_ref[...])) * _dot(hc_ref[...], wbc_ref[...])
    o_ref[...] = acc.astype(BF16)


def _merge(h, hm, ha, hc, wgm, wga, wgc, wbm, wba, wbc):
    T = h.shape[0]
    tm, tn = MG_TM, MG_TN
    row = lambda w: pl.BlockSpec((tm, w), lambda i, j: (i, 0))
    col = lambda kdim: pl.BlockSpec((kdim, tn), lambda i, j: (0, j))
    return pl.pallas_call(
        _merge_kernel,
        grid=(T // tm, D_MODEL // tn),
        in_specs=[row(D_MODEL), row(M_W), row(A_W), row(C_W),
                  col(D_MODEL), col(D_MODEL), col(D_MODEL), col(M_W), col(A_W), col(C_W)],
        out_specs=pl.BlockSpec((tm, tn), lambda i, j: (i, j)),
        out_shape=jax.ShapeDtypeStruct((T, D_MODEL), BF16),
        compiler_params=_params("parallel", "arbitrary"),
        name="merge",
    )(h, hm, ha, hc, wgm, wga, wgc, wbm, wba, wbc)


def _out_proj_kernel(final, x_ref, m_ref, w_ref, g_ref, o_ref):
    y = x_ref[...] + _dot(m_ref[...], w_ref[...])
    o_ref[...] = _rms(y, g_ref[...]) if final else y


def _out_proj(x2, merged, w, gain, final):
    T = x2.shape[0]
    tm = SEQ_TILE
    row = pl.BlockSpec((tm, D_MODEL), lambda i: (i, 0))
    return pl.pallas_call(
        functools.partial(_out_proj_kernel, final),
        grid=(T // tm,),
        in_specs=[row, row, _const_spec(w.shape), _const_spec((1, D_MODEL))],
        out_specs=row,
        out_shape=jax.ShapeDtypeStruct((T, D_MODEL), F32),
        compiler_params=_params("parallel"),
        name="out_proj",
    )(x2, merged, w, gain)


def _rope_lanes(a):
    half = A_ROPE // 2
    z = jnp.zeros(a.shape[:-1] + (half,), a.dtype)
    return jnp.concatenate([a[..., :half], z, a[..., half:], z], axis=-1)


def _layer(x2, pos_col, kv_mem_in, l, final, batch, seq, w_in, b_igate, b_fgate, conv_w, conv_b,
           mh_norm, cq_norm, w_uq, ckv_norm, w_ukv, mem_norm, w_mem_kv, w_br_m, w_br_a, w_br_c,
           w_out, norm, final_norm):
    offs = [0]
    for s in IN_SPLITS:
        offs.append(offs[-1] + s)
    wl = w_in[l]
    seg = lambda i: wl[:, offs[i]:offs[i + 1]]
    bf = lambda a: a.astype(BF16)
    row = lambda a: a.reshape(1, -1).astype(F32)

    wqk, wvt, wo, wz = bf(seg(0)), bf(seg(1).T), bf(seg(2)), bf(seg(3))
    wif = bf(jnp.pad(wl[:, offs[4]:offs[6]], ((0, 0), (0, LANES - 2 * M_HEADS))))
    wcq, wckv, waz = bf(seg(6)), bf(seg(7)), bf(seg(9))
    wkr = bf(_rope_lanes(seg(8)))
    wcqm, wczm = bf(seg(10)), bf(seg(11))
    gates = seg(12)
    wgm, wga, wgc = (bf(gates[:, i * D_MODEL:(i + 1) * D_MODEL]) for i in range(3))
    uq = w_uq[l].reshape(Q_LORA, A_HEADS, A_DQK)
    wqn = bf(uq[:, :, :A_NOPE].reshape(Q_LORA, A_HEADS * A_NOPE))
    wqr = bf(_rope_lanes(uq[:, :, A_NOPE:]).reshape(Q_LORA, A_HEADS * LANES))
    ukv = w_ukv[l].reshape(KV_LORA, A_HEADS, A_NOPE + A_DV)
    wkn = bf(ukv[:, :, :A_NOPE].reshape(KV_LORA, A_HEADS * A_NOPE))
    wvvt = bf(ukv[:, :, A_NOPE:].reshape(KV_LORA, A_HEADS * A_DV).T)
    bias_row = jnp.pad(jnp.concatenate([b_igate[l], b_fgate[l]]).astype(F32),
                       (0, LANES - 2 * M_HEADS)).reshape(1, LANES)
    inv_freq = ROPE_THETA ** (-jnp.arange(0, A_ROPE, 2, dtype=F32) / A_ROPE)
    invf = _rope_lanes(jnp.concatenate([inv_freq, inv_freq])).reshape(1, LANES)
    ones = jnp.ones((A_ROPE // 2,), F32)
    sgn = _rope_lanes(jnp.concatenate([-ones, ones])).reshape(1, LANES)

    h, mq, mk, mvt, mgate, pre_if = _mlstm_proj(x2, row(norm[l]), wqk, conv_w[l].astype(F32),
                                                row(conv_b[l]), wvt, wo, wz, wif, batch, seq)
    rowf, colf = _gate_scan(pre_if, bias_row, batch, seq)
    hm = _decay_attn(mq, mk, mvt, mgate, rowf, colf, row(mh_norm[l]), batch, seq)

    aq, ak, avt, agate = _mla_proj(h, pos_col, invf, sgn, row(cq_norm[l]), row(ckv_norm[l]),
                                   wcq, wckv, wkr, waz, wqn, wqr, wkn, wvvt, batch, seq)
    ha = _mla_attn(aq, ak, avt, agate, batch, seq)

    kv_mem = _mem_kv(kv_mem_in, row(mem_norm[l]), bf(w_mem_kv[l]))
    hc = _mem_attn(h, kv_mem, wcqm, wczm, seq)

    merged = _merge(h, hm, ha, hc, wgm, wga, wgc, bf(w_br_m[l]), bf(w_br_a[l]), bf(w_br_c[l]))
    return _out_proj(x2, merged, bf(w_out[l]), row(final_norm), final)


def kernel(x, mem, positions, w_in, b_igate, b_fgate, conv_w, conv_b, mh_norm, cq_norm, w_uq,
           ckv_norm, w_ukv, mem_norm, w_mem_kv, w_br_m, w_br_a, w_br_c, w_out, norm, final_norm):
    batch, seq, d = x.shape
    depth = w_in.shape[0]
    assert d == D_MODEL and seq % MG_TM == 0 and w_in.shape[2] == sum(IN_SPLITS)
    x2 = x.reshape(batch * seq, d)
    pos_col = positions.astype(F32).reshape(batch * seq, 1)
    for l in range(depth):
        x2 = _layer(x2, pos_col, mem, l, l == depth - 1, batch, seq, w_in, b_igate, b_fgate,
                    conv_w, conv_b, mh_norm, cq_norm, w_uq, ckv_norm, w_ukv, mem_norm, w_mem_kv,
                    w_br_m, w_br_a, w_br_c, w_out, norm, final_norm)
    return x2.reshape(batch, seq, d)
```

```python
import functools
import math

import jax
import jax.numpy as jnp
from jax import lax
from jax.experimental import pallas as pl
from jax.experimental.pallas import tpu as pltpu

F32 = jnp.float32
BF16 = jnp.bfloat16

D_MODEL = 2048
M_HEADS, M_DH = 4, 256
M_W = M_HEADS * M_DH
CONV_K = 4
A_HEADS, A_NOPE, A_ROPE, A_DV = 8, 128, 64, 128
A_DQK = A_NOPE + A_ROPE
A_W = A_HEADS * A_DV
Q_LORA = KV_LORA = 512
ROPE_THETA = 10000.0
N_MEM = 256
C_HEADS, C_DH = 4, 256
C_W = C_HEADS * C_DH
EPS = 1e-6

LANES = 128
SUBLANES = 8
A_QK_PAD = 2 * LANES
VMEM_LIMIT = 56 * 1024 * 1024

SEQ_TILE = 512
HEADS_PER_STEP = 2

IN_SPLITS = (2 * M_W, M_W, M_W, M_W, M_HEADS, M_HEADS, Q_LORA, KV_LORA, A_ROPE, A_W, C_W, C_W,
             3 * D_MODEL)


def _const_spec(shape):
    nd = len(shape)
    return pl.BlockSpec(shape, lambda *_: (0,) * nd, pipeline_mode=pl.Buffered(1))


def _params(*sem):
    return pltpu.CompilerParams(dimension_semantics=sem, vmem_limit_bytes=VMEM_LIMIT)


def _dot(a, b):
    return jnp.dot(a, b, preferred_element_type=F32)


def _dot_nt(a, b):
    return lax.dot_general(a, b, (((1,), (1,)), ((), ())), preferred_element_type=F32)


def _rms(x, g):
    return x * lax.rsqrt(jnp.mean(x * x, axis=-1, keepdims=True) + EPS) * g


def _sigmoid(x):
    return 1.0 / (1.0 + jnp.exp(-x))


def _silu(x):
    return x * _sigmoid(x)


def _keys_le_queries(t):
    return lax.broadcasted_iota(jnp.int32, (t, t), 0) <= lax.broadcasted_iota(jnp.int32, (t, t), 1)


MP_CH = 512


def _mlstm_proj_kernel(tiles_per_seq, x_ref, ng_ref, wqk_ref, cw_ref, cb_ref, wvt_ref, wo_ref,
                       wz_ref, wif_ref, h_ref, q_ref, k_ref, vt_ref, g_ref, if_ref,
                       xbuf_ref):
    tm = x_ref.shape[0]

    @pl.when(pl.program_id(0) % tiles_per_seq == 0)
    def _():
        xbuf_ref[0:SUBLANES, :] = jnp.zeros((SUBLANES, 2 * M_W), F32)

    h = _rms(x_ref[...], ng_ref[...]).astype(BF16)
    h_ref[...] = h

    for c in range(2 * M_W // MP_CH):
        cs = slice(c * MP_CH, (c + 1) * MP_CH)
        acc = _dot_nt(h, wqk_ref[cs, :])
        xbuf_ref[SUBLANES:SUBLANES + tm, cs] = acc
        y = cb_ref[:, cs] + cw_ref[CONV_K - 1:CONV_K, cs] * acc
        for j in range(CONV_K - 1):
            back = CONV_K - 1 - j
            y = y + cw_ref[j:j + 1, cs] * xbuf_ref[SUBLANES - back:SUBLANES - back + tm, cs]
        xbuf_ref[0:SUBLANES, cs] = acc[tm - SUBLANES:tm, :]
        y = _silu(y)
        if c < M_W // MP_CH:
            q_ref[:, cs] = y.astype(BF16)
        else:
            ks = slice(c * MP_CH - M_W, (c + 1) * MP_CH - M_W)
            k_ref[:, ks] = (y * (M_DH ** -0.5)).astype(BF16)

    for c in range(M_W // MP_CH):
        cs = slice(c * MP_CH, (c + 1) * MP_CH)
        vt_ref[0, 0, cs, :] = _dot_nt(wvt_ref[cs, :], h).astype(BF16)
        o = _dot_nt(h, wo_ref[cs, :])
        z = _dot_nt(h, wz_ref[cs, :])
        g_ref[:, cs] = (_sigmoid(o) * _silu(z)).astype(BF16)

    if_ref[...] = _dot_nt(h, wif_ref[...])


def _mlstm_proj(x2, norm_g, wqk, conv_w, conv_b, wvt, wo, wz, wif, batch, seq):
    T = x2.shape[0]
    tm = SEQ_TILE
    per_seq = seq // tm
    row = lambda w: pl.BlockSpec((tm, w), lambda i: (i, 0))
    out_shape = (
        jax.ShapeDtypeStruct((T, D_MODEL), BF16),
        jax.ShapeDtypeStruct((T, M_W), BF16),
        jax.ShapeDtypeStruct((T, M_W), BF16),
        jax.ShapeDtypeStruct((batch, per_seq, M_W, tm), BF16),
        jax.ShapeDtypeStruct((T, M_W), BF16),
        jax.ShapeDtypeStruct((T, LANES), F32),
    )
    vt_spec = pl.BlockSpec((1, 1, M_W, tm), lambda i: (i // per_seq, i % per_seq, 0, 0))
    return pl.pallas_call(
        functools.partial(_mlstm_proj_kernel, per_seq),
        grid=(T // tm,),
        in_specs=[row(D_MODEL), _const_spec((1, D_MODEL)), _const_spec(wqk.shape),
                  _const_spec(conv_w.shape), _const_spec(conv_b.shape), _const_spec(wvt.shape),
                  _const_spec(wo.shape), _const_spec(wz.shape), _const_spec(wif.shape)],
        out_specs=(row(D_MODEL), row(M_W), row(M_W), vt_spec, row(M_W), row(LANES)),
        out_shape=out_shape,
        scratch_shapes=[pltpu.VMEM((tm + SUBLANES, 2 * M_W), F32)],
        compiler_params=_params("arbitrary"),
        name="mlstm_proj",
    )(x2, norm_g, wqk, conv_w, conv_b, wvt, wo, wz, wif)


def _lane_scan(x, op, fill):
    n = x.shape[-1]
    lane = lax.broadcasted_iota(jnp.int32, x.shape, x.ndim - 1)
    d = 1
    while d < n:
        shifted = pltpu.roll(x, d, x.ndim - 1)
        x = op(x, jnp.where(lane >= d, shifted, fill))
        d *= 2
    return x


def _gate_scan_kernel(if_ref, bias_ref, row_ref, col_ref):
    S = if_ref.shape[1]
    t = SEQ_TILE
    pre = if_ref[0] + bias_ref[...]
    t8 = pre.T[0:SUBLANES, :]
    lf = jnp.minimum(t8, 0.0) - jnp.log(1.0 + jnp.exp(-jnp.abs(t8)))
    b = _lane_scan(lf, jnp.add, 0.0)
    b = pltpu.roll(b, M_HEADS, 0)
    a = t8 - b
    mx = jnp.maximum(_lane_scan(a, jnp.maximum, -jnp.inf), 0.0)
    nb = -b - mx
    sub = lax.broadcasted_iota(jnp.int32, (SUBLANES, S), 0)
    zeros = jnp.zeros((LANES - SUBLANES, S), F32)
    n_h = HEADS_PER_STEP
    for p in range(M_HEADS // n_h):
        up = (SUBLANES - n_h * p) % SUBLANES
        m_grp = pltpu.roll(mx, up, 0) if up else mx
        nb_grp = pltpu.roll(nb, (up + n_h) % SUBLANES, 0)
        stack = jnp.where(sub < n_h, m_grp, nb_grp)
        for j in range(S // t):
            row_ref[0, p, j] = stack[:, j * t:(j + 1) * t]
        a_grp = pltpu.roll(a, up, 0) if up else a
        col_ref[0, p] = jnp.concatenate([a_grp, zeros], axis=0).T


def _gate_scan(pre_if, bias_row, batch, seq):
    n_grp = M_HEADS // HEADS_PER_STEP
    n_tile = seq // SEQ_TILE
    return pl.pallas_call(
        _gate_scan_kernel,
        grid=(batch,),
        in_specs=[pl.BlockSpec((1, seq, LANES), lambda b: (b, 0, 0)), _const_spec((1, LANES))],
        out_specs=(pl.BlockSpec((1, n_grp, n_tile, SUBLANES, SEQ_TILE), lambda b: (b, 0, 0, 0, 0)),
                   pl.BlockSpec((1, n_grp, seq, LANES), lambda b: (b, 0, 0, 0))),
        out_shape=(jax.ShapeDtypeStruct((batch, n_grp, n_tile, SUBLANES, SEQ_TILE), F32),
                   jax.ShapeDtypeStruct((batch, n_grp, seq, LANES), F32)),
        compiler_params=_params("parallel"),
        name="gate_scan",
    )(pre_if.reshape(batch, seq, LANES), bias_row)


def _decay_attn_kernel(q_ref, k_ref, vt_ref, g_ref, row_ref, col_ref, gain_ref, o_ref,
                       arep_ref, fac_ref):
    S = q_ref.shape[1]
    t = SEQ_TILE
    n_head = HEADS_PER_STEP
    keep = _keys_le_queries(t)
    hs = [slice(hh * M_DH, (hh + 1) * M_DH) for hh in range(n_head)]
    lane_tiles = t // LANES
    widen = lambda a: jnp.concatenate([a] * lane_tiles, axis=1)

    c_rep = [[None] * (S // t) for _ in range(n_head)]
    for hh in range(n_head):
        arep_ref[hh] = jnp.broadcast_to(col_ref[0, 0, :, hh:hh + 1], (S, LANES))
        for kj in range(S // t - 1):
            a = arep_ref[hh, kj * t:(kj + 1) * t, :]
            c_rep[hh][kj] = jnp.max(a, axis=0, keepdims=True)
            fac_ref[hh, kj * t:(kj + 1) * t, :] = jnp.exp(a - c_rep[hh][kj])

    for qi in range(S // t):
        qs = slice(qi * t, (qi + 1) * t)
        for hh in range(n_head):
            q = q_ref[0, qs, hs[hh]]
            m_row = row_ref[0, 0, qi, hh:hh + 1, :]
            nb_row = row_ref[0, 0, qi, n_head + hh:n_head + hh + 1, :]
            den = num = None
            for kj in range(qi + 1):
                ks = slice(kj * t, (kj + 1) * t)
                st = _dot_nt(k_ref[0, ks, hs[hh]], q)
                if kj == qi:
                    arg = jnp.where(keep, widen(arep_ref[hh, ks, :]) - m_row, -jnp.inf)
                    p = st * jnp.exp(arg)
                    d_blk = jnp.sum(p, axis=0, keepdims=True)
                    n_blk = _dot(vt_ref[0, kj, hs[hh], :], p.astype(BF16))
                else:
                    p = st * widen(fac_ref[hh, ks, :])
                    qfac = jnp.exp(widen(c_rep[hh][kj]) - m_row)
                    d_blk = qfac * jnp.sum(p, axis=0, keepdims=True)
                    n_blk = qfac * _dot(vt_ref[0, kj, hs[hh], :], p.astype(BF16))
                den, num = (d_blk, n_blk) if den is None else (den + d_blk, num + n_blk)
            hv = num / jnp.maximum(jnp.abs(den), jnp.exp(nb_row))
            hv = hv * lax.rsqrt(jnp.mean(hv * hv, axis=0, keepdims=True) + EPS)
            o_ref[0, qs, hs[hh]] = (hv.T * gain_ref[:, hs[hh]]
                                    * g_ref[0, qs, hs[hh]].astype(F32)).astype(BF16)


def _decay_attn(q, k, vt, gate, rowf, colf, gain, batch, seq):
    n_grp = M_HEADS // HEADS_PER_STEP
    n_tile = seq // SEQ_TILE
    w = HEADS_PER_STEP * M_DH
    blk = pl.BlockSpec((1, seq, w), lambda b, p: (b, 0, p))
    q3, k3, g3 = (a.reshape(batch, seq, M_W) for a in (q, k, gate))
    return pl.pallas_call(
        _decay_attn_kernel,
        grid=(batch, n_grp),
        in_specs=[blk, blk,
                  pl.BlockSpec((1, n_tile, w, SEQ_TILE), lambda b, p: (b, 0, p, 0)),
                  blk,
                  pl.BlockSpec((1, 1, n_tile, SUBLANES, SEQ_TILE), lambda b, p: (b, p, 0, 0, 0)),
                  pl.BlockSpec((1, 1, seq, LANES), lambda b, p: (b, p, 0, 0)),
                  pl.BlockSpec((1, w), lambda b, p: (0, p))],
        out_specs=blk,
        out_shape=jax.ShapeDtypeStruct((batch, seq, M_W), BF16),
        scratch_shapes=[pltpu.VMEM((HEADS_PER_STEP, seq, LANES), F32),
                        pltpu.VMEM((HEADS_PER_STEP, seq - SEQ_TILE, LANES), F32)],
        compiler_params=_params("parallel", "parallel"),
        name="decay_attn",
    )(q3, k3, vt, g3, rowf, colf, gain).reshape(batch * seq, M_W)


def _mla_proj_kernel(h_ref, pos_ref, invf_ref, sgn_ref, gq_ref, gkv_ref, wcq_ref, wckv_ref,
                     wkr_ref, waz_ref, wqn_ref, wqr_ref, wkn_ref, wvt_ref,
                     q_ref, k_ref, vt_ref, g_ref):
    h = h_ref[...]
    scale = math.log2(math.e) / math.sqrt(A_DQK)
    ang = pos_ref[...] * invf_ref[...]
    cos = jnp.cos(ang)
    sin = jnp.sin(ang) * sgn_ref[...]

    def rope(r):
        return r * cos + pltpu.roll(r, LANES // 2, 1) * sin

    cqn = _rms(_dot_nt(h, wcq_ref[...]), gq_ref[...]).astype(BF16)
    qn = _dot(cqn, wqn_ref[...]) * scale
    qr = _dot(cqn, wqr_ref[...]) * scale
    for hd in range(A_HEADS):
        base = hd * A_QK_PAD
        ls = slice(hd * LANES, (hd + 1) * LANES)
        q_ref[:, base:base + LANES] = qn[:, ls].astype(BF16)
        q_ref[:, base + LANES:base + A_QK_PAD] = rope(qr[:, ls]).astype(BF16)

    kr = rope(_dot_nt(h, wkr_ref[...])).astype(BF16)
    ckvn = _rms(_dot_nt(h, wckv_ref[...]), gkv_ref[...]).astype(BF16)
    kn = _dot(ckvn, wkn_ref[...])
    for hd in range(A_HEADS):
        base = hd * A_QK_PAD
        k_ref[:, base:base + LANES] = kn[:, hd * LANES:(hd + 1) * LANES].astype(BF16)
        k_ref[:, base + LANES:base + A_QK_PAD] = kr
    vt_ref[0, 0] = _dot_nt(wvt_ref[...], ckvn).astype(BF16)
    g_ref[...] = _silu(_dot_nt(h, waz_ref[...])).astype(BF16)


def _mla_proj(h, pos_col, invf, sgn, gq, gkv, wcq, wckv, wkr, waz, wqn, wqr, wkn, wvt, batch, seq):
    T = h.shape[0]
    tm = SEQ_TILE
    per_seq = seq // tm
    row = lambda w: pl.BlockSpec((tm, w), lambda i: (i, 0))
    consts = (invf, sgn, gq, gkv, wcq, wckv, wkr, waz, wqn, wqr, wkn, wvt)
    vt_spec = pl.BlockSpec((1, 1, A_W, tm), lambda i: (i // per_seq, i % per_seq, 0, 0))
    return pl.pallas_call(
        _mla_proj_kernel,
        grid=(T // tm,),
        in_specs=[row(D_MODEL), row(1)] + [_const_spec(c.shape) for c in consts],
        out_specs=(row(A_HEADS * A_QK_PAD), row(A_HEADS * A_QK_PAD), vt_spec, row(A_W)),
        out_shape=(jax.ShapeDtypeStruct((T, A_HEADS * A_QK_PAD), BF16),
                   jax.ShapeDtypeStruct((T, A_HEADS * A_QK_PAD), BF16),
                   jax.ShapeDtypeStruct((batch, per_seq, A_W, tm), BF16),
                   jax.ShapeDtypeStruct((T, A_W), BF16)),
        compiler_params=_params("parallel"),
        name="mla_proj",
    )(h, pos_col, *consts)


def _mla_attn_kernel(q_ref, k_ref, vt_ref, g_ref, o_ref):
    S = q_ref.shape[1]
    t = SEQ_TILE
    n_head = HEADS_PER_STEP
    keep = _keys_le_queries(t)
    hq = [slice(hh * A_QK_PAD, (hh + 1) * A_QK_PAD) for hh in range(n_head)]
    hv = [slice(hh * A_DV, (hh + 1) * A_DV) for hh in range(n_head)]

    for qi in range(S // t):
        qs = slice(qi * t, (qi + 1) * t)
        for hh in range(n_head):
            q = q_ref[0, qs, hq[hh]]
            st = [_dot_nt(k_ref[0, kj * t:(kj + 1) * t, hq[hh]], q) for kj in range(qi + 1)]
            st[qi] = jnp.where(keep, st[qi], -jnp.inf)
            m = functools.reduce(jnp.maximum, [jnp.max(s, axis=0, keepdims=True) for s in st])
            l = acc = None
            for kj in range(qi + 1):
                p = jnp.exp2(st[kj] - m)
                l_blk = jnp.sum(p, axis=0, keepdims=True)
                pv = _dot(vt_ref[0, kj, hv[hh], :], p.astype(BF16))
                l, acc = (l_blk, pv) if l is None else (l + l_blk, acc + pv)
            o_ref[0, qs, hv[hh]] = ((acc / l).T * g_ref[0, qs, hv[hh]].astype(F32)).astype(BF16)


def _mla_attn(q, k, vt, gate, batch, seq):
    n_tile = seq // SEQ_TILE
    qk_blk = pl.BlockSpec((1, seq, HEADS_PER_STEP * A_QK_PAD), lambda b, p: (b, 0, p))
    v_blk = pl.BlockSpec((1, seq, HEADS_PER_STEP * A_DV), lambda b, p: (b, 0, p))
    vt_blk = pl.BlockSpec((1, n_tile, HEADS_PER_STEP * A_DV, SEQ_TILE), lambda b, p: (b, 0, p, 0))
    q3 = q.reshape(batch, seq, A_HEADS * A_QK_PAD)
    k3 = k.reshape(batch, seq, A_HEADS * A_QK_PAD)
    g3 = gate.reshape(batch, seq, A_W)
    return pl.pallas_call(
        _mla_attn_kernel,
        grid=(batch, A_HEADS // HEADS_PER_STEP),
        in_specs=[qk_blk, qk_blk, vt_blk, v_blk],
        out_specs=v_blk,
        out_shape=jax.ShapeDtypeStruct((batch, seq, A_W), BF16),
        compiler_params=_params("parallel", "parallel"),
        name="mla_attn",
    )(q3, k3, vt, g3).reshape(batch * seq, A_W)


def _mem_kv_kernel(mem_ref, g_ref, w_ref, kv_ref):
    kv_ref[0] = _dot(_rms(mem_ref[0], g_ref[...]).astype(BF16), w_ref[...]).astype(BF16)


def _mem_kv(mem, gain, w):
    batch = mem.shape[0]
    return pl.pallas_call(
        _mem_kv_kernel,
        grid=(batch,),
        in_specs=[pl.BlockSpec((1, N_MEM, D_MODEL), lambda b: (b, 0, 0)),
                  _const_spec((1, D_MODEL)), _const_spec(w.shape)],
        out_specs=pl.BlockSpec((1, N_MEM, 2 * C_W), lambda b: (b, 0, 0)),
        out_shape=jax.ShapeDtypeStruct((batch, N_MEM, 2 * C_W), BF16),
        compiler_params=_params("parallel"),
        name="mem_kv",
    )(mem, gain, w)


def _mem_attn_kernel(h_ref, kv_ref, wq_ref, wz_ref, o_ref):
    h = h_ref[...]
    cq = (_dot_nt(h, wq_ref[...]) * (C_DH ** -0.5)).astype(BF16)
    cz = _dot_nt(h, wz_ref[...])
    for hd in range(C_HEADS):
        hs = slice(hd * C_DH, (hd + 1) * C_DH)
        s = _dot_nt(cq[:, hs], kv_ref[0, :, hs])
        p = jnp.exp(s - jnp.max(s, axis=-1, keepdims=True))
        l = jnp.sum(p, axis=-1, keepdims=True)
        o = _dot(p.astype(BF16), kv_ref[0, :, C_W + hd * C_DH:C_W + (hd + 1) * C_DH]) / l
        o_ref[:, hs] = (o * _silu(cz[:, hs])).astype(BF16)


def _mem_attn(h, kv, wq, wz, seq):
    T = h.shape[0]
    tm = SEQ_TILE
    per_seq = seq // tm
    return pl.pallas_call(
        _mem_attn_kernel,
        grid=(T // tm,),
        in_specs=[pl.BlockSpec((tm, D_MODEL), lambda i: (i, 0)),
                  pl.BlockSpec((1, N_MEM, 2 * C_W), lambda i: (i // per_seq, 0, 0)),
                  _const_spec(wq.shape), _const_spec(wz.shape)],
        out_specs=pl.BlockSpec((tm, C_W), lambda i: (i, 0)),
        out_shape=jax.ShapeDtypeStruct((T, C_W), BF16),
        compiler_params=_params("parallel"),
        name="mem_attn",
    )(h, kv, wq, wz)


MG_TM = 1024
MG_TN = 256


def _merge_kernel(h_ref, hm_ref, ha_ref, hc_ref, wgm_ref, wga_ref, wgc_ref,
                  wbm_ref, wba_ref, wbc_ref, o_ref):
    h = h_ref[...]
    acc = _sigmoid(_dot_nt(h, wgm_ref[...])) * _dot(hm_ref[...], wbm_ref[...])
    acc = acc + _sigmoid(_dot_nt(h, wga_ref[...])) * _dot(ha_ref[...], wba_ref[...])
    acc = acc + _sigmoid(_dot_nt(h, wgc_ref[...])) * _dot(hc_ref[...], wbc_ref[...])
    o_ref[...] = acc.astype(BF16)


def _merge(h, hm, ha, hc, wgm, wga, wgc, wbm, wba, wbc):
    T = h.shape[0]
    tm, tn = MG_TM, MG_TN
    row = lambda w: pl.BlockSpec((tm, w), lambda i, j: (i, 0))
    col = lambda kdim: pl.BlockSpec((kdim, tn), lambda i, j: (0, j))
    gate = pl.BlockSpec((tn, D_MODEL), lambda i, j: (j, 0))
    return pl.pallas_call(
        _merge_kernel,
        grid=(T // tm, D_MODEL // tn),
        in_specs=[row(D_MODEL), row(M_W), row(A_W), row(C_W),
                  gate, gate, gate, col(M_W), col(A_W), col(C_W)],
        out_specs=pl.BlockSpec((tm, tn), lambda i, j: (i, j)),
        out_shape=jax.ShapeDtypeStruct((T, D_MODEL), BF16),
        compiler_params=_params("parallel", "arbitrary"),
        name="merge",
    )(h, hm, ha, hc, wgm, wga, wgc, wbm, wba, wbc)


def _out_proj_kernel(final, x_ref, m_ref, w_ref, g_ref, o_ref):
    y = x_ref[...] + _dot(m_ref[...], w_ref[...])
    o_ref[...] = _rms(y, g_ref[...]) if final else y


def _out_proj(x2, merged, w, gain, final):
    T = x2.shape[0]
    tm = SEQ_TILE
    row = pl.BlockSpec((tm, D_MODEL), lambda i: (i, 0))
    return pl.pallas_call(
        functools.partial(_out_proj_kernel, final),
        grid=(T // tm,),
        in_specs=[row, row, _const_spec(w.shape), _const_spec((1, D_MODEL))],
        out_specs=row,
        out_shape=jax.ShapeDtypeStruct((T, D_MODEL), F32),
        compiler_params=_params("parallel"),
        name="out_proj",
    )(x2, merged, w, gain)


def _rope_lanes(a):
    half = A_ROPE // 2
    z = jnp.zeros(a.shape[:-1] + (half,), a.dtype)
    return jnp.concatenate([a[..., :half], z, a[..., half:], z], axis=-1)


def _layer(x2, pos_col, kv_mem_in, l, final, batch, seq, w_in, b_igate, b_fgate, conv_w, conv_b,
           mh_norm, cq_norm, w_uq, ckv_norm, w_ukv, mem_norm, w_mem_kv, w_br_m, w_br_a, w_br_c,
           w_out, norm, final_norm):
    offs = [0]
    for s in IN_SPLITS:
        offs.append(offs[-1] + s)
    wlt = w_in[l].T
    seg = lambda i: wlt[offs[i]:offs[i + 1], :]
    bf = lambda a: a.astype(BF16)
    row = lambda a: a.reshape(1, -1).astype(F32)

    wqk, wvt, wo, wz = bf(seg(0)), bf(seg(1)), bf(seg(2)), bf(seg(3))
    wif = bf(jnp.pad(wlt[offs[4]:offs[6], :], ((0, LANES - 2 * M_HEADS), (0, 0))))
    wcq, wckv, waz = bf(seg(6)), bf(seg(7)), bf(seg(9))
    wkr = bf(_rope_lanes(seg(8).T).T)
    wcqm, wczm = bf(seg(10)), bf(seg(11))
    gates = seg(12)
    wgm, wga, wgc = (bf(gates[i * D_MODEL:(i + 1) * D_MODEL, :]) for i in range(3))
    uq = w_uq[l].reshape(Q_LORA, A_HEADS, A_DQK)
    wqn = bf(uq[:, :, :A_NOPE].reshape(Q_LORA, A_HEADS * A_NOPE))
    wqr = bf(_rope_lanes(uq[:, :, A_NOPE:]).reshape(Q_LORA, A_HEADS * LANES))
    ukv = w_ukv[l].reshape(KV_LORA, A_HEADS, A_NOPE + A_DV)
    wkn = bf(ukv[:, :, :A_NOPE].reshape(KV_LORA, A_HEADS * A_NOPE))
    wvvt = bf(ukv[:, :, A_NOPE:].reshape(KV_LORA, A_HEADS * A_DV).T)
    bias_row = jnp.pad(jnp.concatenate([b_igate[l], b_fgate[l]]).astype(F32),
                       (0, LANES - 2 * M_HEADS)).reshape(1, LANES)
    inv_freq = ROPE_THETA ** (-jnp.arange(0, A_ROPE, 2, dtype=F32) / A_ROPE)
    invf = _rope_lanes(jnp.concatenate([inv_freq, inv_freq])).reshape(1, LANES)
    ones = jnp.ones((A_ROPE // 2,), F32)
    sgn = _rope_lanes(jnp.concatenate([-ones, ones])).reshape(1, LANES)

    h, mq, mk, mvt, mgate, pre_if = _mlstm_proj(x2, row(norm[l]), wqk, conv_w[l].astype(F32),
                                                row(conv_b[l]), wvt, wo, wz, wif, batch, seq)
    rowf, colf = _gate_scan(pre_if, bias_row, batch, seq)
    hm = _decay_attn(mq, mk, mvt, mgate, rowf, colf, row(mh_norm[l]), batch, seq)

    aq, ak, avt, agate = _mla_proj(h, pos_col, invf, sgn, row(cq_norm[l]), row(ckv_norm[l]),
                                   wcq, wckv, wkr, waz, wqn, wqr, wkn, wvvt, batch, seq)
    ha = _mla_attn(aq, ak, avt, agate, batch, seq)

    kv_mem = _mem_kv(kv_mem_in, row(mem_norm[l]), bf(w_mem_kv[l]))
    hc = _mem_attn(h, kv_mem, wcqm, wczm, seq)

    merged = _merge(h, hm, ha, hc, wgm, wga, wgc, bf(w_br_m[l]), bf(w_br_a[l]), bf(w_br_c[l]))
    return _out_proj(x2, merged, bf(w_out[l]), row(final_norm), final)


def kernel(x, mem, positions, w_in, b_igate, b_fgate, conv_w, conv_b, mh_norm, cq_norm, w_uq,
           ckv_norm, w_ukv, mem_norm, w_mem_kv, w_br_m, w_br_a, w_br_c, w_out, norm, final_norm):
    batch, seq, d = x.shape
    depth = w_in.shape[0]
    assert d == D_MODEL and seq % MG_TM == 0 and w_in.shape[2] == sum(IN_SPLITS)
    x2 = x.reshape(batch * seq, d)
    pos_col = positions.astype(F32).reshape(batch * seq, 1)
    for l in range(depth):
        x2 = _layer(x2, pos_col, mem, l, l == depth - 1, batch, seq, w_in, b_igate, b_fgate,
                    conv_w, conv_b, mh_norm, cq_norm, w_uq, ckv_norm, w_ukv, mem_norm, w_mem_kv,
                    w_br_m, w_br_a, w_br_c, w_out, norm, final_norm)
    return x2.reshape(batch, seq, d)
```

```python
import functools
import math

import jax
import jax.numpy as jnp
from jax import lax
from jax.experimental import pallas as pl
from jax.experimental.pallas import tpu as pltpu

F32 = jnp.float32
BF16 = jnp.bfloat16

D_MODEL = 2048
M_HEADS, M_DH = 4, 256
M_W = M_HEADS * M_DH
CONV_K = 4
A_HEADS, A_NOPE, A_ROPE, A_DV = 8, 128, 64, 128
A_DQK = A_NOPE + A_ROPE
A_W = A_HEADS * A_DV
Q_LORA = KV_LORA = 512
ROPE_THETA = 10000.0
N_MEM = 256
C_HEADS, C_DH = 4, 256
C_W = C_HEADS * C_DH
EPS = 1e-6

LANES = 128
SUBLANES = 8
A_QK_PAD = 2 * LANES
VMEM_LIMIT = 56 * 1024 * 1024

SEQ_TILE = 512
HEADS_PER_STEP = 2
MLA_HEADS_PER_STEP = 4

IN_SPLITS = (2 * M_W, M_W, M_W, M_W, M_HEADS, M_HEADS, Q_LORA, KV_LORA, A_ROPE, A_W, C_W, C_W,
             3 * D_MODEL)


def _const_spec(shape):
    nd = len(shape)
    return pl.BlockSpec(shape, lambda *_: (0,) * nd, pipeline_mode=pl.Buffered(1))


W_QK, W_V, W_O, W_Z, W_IF = 0, 2048, 3072, 4096, 5120
W_CQ, W_CKV, W_KR, W_AZ, W_CQM, W_CZM, W_GATE = 5632, 6144, 6656, 7168, 8192, 9216, 10240
W_ROWS = W_GATE + 3 * D_MODEL


def _w_rows(offset, rows):
    assert offset % rows == 0
    return pl.BlockSpec((rows, D_MODEL), lambda *_: (offset // rows, 0),
                        pipeline_mode=pl.Buffered(1))


def _params(*sem):
    return pltpu.CompilerParams(dimension_semantics=sem, vmem_limit_bytes=VMEM_LIMIT)


def _dot(a, b):
    return jnp.dot(a, b, preferred_element_type=F32)


def _dot_nt(a, b):
    return lax.dot_general(a, b, (((1,), (1,)), ((), ())), preferred_element_type=F32)


def _rms(x, g):
    return x * lax.rsqrt(jnp.mean(x * x, axis=-1, keepdims=True) + EPS) * g


def _sigmoid(x):
    return 1.0 / (1.0 + jnp.exp(-x))


def _silu(x):
    return x * _sigmoid(x)


def _keys_le_queries(nk, nq):
    shape = (nk, nq)
    return lax.broadcasted_iota(jnp.int32, shape, 0) <= lax.broadcasted_iota(jnp.int32, shape, 1)


def _causal_pieces(qi, t):
    half = t // 2
    return ([(kj * t, t, 0, False) for kj in range(qi)]
            + [(qi * t, half, 0, True), (qi * t + half, half, half, True)])


def _left_pad(x, n, fill):
    if n == 0:
        return x
    return jnp.concatenate([jnp.full((x.shape[0], n), fill, x.dtype), x], axis=1)


MP_CH = 512


def _mlstm_proj_kernel(tiles_per_seq, x_ref, ng_ref, wqk_ref, cw_ref, cb_ref, wvt_ref, wo_ref,
                       wz_ref, wif_ref, h_ref, q_ref, k_ref, vt_ref, g_ref, if_ref,
                       xbuf_ref):
    tm = x_ref.shape[0]

    @pl.when(pl.program_id(0) % tiles_per_seq == 0)
    def _():
        xbuf_ref[0:SUBLANES, :] = jnp.zeros((SUBLANES, 2 * M_W), F32)

    h = _rms(x_ref[...], ng_ref[...]).astype(BF16)
    h_ref[...] = h

    for c in range(2 * M_W // MP_CH):
        cs = slice(c * MP_CH, (c + 1) * MP_CH)
        acc = _dot_nt(h, wqk_ref[cs, :])
        xbuf_ref[SUBLANES:SUBLANES + tm, cs] = acc
        y = cb_ref[:, cs] + cw_ref[CONV_K - 1:CONV_K, cs] * acc
        for j in range(CONV_K - 1):
            back = CONV_K - 1 - j
            y = y + cw_ref[j:j + 1, cs] * xbuf_ref[SUBLANES - back:SUBLANES - back + tm, cs]
        xbuf_ref[0:SUBLANES, cs] = acc[tm - SUBLANES:tm, :]
        y = _silu(y)
        if c < M_W // MP_CH:
            q_ref[:, cs] = y.astype(BF16)
        else:
            ks = slice(c * MP_CH - M_W, (c + 1) * MP_CH - M_W)
            k_ref[:, ks] = (y * (M_DH ** -0.5)).astype(BF16)

    for c in range(M_W // MP_CH):
        cs = slice(c * MP_CH, (c + 1) * MP_CH)
        vt_ref[0, 0, cs, :] = _dot_nt(wvt_ref[cs, :], h).astype(BF16)
        o = _dot_nt(h, wo_ref[cs, :])
        z = _dot_nt(h, wz_ref[cs, :])
        g_ref[:, cs] = (_sigmoid(o) * _silu(z)).astype(BF16)

    if_ref[...] = _dot_nt(h, wif_ref[...])


def _mlstm_proj(x2, norm_g, wall, conv_w, conv_b, batch, seq):
    T = x2.shape[0]
    tm = SEQ_TILE
    per_seq = seq // tm
    row = lambda w: pl.BlockSpec((tm, w), lambda i: (i, 0))
    out_shape = (
        jax.ShapeDtypeStruct((T, D_MODEL), BF16),
        jax.ShapeDtypeStruct((T, M_W), BF16),
        jax.ShapeDtypeStruct((T, M_W), BF16),
        jax.ShapeDtypeStruct((batch, per_seq, M_W, tm), BF16),
        jax.ShapeDtypeStruct((T, M_W), BF16),
        jax.ShapeDtypeStruct((T, LANES), F32),
    )
    vt_spec = pl.BlockSpec((1, 1, M_W, tm), lambda i: (i // per_seq, i % per_seq, 0, 0))
    return pl.pallas_call(
        functools.partial(_mlstm_proj_kernel, per_seq),
        grid=(T // tm,),
        in_specs=[row(D_MODEL), _const_spec((1, D_MODEL)), _w_rows(W_QK, 2 * M_W),
                  _const_spec(conv_w.shape), _const_spec(conv_b.shape), _w_rows(W_V, M_W),
                  _w_rows(W_O, M_W), _w_rows(W_Z, M_W), _w_rows(W_IF, LANES)],
        out_specs=(row(D_MODEL), row(M_W), row(M_W), vt_spec, row(M_W), row(LANES)),
        out_shape=out_shape,
        scratch_shapes=[pltpu.VMEM((tm + SUBLANES, 2 * M_W), F32)],
        compiler_params=_params("arbitrary"),
        name="mlstm_proj",
    )(x2, norm_g, wall, conv_w, conv_b, wall, wall, wall, wall)


def _lane_scan(x, op, fill):
    n = x.shape[-1]
    lane = lax.broadcasted_iota(jnp.int32, x.shape, x.ndim - 1)
    d = 1
    while d < n:
        shifted = pltpu.roll(x, d, x.ndim - 1)
        x = op(x, jnp.where(lane >= d, shifted, fill))
        d *= 2
    return x


def _gate_scan_kernel(if_ref, bias_ref, row_ref, col_ref):
    S = if_ref.shape[1]
    t = SEQ_TILE
    pre = if_ref[0] + bias_ref[...]
    t8 = pre.T[0:SUBLANES, :]
    lf = jnp.minimum(t8, 0.0) - jnp.log(1.0 + jnp.exp(-jnp.abs(t8)))
    b = _lane_scan(lf, jnp.add, 0.0)
    b = pltpu.roll(b, M_HEADS, 0)
    a = t8 - b
    mx = jnp.maximum(_lane_scan(a, jnp.maximum, -jnp.inf), 0.0)
    nb = -b - mx
    sub = lax.broadcasted_iota(jnp.int32, (SUBLANES, S), 0)
    zeros = jnp.zeros((LANES - SUBLANES, S), F32)
    n_h = HEADS_PER_STEP
    for p in range(M_HEADS // n_h):
        up = (SUBLANES - n_h * p) % SUBLANES
        m_grp = pltpu.roll(mx, up, 0) if up else mx
        nb_grp = pltpu.roll(nb, (up + n_h) % SUBLANES, 0)
        stack = jnp.where(sub < n_h, m_grp, nb_grp)
        for j in range(S // t):
            row_ref[0, p, j] = stack[:, j * t:(j + 1) * t]
        a_grp = pltpu.roll(a, up, 0) if up else a
        col_ref[0, p] = jnp.concatenate([a_grp, zeros], axis=0).T


def _gate_scan(pre_if, bias_row, batch, seq):
    n_grp = M_HEADS // HEADS_PER_STEP
    n_tile = seq // SEQ_TILE
    return pl.pallas_call(
        _gate_scan_kernel,
        grid=(batch,),
        in_specs=[pl.BlockSpec((1, seq, LANES), lambda b: (b, 0, 0)), _const_spec((1, LANES))],
        out_specs=(pl.BlockSpec((1, n_grp, n_tile, SUBLANES, SEQ_TILE), lambda b: (b, 0, 0, 0, 0)),
                   pl.BlockSpec((1, n_grp, seq, LANES), lambda b: (b, 0, 0, 0))),
        out_shape=(jax.ShapeDtypeStruct((batch, n_grp, n_tile, SUBLANES, SEQ_TILE), F32),
                   jax.ShapeDtypeStruct((batch, n_grp, seq, LANES), F32)),
        compiler_params=_params("parallel"),
        name="gate_scan",
    )(pre_if.reshape(batch, seq, LANES), bias_row)


def _decay_attn_kernel(q_ref, k_ref, vt_ref, g_ref, row_ref, col_ref, gain_ref, o_ref,
                       arep_ref, fac_ref):
    S = q_ref.shape[1]
    t = SEQ_TILE
    n_head = HEADS_PER_STEP
    hs = [slice(hh * M_DH, (hh + 1) * M_DH) for hh in range(n_head)]
    lane_tiles = t // LANES
    widen = lambda a: jnp.concatenate([a] * lane_tiles, axis=1)

    c_rep = [[None] * (S // t) for _ in range(n_head)]
    for hh in range(n_head):
        arep_ref[hh] = jnp.broadcast_to(col_ref[0, 0, :, hh:hh + 1], (S, LANES))
        for kj in range(S // t - 1):
            a = arep_ref[hh, kj * t:(kj + 1) * t, :]
            c_rep[hh][kj] = jnp.max(a, axis=0, keepdims=True)
            fac_ref[hh, kj * t:(kj + 1) * t, :] = jnp.exp(a - c_rep[hh][kj])

    for qi in range(S // t):
        qs = slice(qi * t, (qi + 1) * t)
        for hh in range(n_head):
            q = q_ref[0, qs, hs[hh]]
            m_row = row_ref[0, 0, qi, hh:hh + 1, :]
            nb_row = row_ref[0, 0, qi, n_head + hh:n_head + hh + 1, :]
            den = num = None
            for k0, nk, q0, diag in _causal_pieces(qi, t):
                ks = slice(k0, k0 + nk)
                kc, ko = divmod(k0, t)
                st = _dot_nt(k_ref[0, ks, hs[hh]], q[q0:, :])
                if diag:
                    arg = widen(arep_ref[hh, ks, :])[:, q0:] - m_row[:, q0:]
                    p = st * jnp.exp(jnp.where(_keys_le_queries(nk, t - q0), arg, -jnp.inf))
                    d_blk = jnp.sum(p, axis=0, keepdims=True)
                    n_blk = _dot(vt_ref[0, kc, hs[hh], ko:ko + nk], p.astype(BF16))
                else:
                    p = st * widen(fac_ref[hh, ks, :])
                    qfac = jnp.exp(widen(c_rep[hh][kc]) - m_row)
                    d_blk = qfac * jnp.sum(p, axis=0, keepdims=True)
                    n_blk = qfac * _dot(vt_ref[0, kc, hs[hh], :], p.astype(BF16))
                d_blk, n_blk = _left_pad(d_blk, q0, 0.0), _left_pad(n_blk, q0, 0.0)
                den, num = (d_blk, n_blk) if den is None else (den + d_blk, num + n_blk)
            hv = num / jnp.maximum(jnp.abs(den), jnp.exp(nb_row))
            hv = hv * lax.rsqrt(jnp.mean(hv * hv, axis=0, keepdims=True) + EPS)
            o_ref[0, qs, hs[hh]] = (hv.T * gain_ref[:, hs[hh]]
                                    * g_ref[0, qs, hs[hh]].astype(F32)).astype(BF16)


def _decay_attn(q, k, vt, gate, rowf, colf, gain, batch, seq):
    n_grp = M_HEADS // HEADS_PER_STEP
    n_tile = seq // SEQ_TILE
    w = HEADS_PER_STEP * M_DH
    blk = pl.BlockSpec((1, seq, w), lambda b, p: (b, 0, p))
    q3, k3, g3 = (a.reshape(batch, seq, M_W) for a in (q, k, gate))
    return pl.pallas_call(
        _decay_attn_kernel,
        grid=(batch, n_grp),
        in_specs=[blk, blk,
                  pl.BlockSpec((1, n_tile, w, SEQ_TILE), lambda b, p: (b, 0, p, 0)),
                  blk,
                  pl.BlockSpec((1, 1, n_tile, SUBLANES, SEQ_TILE), lambda b, p: (b, p, 0, 0, 0)),
                  pl.BlockSpec((1, 1, seq, LANES), lambda b, p: (b, p, 0, 0)),
                  pl.BlockSpec((1, w), lambda b, p: (0, p))],
        out_specs=blk,
        out_shape=jax.ShapeDtypeStruct((batch, seq, M_W), BF16),
        scratch_shapes=[pltpu.VMEM((HEADS_PER_STEP, seq, LANES), F32),
                        pltpu.VMEM((HEADS_PER_STEP, seq - SEQ_TILE, LANES), F32)],
        compiler_params=_params("parallel", "parallel"),
        name="decay_attn",
    )(q3, k3, vt, g3, rowf, colf, gain).reshape(batch * seq, M_W)


def _mla_proj_kernel(h_ref, pos_ref, invf_ref, gq_ref, gkv_ref, wcq_ref, wckv_ref,
                     wkr_ref, waz_ref, wqn_ref, wqr_ref, wkn_ref, wvt_ref,
                     q_ref, k_ref, vt_ref, g_ref):
    h = h_ref[...]
    tm = h.shape[0]
    scale = math.log2(math.e) / math.sqrt(A_DQK)
    ang = invf_ref[...] * pos_ref[0]
    c32, s32 = jnp.cos(ang), jnp.sin(ang)
    z32 = jnp.zeros((A_ROPE // 2, tm), F32)
    cos = jnp.concatenate([c32, z32, c32, z32], axis=0).T
    sin = jnp.concatenate([-s32, z32, s32, z32], axis=0).T

    def rope(r):
        return r * cos + pltpu.roll(r, LANES // 2, 1) * sin

    cqn = _rms(_dot_nt(h, wcq_ref[...]), gq_ref[...]).astype(BF16)
    qn = _dot(cqn, wqn_ref[...]) * scale
    qr = _dot(cqn, wqr_ref[...]) * scale
    for hd in range(A_HEADS):
        base = hd * A_QK_PAD
        ls = slice(hd * LANES, (hd + 1) * LANES)
        q_ref[:, base:base + LANES] = qn[:, ls].astype(BF16)
        q_ref[:, base + LANES:base + A_QK_PAD] = rope(qr[:, ls]).astype(BF16)

    kr = rope(_dot_nt(h, wkr_ref[...])).astype(BF16)
    ckvn = _rms(_dot_nt(h, wckv_ref[...]), gkv_ref[...]).astype(BF16)
    kn = _dot(ckvn, wkn_ref[...])
    for hd in range(A_HEADS):
        base = hd * A_QK_PAD
        k_ref[:, base:base + LANES] = kn[:, hd * LANES:(hd + 1) * LANES].astype(BF16)
        k_ref[:, base + LANES:base + A_QK_PAD] = kr
    vt_ref[0, 0] = _dot_nt(wvt_ref[...], ckvn).astype(BF16)
    g_ref[...] = _silu(_dot_nt(h, waz_ref[...])).astype(BF16)


def _mla_proj(h, pos_row, invf, gq, gkv, wall, wqn, wqr, wkn, wvt, batch, seq):
    T = h.shape[0]
    tm = SEQ_TILE
    per_seq = seq // tm
    row = lambda w: pl.BlockSpec((tm, w), lambda i: (i, 0))
    small = (invf, gq, gkv)
    ups = (wqn, wqr, wkn, wvt)
    vt_spec = pl.BlockSpec((1, 1, A_W, tm), lambda i: (i // per_seq, i % per_seq, 0, 0))
    return pl.pallas_call(
        _mla_proj_kernel,
        grid=(T // tm,),
        in_specs=([row(D_MODEL), pl.BlockSpec((1, 1, tm), lambda i: (i, 0, 0))]
                  + [_const_spec(c.shape) for c in small]
                  + [_w_rows(W_CQ, Q_LORA), _w_rows(W_CKV, KV_LORA), _w_rows(W_KR, LANES),
                     _w_rows(W_AZ, A_W)]
                  + [_const_spec(c.shape) for c in ups]),
        out_specs=(row(A_HEADS * A_QK_PAD), row(A_HEADS * A_QK_PAD), vt_spec, row(A_W)),
        out_shape=(jax.ShapeDtypeStruct((T, A_HEADS * A_QK_PAD), BF16),
                   jax.ShapeDtypeStruct((T, A_HEADS * A_QK_PAD), BF16),
                   jax.ShapeDtypeStruct((batch, per_seq, A_W, tm), BF16),
                   jax.ShapeDtypeStruct((T, A_W), BF16)),
        compiler_params=_params("parallel"),
        name="mla_proj",
    )(h, pos_row, *small, wall, wall, wall, wall, *ups)


def _mla_attn_kernel(q_ref, k_ref, vt_ref, g_ref, o_ref):
    S = q_ref.shape[1]
    t = SEQ_TILE
    n_head = MLA_HEADS_PER_STEP
    hq = [slice(hh * A_QK_PAD, (hh + 1) * A_QK_PAD) for hh in range(n_head)]
    hv = [slice(hh * A_DV, (hh + 1) * A_DV) for hh in range(n_head)]

    for qi in range(S // t):
        qs = slice(qi * t, (qi + 1) * t)
        for hh in range(n_head):
            q = q_ref[0, qs, hq[hh]]
            pieces = _causal_pieces(qi, t)
            st = []
            for k0, nk, q0, diag in pieces:
                s = _dot_nt(k_ref[0, k0:k0 + nk, hq[hh]], q[q0:, :])
                st.append(jnp.where(_keys_le_queries(nk, t - q0), s, -jnp.inf) if diag else s)
            m = functools.reduce(jnp.maximum, [
                _left_pad(jnp.max(s, axis=0, keepdims=True), q0, -jnp.inf)
                for s, (_, _, q0, _) in zip(st, pieces)])
            l = acc = None
            for s, (k0, nk, q0, _) in zip(st, pieces):
                p = jnp.exp2(s - m[:, q0:])
                l_blk = _left_pad(jnp.sum(p, axis=0, keepdims=True), q0, 0.0)
                kc, ko = divmod(k0, t)
                pv = _left_pad(_dot(vt_ref[0, kc, hv[hh], ko:ko + nk], p.astype(BF16)), q0, 0.0)
                l, acc = (l_blk, pv) if l is None else (l + l_blk, acc + pv)
            o_ref[0, qs, hv[hh]] = ((acc / l).T * g_ref[0, qs, hv[hh]].astype(F32)).astype(BF16)


def _mla_attn(q, k, vt, gate, batch, seq):
    n_tile = seq // SEQ_TILE
    n_h = MLA_HEADS_PER_STEP
    qk_blk = pl.BlockSpec((1, seq, n_h * A_QK_PAD), lambda b, p: (b, 0, p))
    v_blk = pl.BlockSpec((1, seq, n_h * A_DV), lambda b, p: (b, 0, p))
    vt_blk = pl.BlockSpec((1, n_tile, n_h * A_DV, SEQ_TILE), lambda b, p: (b, 0, p, 0))
    q3 = q.reshape(batch, seq, A_HEADS * A_QK_PAD)
    k3 = k.reshape(batch, seq, A_HEADS * A_QK_PAD)
    g3 = gate.reshape(batch, seq, A_W)
    return pl.pallas_call(
        _mla_attn_kernel,
        grid=(batch, A_HEADS // n_h),
        in_specs=[qk_blk, qk_blk, vt_blk, v_blk],
        out_specs=v_blk,
        out_shape=jax.ShapeDtypeStruct((batch, seq, A_W), BF16),
        compiler_params=_params("parallel", "parallel"),
        name="mla_attn",
    )(q3, k3, vt, g3).reshape(batch * seq, A_W)


def _mem_kv_kernel(mem_ref, g_ref, w_ref, kv_ref):
    kv_ref[0] = _dot(_rms(mem_ref[0], g_ref[...]).astype(BF16), w_ref[...]).astype(BF16)


def _mem_kv(mem, gain, w):
    batch = mem.shape[0]
    return pl.pallas_call(
        _mem_kv_kernel,
        grid=(batch,),
        in_specs=[pl.BlockSpec((1, N_MEM, D_MODEL), lambda b: (b, 0, 0)),
                  _const_spec((1, D_MODEL)), _const_spec(w.shape)],
        out_specs=pl.BlockSpec((1, N_MEM, 2 * C_W), lambda b: (b, 0, 0)),
        out_shape=jax.ShapeDtypeStruct((batch, N_MEM, 2 * C_W), BF16),
        compiler_params=_params("parallel"),
        name="mem_kv",
    )(mem, gain, w)


def _mem_attn_kernel(h_ref, kv_ref, wq_ref, wz_ref, o_ref):
    h = h_ref[...]
    cq = (_dot_nt(h, wq_ref[...]) * (C_DH ** -0.5)).astype(BF16)
    cz = _dot_nt(h, wz_ref[...])
    for hd in range(C_HEADS):
        hs = slice(hd * C_DH, (hd + 1) * C_DH)
        s = _dot_nt(cq[:, hs], kv_ref[0, :, hs])
        p = jnp.exp(s - jnp.max(s, axis=-1, keepdims=True))
        l = jnp.sum(p, axis=-1, keepdims=True)
        o = _dot(p.astype(BF16), kv_ref[0, :, C_W + hd * C_DH:C_W + (hd + 1) * C_DH]) / l
        o_ref[:, hs] = (o * _silu(cz[:, hs])).astype(BF16)


def _mem_attn(h, kv, wall, seq):
    T = h.shape[0]
    tm = SEQ_TILE
    per_seq = seq // tm
    return pl.pallas_call(
        _mem_attn_kernel,
        grid=(T // tm,),
        in_specs=[pl.BlockSpec((tm, D_MODEL), lambda i: (i, 0)),
                  pl.BlockSpec((1, N_MEM, 2 * C_W), lambda i: (i // per_seq, 0, 0)),
                  _w_rows(W_CQM, C_W), _w_rows(W_CZM, C_W)],
        out_specs=pl.BlockSpec((tm, C_W), lambda i: (i, 0)),
        out_shape=jax.ShapeDtypeStruct((T, C_W), BF16),
        compiler_params=_params("parallel"),
        name="mem_attn",
    )(h, kv, wall, wall)


MG_TM = 1024
MG_TN = 256


def _merge_kernel(h_ref, hm_ref, ha_ref, hc_ref, wgm_ref, wga_ref, wgc_ref,
                  wbm_ref, wba_ref, wbc_ref, o_ref):
    h = h_ref[...]
    acc = _sigmoid(_dot_nt(h, wgm_ref[...])) * _dot(hm_ref[...], wbm_ref[...])
    acc = acc + _sigmoid(_dot_nt(h, wga_ref[...])) * _dot(ha_ref[...], wba_ref[...])
    acc = acc + _sigmoid(_dot_nt(h, wgc_ref[...])) * _dot(hc_ref[...], wbc_ref[...])
    o_ref[...] = acc.astype(BF16)


def _merge(h, hm, ha, hc, wall, wbm, wba, wbc):
    T = h.shape[0]
    tm, tn = MG_TM, MG_TN
    row = lambda w: pl.BlockSpec((tm, w), lambda i, j: (i, 0))
    col = lambda kdim: pl.BlockSpec((kdim, tn), lambda i, j: (0, j))

    def gate(branch):
        first = (W_GATE + branch * D_MODEL) // tn
        return pl.BlockSpec((tn, D_MODEL), lambda i, j: (first + j, 0))

    return pl.pallas_call(
        _merge_kernel,
        grid=(T // tm, D_MODEL // tn),
        in_specs=[row(D_MODEL), row(M_W), row(A_W), row(C_W),
                  gate(0), gate(1), gate(2), col(M_W), col(A_W), col(C_W)],
        out_specs=pl.BlockSpec((tm, tn), lambda i, j: (i, j)),
        out_shape=jax.ShapeDtypeStruct((T, D_MODEL), BF16),
        compiler_params=_params("parallel", "arbitrary"),
        name="merge",
    )(h, hm, ha, hc, wall, wall, wall, wbm, wba, wbc)


def _out_proj_kernel(final, x_ref, m_ref, w_ref, g_ref, o_ref):
    y = x_ref[...] + _dot(m_ref[...], w_ref[...])
    o_ref[...] = _rms(y, g_ref[...]) if final else y


def _out_proj(x2, merged, w, gain, final):
    T = x2.shape[0]
    tm = SEQ_TILE
    row = pl.BlockSpec((tm, D_MODEL), lambda i: (i, 0))
    return pl.pallas_call(
        functools.partial(_out_proj_kernel, final),
        grid=(T // tm,),
        in_specs=[row, row, _const_spec(w.shape), _const_spec((1, D_MODEL))],
        out_specs=row,
        out_shape=jax.ShapeDtypeStruct((T, D_MODEL), F32),
        compiler_params=_params("parallel"),
        name="out_proj",
    )(x2, merged, w, gain)


def _rope_lanes(a):
    half = A_ROPE // 2
    z = jnp.zeros(a.shape[:-1] + (half,), a.dtype)
    return jnp.concatenate([a[..., :half], z, a[..., half:], z], axis=-1)


def _pack_w_in(w):
    offs = [0]
    for s in IN_SPLITS:
        offs.append(offs[-1] + s)
    wt = w.T
    zeros = lambda n: jnp.zeros((n, D_MODEL), w.dtype)
    half = A_ROPE // 2
    kr = wt[offs[8]:offs[9]]
    pieces = [wt[:offs[6]], zeros(W_CQ - offs[6]),
              wt[offs[6]:offs[8]],
              kr[:half], zeros(half), kr[half:], zeros(half), zeros(W_AZ - W_KR - LANES),
              wt[offs[9]:]]
    packed = jnp.concatenate(pieces, axis=0).astype(BF16)
    assert packed.shape == (W_ROWS, D_MODEL)
    return packed


def _layer(x2, pos_row, kv_mem_in, l, final, batch, seq, w_in, b_igate, b_fgate, conv_w, conv_b,
           mh_norm, cq_norm, w_uq, ckv_norm, w_ukv, mem_norm, w_mem_kv, w_br_m, w_br_a, w_br_c,
           w_out, norm, final_norm):
    bf = lambda a: a.astype(BF16)
    row = lambda a: a.reshape(1, -1).astype(F32)

    wall = _pack_w_in(w_in[l])
    uq = w_uq[l].reshape(Q_LORA, A_HEADS, A_DQK)
    wqn = bf(uq[:, :, :A_NOPE].reshape(Q_LORA, A_HEADS * A_NOPE))
    wqr = bf(_rope_lanes(uq[:, :, A_NOPE:]).reshape(Q_LORA, A_HEADS * LANES))
    ukv = w_ukv[l].reshape(KV_LORA, A_HEADS, A_NOPE + A_DV)
    wkn = bf(ukv[:, :, :A_NOPE].reshape(KV_LORA, A_HEADS * A_NOPE))
    wvvt = bf(ukv[:, :, A_NOPE:].reshape(KV_LORA, A_HEADS * A_DV).T)
    bias_row = jnp.pad(jnp.concatenate([b_igate[l], b_fgate[l]]).astype(F32),
                       (0, LANES - 2 * M_HEADS)).reshape(1, LANES)
    invf = (ROPE_THETA ** (-jnp.arange(0, A_ROPE, 2, dtype=F32) / A_ROPE)).reshape(-1, 1)

    h, mq, mk, mvt, mgate, pre_if = _mlstm_proj(x2, row(norm[l]), wall, conv_w[l].astype(F32),
                                                row(conv_b[l]), batch, seq)
    rowf, colf = _gate_scan(pre_if, bias_row, batch, seq)
    hm = _decay_attn(mq, mk, mvt, mgate, rowf, colf, row(mh_norm[l]), batch, seq)

    aq, ak, avt, agate = _mla_proj(h, pos_row, invf, row(cq_norm[l]), row(ckv_norm[l]),
                                   wall, wqn, wqr, wkn, wvvt, batch, seq)
    ha = _mla_attn(aq, ak, avt, agate, batch, seq)

    kv_mem = _mem_kv(kv_mem_in, row(mem_norm[l]), bf(w_mem_kv[l]))
    hc = _mem_attn(h, kv_mem, wall, seq)

    merged = _merge(h, hm, ha, hc, wall, bf(w_br_m[l]), bf(w_br_a[l]), bf(w_br_c[l]))
    return _out_proj(x2, merged, bf(w_out[l]), row(final_norm), final)


def kernel(x, mem, positions, w_in, b_igate, b_fgate, conv_w, conv_b, mh_norm, cq_norm, w_uq,
           ckv_norm, w_ukv, mem_norm, w_mem_kv, w_br_m, w_br_a, w_br_c, w_out, norm, final_norm):
    batch, seq, d = x.shape
    depth = w_in.shape[0]
    assert d == D_MODEL and seq % MG_TM == 0 and w_in.shape[2] == sum(IN_SPLITS)
    x2 = x.reshape(batch * seq, d)
    pos_row = positions.astype(F32).reshape(batch * seq // SEQ_TILE, 1, SEQ_TILE)
    for l in range(depth):
        x2 = _layer(x2, pos_row, mem, l, l == depth - 1, batch, seq, w_in, b_igate, b_fgate,
                    conv_w, conv_b, mh_norm, cq_norm, w_uq, ckv_norm, w_ukv, mem_norm, w_mem_kv,
                    w_br_m, w_br_a, w_br_c, w_out, norm, final_norm)
    return x2.reshape(batch, seq, d)
```

```python
import functools
import math

import jax
import jax.numpy as jnp
from jax import lax
from jax.experimental import pallas as pl
from jax.experimental.pallas import tpu as pltpu

F32 = jnp.float32
BF16 = jnp.bfloat16

D_MODEL = 2048
M_HEADS, M_DH = 4, 256
M_W = M_HEADS * M_DH
CONV_K = 4
A_HEADS, A_NOPE, A_ROPE, A_DV = 8, 128, 64, 128
A_DQK = A_NOPE + A_ROPE
A_W = A_HEADS * A_DV
Q_LORA = KV_LORA = 512
ROPE_THETA = 10000.0
N_MEM = 256
C_HEADS, C_DH = 4, 256
C_W = C_HEADS * C_DH
EPS = 1e-6

LANES = 128
SUBLANES = 8
A_QK_PAD = 2 * LANES
VMEM_LIMIT = 56 * 1024 * 1024

SEQ_TILE = 512
HEADS_PER_STEP = 2
MLA_HEADS_PER_STEP = 4

IN_SPLITS = (2 * M_W, M_W, M_W, M_W, M_HEADS, M_HEADS, Q_LORA, KV_LORA, A_ROPE, A_W, C_W, C_W,
             3 * D_MODEL)


def _const_spec(shape):
    nd = len(shape)
    return pl.BlockSpec(shape, lambda *_: (0,) * nd, pipeline_mode=pl.Buffered(1))


IN_OFFS = [sum(IN_SPLITS[:i]) for i in range(len(IN_SPLITS) + 1)]

W_QK, W_V, W_O, W_Z, W_IF = 0, 2048, 3072, 4096, 5120
W_CQ, W_CKV, W_KR, W_AZ = 5632, 6144, 6656, 7168
W_ROWS = W_AZ + A_W


def _w_rows(offset, rows):
    assert offset % rows == 0
    return pl.BlockSpec((rows, D_MODEL), lambda *_: (offset // rows, 0),
                        pipeline_mode=pl.Buffered(1))


def _w_in_rows(first, rows):
    assert first % SUBLANES == 0
    return pl.BlockSpec((pl.Element(rows), pl.Element(D_MODEL)), lambda *_: (first, 0),
                        pipeline_mode=pl.Buffered(1))


def _params(*sem):
    return pltpu.CompilerParams(dimension_semantics=sem, vmem_limit_bytes=VMEM_LIMIT)


def _dot(a, b):
    return jnp.dot(a, b, preferred_element_type=F32)


def _dot_nt(a, b):
    return lax.dot_general(a, b, (((1,), (1,)), ((), ())), preferred_element_type=F32)


def _rms(x, g):
    return x * lax.rsqrt(jnp.mean(x * x, axis=-1, keepdims=True) + EPS) * g


def _sigmoid(x):
    return 1.0 / (1.0 + jnp.exp(-x))


def _silu(x):
    return x * _sigmoid(x)


def _keys_le_queries(nk, nq):
    shape = (nk, nq)
    return lax.broadcasted_iota(jnp.int32, shape, 0) <= lax.broadcasted_iota(jnp.int32, shape, 1)


def _causal_pieces(qi, t):
    half = t // 2
    return ([(kj * t, t, 0, False) for kj in range(qi)]
            + [(qi * t, half, 0, True), (qi * t + half, half, half, True)])


def _left_pad(x, n, fill):
    if n == 0:
        return x
    return jnp.concatenate([jnp.full((x.shape[0], n), fill, x.dtype), x], axis=1)


MP_CH = 512


def _mlstm_proj_kernel(tiles_per_seq, x_ref, ng_ref, wqk_ref, cw_ref, cb_ref, wvt_ref, wo_ref,
                       wz_ref, wif_ref, h_ref, q_ref, k_ref, vt_ref, g_ref, if_ref,
                       xbuf_ref):
    tm = x_ref.shape[0]

    @pl.when(pl.program_id(0) % tiles_per_seq == 0)
    def _():
        xbuf_ref[0:SUBLANES, :] = jnp.zeros((SUBLANES, 2 * M_W), F32)

    h = _rms(x_ref[...], ng_ref[...]).astype(BF16)
    h_ref[...] = h

    for c in range(2 * M_W // MP_CH):
        cs = slice(c * MP_CH, (c + 1) * MP_CH)
        acc = _dot_nt(h, wqk_ref[cs, :])
        xbuf_ref[SUBLANES:SUBLANES + tm, cs] = acc
        y = cb_ref[:, cs] + cw_ref[CONV_K - 1:CONV_K, cs] * acc
        for j in range(CONV_K - 1):
            back = CONV_K - 1 - j
            y = y + cw_ref[j:j + 1, cs] * xbuf_ref[SUBLANES - back:SUBLANES - back + tm, cs]
        xbuf_ref[0:SUBLANES, cs] = acc[tm - SUBLANES:tm, :]
        y = _silu(y)
        if c < M_W // MP_CH:
            q_ref[:, cs] = y.astype(BF16)
        else:
            ks = slice(c * MP_CH - M_W, (c + 1) * MP_CH - M_W)
            k_ref[:, ks] = (y * (M_DH ** -0.5)).astype(BF16)

    for c in range(M_W // MP_CH):
        cs = slice(c * MP_CH, (c + 1) * MP_CH)
        vt_ref[0, 0, cs, :] = _dot_nt(wvt_ref[cs, :], h).astype(BF16)
        o = _dot_nt(h, wo_ref[cs, :])
        z = _dot_nt(h, wz_ref[cs, :])
        g_ref[:, cs] = (_sigmoid(o) * _silu(z)).astype(BF16)

    if_ref[...] = _dot_nt(h, wif_ref[...])


def _mlstm_proj(x2, norm_g, wall, conv_w, conv_b, batch, seq):
    T = x2.shape[0]
    tm = SEQ_TILE
    per_seq = seq // tm
    row = lambda w: pl.BlockSpec((tm, w), lambda i: (i, 0))
    out_shape = (
        jax.ShapeDtypeStruct((T, D_MODEL), BF16),
        jax.ShapeDtypeStruct((T, M_W), BF16),
        jax.ShapeDtypeStruct((T, M_W), BF16),
        jax.ShapeDtypeStruct((batch, per_seq, M_W, tm), BF16),
        jax.ShapeDtypeStruct((T, M_W), BF16),
        jax.ShapeDtypeStruct((T, LANES), F32),
    )
    vt_spec = pl.BlockSpec((1, 1, M_W, tm), lambda i: (i // per_seq, i % per_seq, 0, 0))
    return pl.pallas_call(
        functools.partial(_mlstm_proj_kernel, per_seq),
        grid=(T // tm,),
        in_specs=[row(D_MODEL), _const_spec((1, D_MODEL)), _w_rows(W_QK, 2 * M_W),
                  _const_spec(conv_w.shape), _const_spec(conv_b.shape), _w_rows(W_V, M_W),
                  _w_rows(W_O, M_W), _w_rows(W_Z, M_W), _w_rows(W_IF, LANES)],
        out_specs=(row(D_MODEL), row(M_W), row(M_W), vt_spec, row(M_W), row(LANES)),
        out_shape=out_shape,
        scratch_shapes=[pltpu.VMEM((tm + SUBLANES, 2 * M_W), F32)],
        compiler_params=_params("arbitrary"),
        name="mlstm_proj",
    )(x2, norm_g, wall, conv_w, conv_b, wall, wall, wall, wall)


def _lane_scan(x, op, fill):
    n = x.shape[-1]
    lane = lax.broadcasted_iota(jnp.int32, x.shape, x.ndim - 1)
    d = 1
    while d < n:
        shifted = pltpu.roll(x, d, x.ndim - 1)
        x = op(x, jnp.where(lane >= d, shifted, fill))
        d *= 2
    return x


def _gate_scan_kernel(if_ref, bias_ref, row_ref, col_ref):
    S = if_ref.shape[1]
    t = SEQ_TILE
    pre = if_ref[0] + bias_ref[...]
    t8 = pre.T[0:SUBLANES, :]
    lf = jnp.minimum(t8, 0.0) - jnp.log(1.0 + jnp.exp(-jnp.abs(t8)))
    b = _lane_scan(lf, jnp.add, 0.0)
    b = pltpu.roll(b, M_HEADS, 0)
    a = t8 - b
    mx = jnp.maximum(_lane_scan(a, jnp.maximum, -jnp.inf), 0.0)
    nb = -b - mx
    sub = lax.broadcasted_iota(jnp.int32, (SUBLANES, S), 0)
    zeros = jnp.zeros((LANES - SUBLANES, S), F32)
    n_h = HEADS_PER_STEP
    for p in range(M_HEADS // n_h):
        up = (SUBLANES - n_h * p) % SUBLANES
        m_grp = pltpu.roll(mx, up, 0) if up else mx
        nb_grp = pltpu.roll(nb, (up + n_h) % SUBLANES, 0)
        stack = jnp.where(sub < n_h, m_grp, nb_grp)
        for j in range(S // t):
            row_ref[0, p, j] = stack[:, j * t:(j + 1) * t]
        a_grp = pltpu.roll(a, up, 0) if up else a
        col_ref[0, p] = jnp.concatenate([a_grp, zeros], axis=0).T


def _gate_scan(pre_if, bias_row, batch, seq):
    n_grp = M_HEADS // HEADS_PER_STEP
    n_tile = seq // SEQ_TILE
    return pl.pallas_call(
        _gate_scan_kernel,
        grid=(batch,),
        in_specs=[pl.BlockSpec((1, seq, LANES), lambda b: (b, 0, 0)), _const_spec((1, LANES))],
        out_specs=(pl.BlockSpec((1, n_grp, n_tile, SUBLANES, SEQ_TILE), lambda b: (b, 0, 0, 0, 0)),
                   pl.BlockSpec((1, n_grp, seq, LANES), lambda b: (b, 0, 0, 0))),
        out_shape=(jax.ShapeDtypeStruct((batch, n_grp, n_tile, SUBLANES, SEQ_TILE), F32),
                   jax.ShapeDtypeStruct((batch, n_grp, seq, LANES), F32)),
        compiler_params=_params("parallel"),
        name="gate_scan",
    )(pre_if.reshape(batch, seq, LANES), bias_row)


def _decay_attn_kernel(q_ref, k_ref, vt_ref, g_ref, row_ref, col_ref, gain_ref, o_ref,
                       arep_ref, fac_ref):
    S = q_ref.shape[1]
    t = SEQ_TILE
    n_head = HEADS_PER_STEP
    hs = [slice(hh * M_DH, (hh + 1) * M_DH) for hh in range(n_head)]
    lane_tiles = t // LANES
    widen = lambda a: jnp.concatenate([a] * lane_tiles, axis=1)

    c_rep = [[None] * (S // t) for _ in range(n_head)]
    for hh in range(n_head):
        arep_ref[hh] = jnp.broadcast_to(col_ref[0, 0, :, hh:hh + 1], (S, LANES))
        for kj in range(S // t - 1):
            a = arep_ref[hh, kj * t:(kj + 1) * t, :]
            c_rep[hh][kj] = jnp.max(a, axis=0, keepdims=True)
            fac_ref[hh, kj * t:(kj + 1) * t, :] = jnp.exp(a - c_rep[hh][kj])

    for qi in range(S // t):
        qs = slice(qi * t, (qi + 1) * t)
        for hh in range(n_head):
            q = q_ref[0, qs, hs[hh]]
            m_row = row_ref[0, 0, qi, hh:hh + 1, :]
            nb_row = row_ref[0, 0, qi, n_head + hh:n_head + hh + 1, :]
            den = num = None
            for k0, nk, q0, diag in _causal_pieces(qi, t):
                ks = slice(k0, k0 + nk)
                kc, ko = divmod(k0, t)
                st = _dot_nt(k_ref[0, ks, hs[hh]], q[q0:, :])
                if diag:
                    arg = widen(arep_ref[hh, ks, :])[:, q0:] - m_row[:, q0:]
                    p = st * jnp.exp(jnp.where(_keys_le_queries(nk, t - q0), arg, -jnp.inf))
                    d_blk = jnp.sum(p, axis=0, keepdims=True)
                    n_blk = _dot(vt_ref[0, kc, hs[hh], ko:ko + nk], p.astype(BF16))
                else:
                    p = st * widen(fac_ref[hh, ks, :])
                    qfac = jnp.exp(widen(c_rep[hh][kc]) - m_row)
                    d_blk = qfac * jnp.sum(p, axis=0, keepdims=True)
                    n_blk = qfac * _dot(vt_ref[0, kc, hs[hh], :], p.astype(BF16))
                d_blk, n_blk = _left_pad(d_blk, q0, 0.0), _left_pad(n_blk, q0, 0.0)
                den, num = (d_blk, n_blk) if den is None else (den + d_blk, num + n_blk)
            hv = num / jnp.maximum(jnp.abs(den), jnp.exp(nb_row))
            hv = hv * lax.rsqrt(jnp.mean(hv * hv, axis=0, keepdims=True) + EPS)
            o_ref[0, qs, hs[hh]] = (hv.T * gain_ref[:, hs[hh]]
                                    * g_ref[0, qs, hs[hh]].astype(F32)).astype(BF16)


def _decay_attn(q, k, vt, gate, rowf, colf, gain, batch, seq):
    n_grp = M_HEADS // HEADS_PER_STEP
    n_tile = seq // SEQ_TILE
    w = HEADS_PER_STEP * M_DH
    blk = pl.BlockSpec((1, seq, w), lambda b, p: (b, 0, p))
    q3, k3, g3 = (a.reshape(batch, seq, M_W) for a in (q, k, gate))
    return pl.pallas_call(
        _decay_attn_kernel,
        grid=(batch, n_grp),
        in_specs=[blk, blk,
                  pl.BlockSpec((1, n_tile, w, SEQ_TILE), lambda b, p: (b, 0, p, 0)),
                  blk,
                  pl.BlockSpec((1, 1, n_tile, SUBLANES, SEQ_TILE), lambda b, p: (b, p, 0, 0, 0)),
                  pl.BlockSpec((1, 1, seq, LANES), lambda b, p: (b, p, 0, 0)),
                  pl.BlockSpec((1, w), lambda b, p: (0, p))],
        out_specs=blk,
        out_shape=jax.ShapeDtypeStruct((batch, seq, M_W), BF16),
        scratch_shapes=[pltpu.VMEM((HEADS_PER_STEP, seq, LANES), F32),
                        pltpu.VMEM((HEADS_PER_STEP, seq - SEQ_TILE, LANES), F32)],
        compiler_params=_params("parallel", "parallel"),
        name="decay_attn",
    )(q3, k3, vt, g3, rowf, colf, gain).reshape(batch * seq, M_W)


def _mla_proj_kernel(h_ref, pos_ref, invf_ref, gq_ref, gkv_ref, wcq_ref, wckv_ref,
                     wkr_ref, waz_ref, wqn_ref, wqr_ref, wkn_ref, wvt_ref,
                     q_ref, k_ref, vt_ref, g_ref):
    h = h_ref[...]
    tm = h.shape[0]
    scale = math.log2(math.e) / math.sqrt(A_DQK)
    ang = invf_ref[...] * pos_ref[0]
    c32, s32 = jnp.cos(ang), jnp.sin(ang)
    z32 = jnp.zeros((A_ROPE // 2, tm), F32)
    cos = jnp.concatenate([c32, z32, c32, z32], axis=0).T
    sin = jnp.concatenate([-s32, z32, s32, z32], axis=0).T

    def rope(r):
        return r * cos + pltpu.roll(r, LANES // 2, 1) * sin

    cqn = _rms(_dot_nt(h, wcq_ref[...]), gq_ref[...]).astype(BF16)
    qn = _dot(cqn, wqn_ref[...]) * scale
    qr = _dot(cqn, wqr_ref[...]) * scale
    for hd in range(A_HEADS):
        base = hd * A_QK_PAD
        ls = slice(hd * LANES, (hd + 1) * LANES)
        q_ref[:, base:base + LANES] = qn[:, ls].astype(BF16)
        q_ref[:, base + LANES:base + A_QK_PAD] = rope(qr[:, ls]).astype(BF16)

    kr = rope(_dot_nt(h, wkr_ref[...])).astype(BF16)
    ckvn = _rms(_dot_nt(h, wckv_ref[...]), gkv_ref[...]).astype(BF16)
    kn = _dot(ckvn, wkn_ref[...])
    for hd in range(A_HEADS):
        base = hd * A_QK_PAD
        k_ref[:, base:base + LANES] = kn[:, hd * LANES:(hd + 1) * LANES].astype(BF16)
        k_ref[:, base + LANES:base + A_QK_PAD] = kr
    vt_ref[0, 0] = _dot_nt(wvt_ref[...], ckvn).astype(BF16)
    g_ref[...] = _silu(_dot_nt(h, waz_ref[...])).astype(BF16)


def _mla_proj(h, pos_row, invf, gq, gkv, wall, wqn, wqr, wkn, wvt, batch, seq):
    T = h.shape[0]
    tm = SEQ_TILE
    per_seq = seq // tm
    row = lambda w: pl.BlockSpec((tm, w), lambda i: (i, 0))
    small = (invf, gq, gkv)
    ups = (wqn, wqr, wkn, wvt)
    vt_spec = pl.BlockSpec((1, 1, A_W, tm), lambda i: (i // per_seq, i % per_seq, 0, 0))
    return pl.pallas_call(
        _mla_proj_kernel,
        grid=(T // tm,),
        in_specs=([row(D_MODEL), pl.BlockSpec((1, 1, tm), lambda i: (i, 0, 0))]
                  + [_const_spec(c.shape) for c in small]
                  + [_w_rows(W_CQ, Q_LORA), _w_rows(W_CKV, KV_LORA), _w_rows(W_KR, LANES),
                     _w_rows(W_AZ, A_W)]
                  + [_const_spec(c.shape) for c in ups]),
        out_specs=(row(A_HEADS * A_QK_PAD), row(A_HEADS * A_QK_PAD), vt_spec, row(A_W)),
        out_shape=(jax.ShapeDtypeStruct((T, A_HEADS * A_QK_PAD), BF16),
                   jax.ShapeDtypeStruct((T, A_HEADS * A_QK_PAD), BF16),
                   jax.ShapeDtypeStruct((batch, per_seq, A_W, tm), BF16),
                   jax.ShapeDtypeStruct((T, A_W), BF16)),
        compiler_params=_params("parallel"),
        name="mla_proj",
    )(h, pos_row, *small, wall, wall, wall, wall, *ups)


def _mla_attn_kernel(q_ref, k_ref, vt_ref, g_ref, o_ref):
    S = q_ref.shape[1]
    t = SEQ_TILE
    n_head = MLA_HEADS_PER_STEP
    hq = [slice(hh * A_QK_PAD, (hh + 1) * A_QK_PAD) for hh in range(n_head)]
    hv = [slice(hh * A_DV, (hh + 1) * A_DV) for hh in range(n_head)]

    for qi in range(S // t):
        qs = slice(qi * t, (qi + 1) * t)
        for hh in range(n_head):
            q = q_ref[0, qs, hq[hh]]
            pieces = _causal_pieces(qi, t)
            st = []
            for k0, nk, q0, diag in pieces:
                s = _dot_nt(k_ref[0, k0:k0 + nk, hq[hh]], q[q0:, :])
                st.append(jnp.where(_keys_le_queries(nk, t - q0), s, -jnp.inf) if diag else s)
            m = functools.reduce(jnp.maximum, [
                _left_pad(jnp.max(s, axis=0, keepdims=True), q0, -jnp.inf)
                for s, (_, _, q0, _) in zip(st, pieces)])
            l = acc = None
            for s, (k0, nk, q0, _) in zip(st, pieces):
                p = jnp.exp2(s - m[:, q0:])
                l_blk = _left_pad(jnp.sum(p, axis=0, keepdims=True), q0, 0.0)
                kc, ko = divmod(k0, t)
                pv = _left_pad(_dot(vt_ref[0, kc, hv[hh], ko:ko + nk], p.astype(BF16)), q0, 0.0)
                l, acc = (l_blk, pv) if l is None else (l + l_blk, acc + pv)
            o_ref[0, qs, hv[hh]] = ((acc / l).T * g_ref[0, qs, hv[hh]].astype(F32)).astype(BF16)


def _mla_attn(q, k, vt, gate, batch, seq):
    n_tile = seq // SEQ_TILE
    n_h = MLA_HEADS_PER_STEP
    qk_blk = pl.BlockSpec((1, seq, n_h * A_QK_PAD), lambda b, p: (b, 0, p))
    v_blk = pl.BlockSpec((1, seq, n_h * A_DV), lambda b, p: (b, 0, p))
    vt_blk = pl.BlockSpec((1, n_tile, n_h * A_DV, SEQ_TILE), lambda b, p: (b, 0, p, 0))
    q3 = q.reshape(batch, seq, A_HEADS * A_QK_PAD)
    k3 = k.reshape(batch, seq, A_HEADS * A_QK_PAD)
    g3 = gate.reshape(batch, seq, A_W)
    return pl.pallas_call(
        _mla_attn_kernel,
        grid=(batch, A_HEADS // n_h),
        in_specs=[qk_blk, qk_blk, vt_blk, v_blk],
        out_specs=v_blk,
        out_shape=jax.ShapeDtypeStruct((batch, seq, A_W), BF16),
        compiler_params=_params("parallel", "parallel"),
        name="mla_attn",
    )(q3, k3, vt, g3).reshape(batch * seq, A_W)


def _cast_once(w_ref, wbf_ref):
    @pl.when(pl.program_id(0) == 0)
    def _():
        wbf_ref[...] = w_ref[...].astype(BF16)


def _mem_kv_kernel(mem_ref, g_ref, w_ref, kv_ref, wbf_ref):
    _cast_once(w_ref, wbf_ref)
    kv_ref[0] = _dot(_rms(mem_ref[0], g_ref[...]).astype(BF16), wbf_ref[...]).astype(BF16)


def _mem_kv(mem, gain, w):
    batch = mem.shape[0]
    return pl.pallas_call(
        _mem_kv_kernel,
        grid=(batch,),
        in_specs=[pl.BlockSpec((1, N_MEM, D_MODEL), lambda b: (b, 0, 0)),
                  _const_spec((1, D_MODEL)), _const_spec(w.shape)],
        out_specs=pl.BlockSpec((1, N_MEM, 2 * C_W), lambda b: (b, 0, 0)),
        out_shape=jax.ShapeDtypeStruct((batch, N_MEM, 2 * C_W), BF16),
        scratch_shapes=[pltpu.VMEM(w.shape, BF16)],
        compiler_params=_params("arbitrary"),
        name="mem_kv",
    )(mem, gain, w)


def _mem_attn_kernel(h_ref, kv_ref, wq_ref, wz_ref, o_ref):
    h = h_ref[...]
    cq = (_dot_nt(h, wq_ref[...].astype(BF16)) * (C_DH ** -0.5)).astype(BF16)
    cz = _dot_nt(h, wz_ref[...].astype(BF16))
    for hd in range(C_HEADS):
        hs = slice(hd * C_DH, (hd + 1) * C_DH)
        s = _dot_nt(cq[:, hs], kv_ref[0, :, hs])
        p = jnp.exp(s - jnp.max(s, axis=-1, keepdims=True))
        l = jnp.sum(p, axis=-1, keepdims=True)
        o = _dot(p.astype(BF16), kv_ref[0, :, C_W + hd * C_DH:C_W + (hd + 1) * C_DH]) / l
        o_ref[:, hs] = (o * _silu(cz[:, hs])).astype(BF16)


def _mem_attn(h, kv, wt, seq):
    T = h.shape[0]
    tm = SEQ_TILE
    per_seq = seq // tm
    return pl.pallas_call(
        _mem_attn_kernel,
        grid=(T // tm,),
        in_specs=[pl.BlockSpec((tm, D_MODEL), lambda i: (i, 0)),
                  pl.BlockSpec((1, N_MEM, 2 * C_W), lambda i: (i // per_seq, 0, 0)),
                  _w_in_rows(IN_OFFS[10], C_W), _w_in_rows(IN_OFFS[11], C_W)],
        out_specs=pl.BlockSpec((tm, C_W), lambda i: (i, 0)),
        out_shape=jax.ShapeDtypeStruct((T, C_W), BF16),
        compiler_params=_params("parallel"),
        name="mem_attn",
    )(h, kv, wt, wt)


MG_TM = 1024
MG_TN = 256


def _merge_kernel(h_ref, hm_ref, ha_ref, hc_ref, wgm_ref, wga_ref, wgc_ref,
                  wbm_ref, wba_ref, wbc_ref, o_ref):
    h = h_ref[...]
    gate = lambda w_ref: _sigmoid(_dot_nt(h, w_ref[...].astype(BF16)))
    branch = lambda a_ref, w_ref: _dot(a_ref[...], w_ref[...].astype(BF16))
    acc = gate(wgm_ref) * branch(hm_ref, wbm_ref)
    acc = acc + gate(wga_ref) * branch(ha_ref, wba_ref)
    acc = acc + gate(wgc_ref) * branch(hc_ref, wbc_ref)
    o_ref[...] = acc.astype(BF16)


def _merge(h, hm, ha, hc, wt, wbm, wba, wbc):
    T = h.shape[0]
    tm, tn = MG_TM, MG_TN
    row = lambda w: pl.BlockSpec((tm, w), lambda i, j: (i, 0))
    col = lambda kdim: pl.BlockSpec((kdim, tn), lambda i, j: (0, j))

    def gate(branch):
        first = IN_OFFS[12] + branch * D_MODEL
        return pl.BlockSpec((pl.Element(tn), pl.Element(D_MODEL)),
                            lambda i, j: (pl.multiple_of(first + j * tn, SUBLANES), 0))

    return pl.pallas_call(
        _merge_kernel,
        grid=(T // tm, D_MODEL // tn),
        in_specs=[row(D_MODEL), row(M_W), row(A_W), row(C_W),
                  gate(0), gate(1), gate(2), col(M_W), col(A_W), col(C_W)],
        out_specs=pl.BlockSpec((tm, tn), lambda i, j: (i, j)),
        out_shape=jax.ShapeDtypeStruct((T, D_MODEL), BF16),
        compiler_params=_params("parallel", "arbitrary"),
        name="merge",
    )(h, hm, ha, hc, wt, wt, wt, wbm, wba, wbc)


def _out_proj_kernel(final, x_ref, m_ref, w_ref, g_ref, o_ref, wbf_ref):
    _cast_once(w_ref, wbf_ref)
    y = x_ref[...] + _dot(m_ref[...], wbf_ref[...])
    o_ref[...] = _rms(y, g_ref[...]) if final else y


def _out_proj(x2, merged, w, gain, final):
    T = x2.shape[0]
    tm = SEQ_TILE
    row = pl.BlockSpec((tm, D_MODEL), lambda i: (i, 0))
    return pl.pallas_call(
        functools.partial(_out_proj_kernel, final),
        grid=(T // tm,),
        in_specs=[row, row, _const_spec(w.shape), _const_spec((1, D_MODEL))],
        out_specs=row,
        out_shape=jax.ShapeDtypeStruct((T, D_MODEL), F32),
        scratch_shapes=[pltpu.VMEM(w.shape, BF16)],
        compiler_params=_params("arbitrary"),
        name="out_proj",
    )(x2, merged, w, gain)


def _rope_lanes(a):
    half = A_ROPE // 2
    z = jnp.zeros(a.shape[:-1] + (half,), a.dtype)
    return jnp.concatenate([a[..., :half], z, a[..., half:], z], axis=-1)


PK_ROWS = 512
PK_SRC = ([b * PK_ROWS for b in range(W_IF // PK_ROWS)]
          + [IN_OFFS[4], IN_OFFS[6], IN_OFFS[7], IN_OFFS[8], IN_OFFS[9], IN_OFFS[9] + PK_ROWS])


def _pack_w_kernel(w_ref, o_ref):
    b = pl.program_id(0)
    x = w_ref[...]
    o_ref[...] = x.astype(BF16)

    @pl.when(b == W_IF // PK_ROWS)
    def _():
        rows = lax.broadcasted_iota(jnp.int32, x.shape, 0)
        o_ref[...] = jnp.where(rows < 2 * M_HEADS, x, 0.0).astype(BF16)

    @pl.when(b == W_KR // PK_ROWS)
    def _():
        half = A_ROPE // 2
        z = jnp.zeros((half, D_MODEL), F32)
        tail = jnp.zeros((PK_ROWS - 4 * half, D_MODEL), F32)
        o_ref[...] = jnp.concatenate([x[:half], z, x[half:2 * half], z, tail],
                                     axis=0).astype(BF16)


def _pack_w_in(wt):
    assert len(PK_SRC) * PK_ROWS == W_ROWS and all(s % SUBLANES == 0 for s in PK_SRC)

    def src(b):
        first = sum(jnp.where(b == i, s, 0) for i, s in enumerate(PK_SRC))
        return pl.multiple_of(first, SUBLANES), 0

    return pl.pallas_call(
        _pack_w_kernel,
        grid=(len(PK_SRC),),
        in_specs=[pl.BlockSpec((pl.Element(PK_ROWS), pl.Element(D_MODEL)), src)],
        out_specs=pl.BlockSpec((PK_ROWS, D_MODEL), lambda b: (b, 0)),
        out_shape=jax.ShapeDtypeStruct((W_ROWS, D_MODEL), BF16),
        compiler_params=_params("parallel"),
        name="pack_w",
    )(wt)


def _layer(x2, pos_row, kv_mem_in, l, final, batch, seq, w_in, b_igate, b_fgate, conv_w, conv_b,
           mh_norm, cq_norm, w_uq, ckv_norm, w_ukv, mem_norm, w_mem_kv, w_br_m, w_br_a, w_br_c,
           w_out, norm, final_norm):
    bf = lambda a: a.astype(BF16)
    row = lambda a: a.reshape(1, -1).astype(F32)

    wt = w_in[l].T
    wall = _pack_w_in(wt)
    uq = w_uq[l].reshape(Q_LORA, A_HEADS, A_DQK)
    wqn = bf(uq[:, :, :A_NOPE].reshape(Q_LORA, A_HEADS * A_NOPE))
    wqr = bf(_rope_lanes(uq[:, :, A_NOPE:]).reshape(Q_LORA, A_HEADS * LANES))
    ukv = w_ukv[l].reshape(KV_LORA, A_HEADS, A_NOPE + A_DV)
    wkn = bf(ukv[:, :, :A_NOPE].reshape(KV_LORA, A_HEADS * A_NOPE))
    wvvt = bf(ukv[:, :, A_NOPE:].reshape(KV_LORA, A_HEADS * A_DV).T)
    bias_row = jnp.pad(jnp.concatenate([b_igate[l], b_fgate[l]]).astype(F32),
                       (0, LANES - 2 * M_HEADS)).reshape(1, LANES)
    invf = (ROPE_THETA ** (-jnp.arange(0, A_ROPE, 2, dtype=F32) / A_ROPE)).reshape(-1, 1)

    h, mq, mk, mvt, mgate, pre_if = _mlstm_proj(x2, row(norm[l]), wall, conv_w[l].astype(F32),
                                                row(conv_b[l]), batch, seq)
    rowf, colf = _gate_scan(pre_if, bias_row, batch, seq)
    hm = _decay_attn(mq, mk, mvt, mgate, rowf, colf, row(mh_norm[l]), batch, seq)

    aq, ak, avt, agate = _mla_proj(h, pos_row, invf, row(cq_norm[l]), row(ckv_norm[l]),
                                   wall, wqn, wqr, wkn, wvvt, batch, seq)
    ha = _mla_attn(aq, ak, avt, agate, batch, seq)

    kv_mem = _mem_kv(kv_mem_in, row(mem_norm[l]), w_mem_kv[l])
    hc = _mem_attn(h, kv_mem, wt, seq)

    merged = _merge(h, hm, ha, hc, wt, w_br_m[l], w_br_a[l], w_br_c[l])
    return _out_proj(x2, merged, w_out[l], row(final_norm), final)


def kernel(x, mem, positions, w_in, b_igate, b_fgate, conv_w, conv_b, mh_norm, cq_norm, w_uq,
           ckv_norm, w_ukv, mem_norm, w_mem_kv, w_br_m, w_br_a, w_br_c, w_out, norm, final_norm):
    batch, seq, d = x.shape
    depth = w_in.shape[0]
    assert d == D_MODEL and seq % MG_TM == 0 and w_in.shape[2] == sum(IN_SPLITS)
    x2 = x.reshape(batch * seq, d)
    pos_row = positions.astype(F32).reshape(batch * seq // SEQ_TILE, 1, SEQ_TILE)
    for l in range(depth):
        x2 = _layer(x2, pos_row, mem, l, l == depth - 1, batch, seq, w_in, b_igate, b_fgate,
                    conv_w, conv_b, mh_norm, cq_norm, w_uq, ckv_norm, w_ukv, mem_norm, w_mem_kv,
                    w_br_m, w_br_a, w_br_c, w_out, norm, final_norm)
    return x2.reshape(batch, seq, d)
```

```python
import functools
import math

import jax
import jax.numpy as jnp
from jax import lax
from jax.experimental import pallas as pl
from jax.experimental.pallas import tpu as pltpu

F32 = jnp.float32
BF16 = jnp.bfloat16

D_MODEL = 2048
M_HEADS, M_DH = 4, 256
M_W = M_HEADS * M_DH
CONV_K = 4
A_HEADS, A_NOPE, A_ROPE, A_DV = 8, 128, 64, 128
A_DQK = A_NOPE + A_ROPE
A_W = A_HEADS * A_DV
Q_LORA = KV_LORA = 512
ROPE_THETA = 10000.0
N_MEM = 256
C_HEADS, C_DH = 4, 256
C_W = C_HEADS * C_DH
EPS = 1e-6

LANES = 128
SUBLANES = 8
A_QK_PAD = 2 * LANES
VMEM_LIMIT = 56 * 1024 * 1024

SEQ_TILE = 512
HEADS_PER_STEP = 2
MLA_HEADS_PER_STEP = 4

IN_SPLITS = (2 * M_W, M_W, M_W, M_W, M_HEADS, M_HEADS, Q_LORA, KV_LORA, A_ROPE, A_W, C_W, C_W,
             3 * D_MODEL)


def _const_spec(shape):
    nd = len(shape)
    return pl.BlockSpec(shape, lambda *_: (0,) * nd, pipeline_mode=pl.Buffered(1))


IN_OFFS = [sum(IN_SPLITS[:i]) for i in range(len(IN_SPLITS) + 1)]

W_QK, W_V, W_O, W_Z = 0, 2048, 3072, 4096
W_AZ, W_CQ, W_CKV, W_KR = 5120, 6144, 6656, 7168
IF_LANE = A_ROPE // 2


def _w_rows(offset, rows):
    assert offset % rows == 0
    return pl.BlockSpec((rows, D_MODEL), lambda *_: (offset // rows, 0),
                        pipeline_mode=pl.Buffered(1))


def _w_in_rows(first, rows):
    assert first % SUBLANES == 0
    return pl.BlockSpec((pl.Element(rows), pl.Element(D_MODEL)), lambda *_: (first, 0),
                        pipeline_mode=pl.Buffered(1))


def _params(*sem):
    return pltpu.CompilerParams(dimension_semantics=sem, vmem_limit_bytes=VMEM_LIMIT)


def _dot(a, b):
    return jnp.dot(a, b, preferred_element_type=F32)


def _dot_nt(a, b):
    return lax.dot_general(a, b, (((1,), (1,)), ((), ())), preferred_element_type=F32)


def _rms(x, g):
    return x * lax.rsqrt(jnp.mean(x * x, axis=-1, keepdims=True) + EPS) * g


def _sigmoid(x):
    return 1.0 / (1.0 + jnp.exp(-x))


def _silu(x):
    return x * _sigmoid(x)


def _keys_le_queries(nk, nq):
    shape = (nk, nq)
    return lax.broadcasted_iota(jnp.int32, shape, 0) <= lax.broadcasted_iota(jnp.int32, shape, 1)


def _causal_pieces(qi, t, nk_full):
    half = t // 2
    return ([(k0, nk_full, 0, False) for k0 in range(0, qi * t, nk_full)]
            + [(qi * t, half, 0, True), (qi * t + half, half, half, True)])


def _left_pad(x, n, fill):
    if n == 0:
        return x
    return jnp.concatenate([jnp.full((x.shape[0], n), fill, x.dtype), x], axis=1)


MP_CH = 512


def _mlstm_proj_kernel(tiles_per_seq, x_ref, ng_ref, wqk_ref, cw_ref, cb_ref, wvt_ref, wo_ref,
                       wz_ref, h_ref, q_ref, k_ref, vt_ref, g_ref, xbuf_ref):
    tm = x_ref.shape[0]

    @pl.when(pl.program_id(0) % tiles_per_seq == 0)
    def _():
        xbuf_ref[0:SUBLANES, :] = jnp.zeros((SUBLANES, 2 * M_W), F32)

    h = _rms(x_ref[...], ng_ref[...]).astype(BF16)
    h_ref[...] = h

    def conv_chunk(c):
        cs = slice(c * MP_CH, (c + 1) * MP_CH)
        acc = _dot_nt(h, wqk_ref[cs, :])
        xbuf_ref[SUBLANES:SUBLANES + tm, cs] = acc
        y = cb_ref[:, cs] + cw_ref[CONV_K - 1:CONV_K, cs] * acc
        for j in range(CONV_K - 1):
            back = CONV_K - 1 - j
            y = y + cw_ref[j:j + 1, cs] * xbuf_ref[SUBLANES - back:SUBLANES - back + tm, cs]
        xbuf_ref[0:SUBLANES, cs] = acc[tm - SUBLANES:tm, :]
        y = _silu(y)
        if c < M_W // MP_CH:
            q_ref[:, cs] = y.astype(BF16)
        else:
            ks = slice(c * MP_CH - M_W, (c + 1) * MP_CH - M_W)
            k_ref[:, ks] = (y * (M_DH ** -0.5)).astype(BF16)

    def value_chunk(c):
        cs = slice(c * MP_CH, (c + 1) * MP_CH)
        vt_ref[0, 0, cs, :] = _dot_nt(wvt_ref[cs, :], h).astype(BF16)

    def gate_chunk(c):
        cs = slice(c * MP_CH, (c + 1) * MP_CH)
        o = _dot_nt(h, wo_ref[cs, :])
        z = _dot_nt(h, wz_ref[cs, :])
        g_ref[:, cs] = (_sigmoid(o) * _silu(z)).astype(BF16)

    light = [functools.partial(f, c) for c in range(M_W // MP_CH) for f in (value_chunk, gate_chunk)]
    for c in range(2 * M_W // MP_CH):
        conv_chunk(c)
        if c < len(light):
            light[c]()
    for f in light[2 * M_W // MP_CH:]:
        f()


def _mlstm_proj(x2, norm_g, wall, conv_w, conv_b, batch, seq):
    T = x2.shape[0]
    tm = SEQ_TILE
    per_seq = seq // tm
    row = lambda w: pl.BlockSpec((tm, w), lambda i: (i, 0))
    out_shape = (
        jax.ShapeDtypeStruct((T, D_MODEL), BF16),
        jax.ShapeDtypeStruct((T, M_W), BF16),
        jax.ShapeDtypeStruct((T, M_W), BF16),
        jax.ShapeDtypeStruct((batch, per_seq, M_W, tm), BF16),
        jax.ShapeDtypeStruct((T, M_W), BF16),
    )
    vt_spec = pl.BlockSpec((1, 1, M_W, tm), lambda i: (i // per_seq, i % per_seq, 0, 0))
    return pl.pallas_call(
        functools.partial(_mlstm_proj_kernel, per_seq),
        grid=(T // tm,),
        in_specs=[row(D_MODEL), _const_spec((1, D_MODEL)), _w_rows(W_QK, 2 * M_W),
                  _const_spec(conv_w.shape), _const_spec(conv_b.shape), _w_rows(W_V, M_W),
                  _w_rows(W_O, M_W), _w_rows(W_Z, M_W)],
        out_specs=(row(D_MODEL), row(M_W), row(M_W), vt_spec, row(M_W)),
        out_shape=out_shape,
        scratch_shapes=[pltpu.VMEM((tm + SUBLANES, 2 * M_W), F32)],
        compiler_params=_params("arbitrary"),
        name="mlstm_proj",
    )(x2, norm_g, wall, conv_w, conv_b, wall, wall, wall)


def _lane_scan(x, op, fill):
    n = x.shape[-1]
    lane = lax.broadcasted_iota(jnp.int32, x.shape, x.ndim - 1)
    d = 1
    while d < n:
        shifted = pltpu.roll(x, d, x.ndim - 1)
        x = op(x, jnp.where(lane >= d, shifted, fill))
        d *= 2
    return x


def _gate_scan_kernel(if_ref, bias_ref, row_ref, col_ref):
    S = if_ref.shape[1]
    t = SEQ_TILE
    pre = if_ref[0] + bias_ref[...]
    t8 = pre.T[IF_LANE:IF_LANE + SUBLANES, :]
    lf = jnp.minimum(t8, 0.0) - jnp.log(1.0 + jnp.exp(-jnp.abs(t8)))
    b = _lane_scan(lf, jnp.add, 0.0)
    b = pltpu.roll(b, M_HEADS, 0)
    a = t8 - b
    mx = jnp.maximum(_lane_scan(a, jnp.maximum, -jnp.inf), 0.0)
    nb = -b - mx
    sub = lax.broadcasted_iota(jnp.int32, (SUBLANES, S), 0)
    zeros = jnp.zeros((LANES - SUBLANES, S), F32)
    n_h = HEADS_PER_STEP
    for p in range(M_HEADS // n_h):
        up = (SUBLANES - n_h * p) % SUBLANES
        m_grp = pltpu.roll(mx, up, 0) if up else mx
        nb_grp = pltpu.roll(nb, (up + n_h) % SUBLANES, 0)
        stack = jnp.where(sub < n_h, m_grp, nb_grp)
        for j in range(S // t):
            row_ref[0, p, j] = stack[:, j * t:(j + 1) * t]
        a_grp = pltpu.roll(a, up, 0) if up else a
        col_ref[0, p] = jnp.concatenate([a_grp, zeros], axis=0).T


def _gate_scan(pre_if, bias_row, batch, seq):
    n_grp = M_HEADS // HEADS_PER_STEP
    n_tile = seq // SEQ_TILE
    return pl.pallas_call(
        _gate_scan_kernel,
        grid=(batch,),
        in_specs=[pl.BlockSpec((1, seq, LANES), lambda b: (b, 0, 0)), _const_spec((1, LANES))],
        out_specs=(pl.BlockSpec((1, n_grp, n_tile, SUBLANES, SEQ_TILE), lambda b: (b, 0, 0, 0, 0)),
                   pl.BlockSpec((1, n_grp, seq, LANES), lambda b: (b, 0, 0, 0))),
        out_shape=(jax.ShapeDtypeStruct((batch, n_grp, n_tile, SUBLANES, SEQ_TILE), F32),
                   jax.ShapeDtypeStruct((batch, n_grp, seq, LANES), F32)),
        compiler_params=_params("parallel"),
        name="gate_scan",
    )(pre_if.reshape(batch, seq, LANES), bias_row)


def _decay_attn_kernel(q_ref, k_ref, vt_ref, g_ref, row_ref, col_ref, gain_ref, o_ref,
                       arep_ref, fac_ref):
    S = q_ref.shape[1]
    t = SEQ_TILE
    n_head = HEADS_PER_STEP
    hs = [slice(hh * M_DH, (hh + 1) * M_DH) for hh in range(n_head)]
    lane_tiles = t // LANES
    widen = lambda a: jnp.concatenate([a] * lane_tiles, axis=1)

    c_rep = [[None] * (S // t) for _ in range(n_head)]
    for hh in range(n_head):
        arep_ref[hh] = jnp.broadcast_to(col_ref[0, 0, :, hh:hh + 1], (S, LANES))
        for kj in range(S // t - 1):
            a = arep_ref[hh, kj * t:(kj + 1) * t, :]
            c_rep[hh][kj] = jnp.max(a, axis=0, keepdims=True)
            fac_ref[hh, kj * t:(kj + 1) * t, :] = jnp.exp(a - c_rep[hh][kj])

    for qi in range(S // t):
        qs = slice(qi * t, (qi + 1) * t)
        for hh in range(n_head):
            q = q_ref[0, qs, hs[hh]]
            m_row = row_ref[0, 0, qi, hh:hh + 1, :]
            nb_row = row_ref[0, 0, qi, n_head + hh:n_head + hh + 1, :]
            den = num = None
            for k0, nk, q0, diag in _causal_pieces(qi, t, t):
                ks = slice(k0, k0 + nk)
                kc, ko = divmod(k0, t)
                st = _dot_nt(k_ref[0, ks, hs[hh]], q[q0:, :])
                if diag:
                    arg = widen(arep_ref[hh, ks, :])[:, q0:] - m_row[:, q0:]
                    p = st * jnp.exp(jnp.where(_keys_le_queries(nk, t - q0), arg, -jnp.inf))
                    d_blk = jnp.sum(p, axis=0, keepdims=True)
                    n_blk = _dot(vt_ref[0, kc, hs[hh], ko:ko + nk], p.astype(BF16))
                else:
                    p = st * widen(fac_ref[hh, ks, :])
                    qfac = jnp.exp(widen(c_rep[hh][kc]) - m_row)
                    d_blk = qfac * jnp.sum(p, axis=0, keepdims=True)
                    n_blk = qfac * _dot(vt_ref[0, kc, hs[hh], ko:ko + nk], p.astype(BF16))
                d_blk, n_blk = _left_pad(d_blk, q0, 0.0), _left_pad(n_blk, q0, 0.0)
                den, num = (d_blk, n_blk) if den is None else (den + d_blk, num + n_blk)
            hv = num / jnp.maximum(jnp.abs(den), jnp.exp(nb_row))
            hv = hv * lax.rsqrt(jnp.mean(hv * hv, axis=0, keepdims=True) + EPS)
            o_ref[0, qs, hs[hh]] = (hv.T * gain_ref[:, hs[hh]]
                                    * g_ref[0, qs, hs[hh]].astype(F32)).astype(BF16)


def _decay_attn(q, k, vt, gate, rowf, colf, gain, batch, seq):
    n_grp = M_HEADS // HEADS_PER_STEP
    n_tile = seq // SEQ_TILE
    w = HEADS_PER_STEP * M_DH
    blk = pl.BlockSpec((1, seq, w), lambda b, p: (b, 0, p))
    q3, k3, g3 = (a.reshape(batch, seq, M_W) for a in (q, k, gate))
    return pl.pallas_call(
        _decay_attn_kernel,
        grid=(batch, n_grp),
        in_specs=[blk, blk,
                  pl.BlockSpec((1, n_tile, w, SEQ_TILE), lambda b, p: (b, 0, p, 0)),
                  blk,
                  pl.BlockSpec((1, 1, n_tile, SUBLANES, SEQ_TILE), lambda b, p: (b, p, 0, 0, 0)),
                  pl.BlockSpec((1, 1, seq, LANES), lambda b, p: (b, p, 0, 0)),
                  pl.BlockSpec((1, w), lambda b, p: (0, p))],
        out_specs=blk,
        out_shape=jax.ShapeDtypeStruct((batch, seq, M_W), BF16),
        scratch_shapes=[pltpu.VMEM((HEADS_PER_STEP, seq, LANES), F32),
                        pltpu.VMEM((HEADS_PER_STEP, seq - SEQ_TILE, LANES), F32)],
        compiler_params=_params("parallel", "parallel"),
        name="decay_attn",
    )(q3, k3, vt, g3, rowf, colf, gain).reshape(batch * seq, M_W)


def _mla_proj_kernel(h_ref, pos_ref, invf_ref, gq_ref, gkv_ref, wcq_ref, wckv_ref,
                     wkr_ref, waz_ref, wqn_ref, wqr_ref, wkn_ref, wvt_ref,
                     q_ref, k_ref, vt_ref, g_ref, if_ref):
    h = h_ref[...]
    tm = h.shape[0]
    scale = math.log2(math.e) / math.sqrt(A_DQK)
    ang = invf_ref[...] * pos_ref[0]
    c32, s32 = jnp.cos(ang), jnp.sin(ang)
    z32 = jnp.zeros((A_ROPE // 2, tm), F32)
    cos = jnp.concatenate([c32, z32, c32, z32], axis=0).T
    sin = jnp.concatenate([-s32, z32, s32, z32], axis=0).T

    def rope(r):
        return r * cos + pltpu.roll(r, LANES // 2, 1) * sin

    cq = _dot_nt(h, wcq_ref[...])
    ckv = _dot_nt(h, wckv_ref[...])
    g_ref[...] = _silu(_dot_nt(h, waz_ref[...])).astype(BF16)
    kr_if = _dot_nt(h, wkr_ref[...])
    if_ref[...] = kr_if
    kr = rope(kr_if).astype(BF16)

    cqn = _rms(cq, gq_ref[...]).astype(BF16)
    ckvn = _rms(ckv, gkv_ref[...]).astype(BF16)
    qn = _dot(cqn, wqn_ref[...]) * scale
    qr = _dot(cqn, wqr_ref[...]) * scale
    kn = _dot(ckvn, wkn_ref[...])
    for hd in range(A_HEADS):
        base = hd * A_QK_PAD
        ls = slice(hd * LANES, (hd + 1) * LANES)
        q_ref[:, base:base + LANES] = qn[:, ls].astype(BF16)
        q_ref[:, base + LANES:base + A_QK_PAD] = rope(qr[:, ls]).astype(BF16)
        k_ref[:, base:base + LANES] = kn[:, ls].astype(BF16)
        k_ref[:, base + LANES:base + A_QK_PAD] = kr
    vt_ref[0, 0] = _dot_nt(wvt_ref[...], ckvn).astype(BF16)


def _mla_proj(h, pos_row, invf, gq, gkv, wall, wqn, wqr, wkn, wvt, batch, seq):
    T = h.shape[0]
    tm = SEQ_TILE
    per_seq = seq // tm
    row = lambda w: pl.BlockSpec((tm, w), lambda i: (i, 0))
    small = (invf, gq, gkv)
    ups = (wqn, wqr, wkn, wvt)
    vt_spec = pl.BlockSpec((1, 1, A_W, tm), lambda i: (i // per_seq, i % per_seq, 0, 0))
    return pl.pallas_call(
        _mla_proj_kernel,
        grid=(T // tm,),
        in_specs=([row(D_MODEL), pl.BlockSpec((1, 1, tm), lambda i: (i, 0, 0))]
                  + [_const_spec(c.shape) for c in small]
                  + [_w_rows(W_CQ, Q_LORA), _w_rows(W_CKV, KV_LORA), _w_rows(W_KR, LANES),
                     _w_rows(W_AZ, A_W)]
                  + [_const_spec(c.shape) for c in ups]),
        out_specs=(row(A_HEADS * A_QK_PAD), row(A_HEADS * A_QK_PAD), vt_spec, row(A_W),
                   row(LANES)),
        out_shape=(jax.ShapeDtypeStruct((T, A_HEADS * A_QK_PAD), BF16),
                   jax.ShapeDtypeStruct((T, A_HEADS * A_QK_PAD), BF16),
                   jax.ShapeDtypeStruct((batch, per_seq, A_W, tm), BF16),
                   jax.ShapeDtypeStruct((T, A_W), BF16),
                   jax.ShapeDtypeStruct((T, LANES), F32)),
        compiler_params=_params("parallel"),
        name="mla_proj",
    )(h, pos_row, *small, wall, wall, wall, wall, *ups)


def _mla_attn_kernel(q_ref, k_ref, vt_ref, g_ref, o_ref):
    S = q_ref.shape[1]
    t = SEQ_TILE
    n_head = MLA_HEADS_PER_STEP
    hq = [slice(hh * A_QK_PAD, (hh + 1) * A_QK_PAD) for hh in range(n_head)]
    hv = [slice(hh * A_DV, (hh + 1) * A_DV) for hh in range(n_head)]

    for qi in range(S // t):
        qs = slice(qi * t, (qi + 1) * t)
        for hh in range(n_head):
            q = q_ref[0, qs, hq[hh]]
            pieces = _causal_pieces(qi, t, t // 2)
            st = []
            for k0, nk, q0, diag in pieces:
                s = _dot_nt(k_ref[0, k0:k0 + nk, hq[hh]], q[q0:, :])
                st.append(jnp.where(_keys_le_queries(nk, t - q0), s, -jnp.inf) if diag else s)
            m = functools.reduce(jnp.maximum, [
                _left_pad(jnp.max(s, axis=0, keepdims=True), q0, -jnp.inf)
                for s, (_, _, q0, _) in zip(st, pieces)])
            l = acc = None
            for s, (k0, nk, q0, _) in zip(st, pieces):
                p = jnp.exp2(s - m[:, q0:])
                l_blk = _left_pad(jnp.sum(p, axis=0, keepdims=True), q0, 0.0)
                kc, ko = divmod(k0, t)
                pv = _left_pad(_dot(vt_ref[0, kc, hv[hh], ko:ko + nk], p.astype(BF16)), q0, 0.0)
                l, acc = (l_blk, pv) if l is None else (l + l_blk, acc + pv)
            o_ref[0, qs, hv[hh]] = ((acc / l).T * g_ref[0, qs, hv[hh]].astype(F32)).astype(BF16)


def _mla_attn(q, k, vt, gate, batch, seq):
    n_tile = seq // SEQ_TILE
    n_h = MLA_HEADS_PER_STEP
    qk_blk = pl.BlockSpec((1, seq, n_h * A_QK_PAD), lambda b, p: (b, 0, p))
    v_blk = pl.BlockSpec((1, seq, n_h * A_DV), lambda b, p: (b, 0, p))
    vt_blk = pl.BlockSpec((1, n_tile, n_h * A_DV, SEQ_TILE), lambda b, p: (b, 0, p, 0))
    q3 = q.reshape(batch, seq, A_HEADS * A_QK_PAD)
    k3 = k.reshape(batch, seq, A_HEADS * A_QK_PAD)
    g3 = gate.reshape(batch, seq, A_W)
    return pl.pallas_call(
        _mla_attn_kernel,
        grid=(batch, A_HEADS // n_h),
        in_specs=[qk_blk, qk_blk, vt_blk, v_blk],
        out_specs=v_blk,
        out_shape=jax.ShapeDtypeStruct((batch, seq, A_W), BF16),
        compiler_params=_params("parallel", "parallel"),
        name="mla_attn",
    )(q3, k3, vt, g3).reshape(batch * seq, A_W)


def _cast_once(w_ref, wbf_ref):
    @pl.when(pl.program_id(0) == 0)
    def _():
        wbf_ref[...] = w_ref[...].astype(BF16)


def _mem_kv_kernel(mem_ref, g_ref, w_ref, kv_ref, wbf_ref):
    _cast_once(w_ref, wbf_ref)
    kv_ref[0] = _dot(_rms(mem_ref[0], g_ref[...]).astype(BF16), wbf_ref[...]).astype(BF16)


def _mem_kv(mem, gain, w):
    batch = mem.shape[0]
    return pl.pallas_call(
        _mem_kv_kernel,
        grid=(batch,),
        in_specs=[pl.BlockSpec((1, N_MEM, D_MODEL), lambda b: (b, 0, 0)),
                  _const_spec((1, D_MODEL)), _const_spec(w.shape)],
        out_specs=pl.BlockSpec((1, N_MEM, 2 * C_W), lambda b: (b, 0, 0)),
        out_shape=jax.ShapeDtypeStruct((batch, N_MEM, 2 * C_W), BF16),
        scratch_shapes=[pltpu.VMEM(w.shape, BF16)],
        compiler_params=_params("arbitrary"),
        name="mem_kv",
    )(mem, gain, w)


def _mem_attn_kernel(h_ref, kv_ref, wq_ref, wz_ref, o_ref):
    h = h_ref[...]
    cq = (_dot_nt(h, wq_ref[...].astype(BF16)) * (C_DH ** -0.5)).astype(BF16)
    for hd in range(C_HEADS):
        hs = slice(hd * C_DH, (hd + 1) * C_DH)
        s = _dot_nt(cq[:, hs], kv_ref[0, :, hs])
        cz = _dot_nt(h, wz_ref[hs, :].astype(BF16))
        p = jnp.exp(s - jnp.max(s, axis=-1, keepdims=True))
        l = jnp.sum(p, axis=-1, keepdims=True)
        o = _dot(p.astype(BF16), kv_ref[0, :, C_W + hd * C_DH:C_W + (hd + 1) * C_DH]) / l
        o_ref[:, hs] = (o * _silu(cz)).astype(BF16)


def _mem_attn(h, kv, wt, seq):
    T = h.shape[0]
    tm = SEQ_TILE
    per_seq = seq // tm
    return pl.pallas_call(
        _mem_attn_kernel,
        grid=(T // tm,),
        in_specs=[pl.BlockSpec((tm, D_MODEL), lambda i: (i, 0)),
                  pl.BlockSpec((1, N_MEM, 2 * C_W), lambda i: (i // per_seq, 0, 0)),
                  _w_in_rows(IN_OFFS[10], C_W), _w_in_rows(IN_OFFS[11], C_W)],
        out_specs=pl.BlockSpec((tm, C_W), lambda i: (i, 0)),
        out_shape=jax.ShapeDtypeStruct((T, C_W), BF16),
        compiler_params=_params("parallel"),
        name="mem_attn",
    )(h, kv, wt, wt)


MG_TM = 1024
MG_TN = 256


def _merge_kernel(h_ref, hm_ref, ha_ref, hc_ref, wgm_ref, wga_ref, wgc_ref,
                  wbm_ref, wba_ref, wbc_ref, o_ref):
    h = h_ref[...]
    gate = lambda w_ref: _sigmoid(_dot_nt(h, w_ref[...].astype(BF16)))
    branch = lambda a_ref, w_ref: _dot(a_ref[...], w_ref[...].astype(BF16))
    acc = gate(wgm_ref) * branch(hm_ref, wbm_ref)
    acc = acc + gate(wga_ref) * branch(ha_ref, wba_ref)
    acc = acc + gate(wgc_ref) * branch(hc_ref, wbc_ref)
    o_ref[...] = acc.astype(BF16)


def _merge(h, hm, ha, hc, wt, wbm, wba, wbc):
    T = h.shape[0]
    tm, tn = MG_TM, MG_TN
    row = lambda w: pl.BlockSpec((tm, w), lambda i, j: (i, 0))
    col = lambda kdim: pl.BlockSpec((kdim, tn), lambda i, j: (0, j))

    def gate(branch):
        first = IN_OFFS[12] + branch * D_MODEL
        return pl.BlockSpec((pl.Element(tn), pl.Element(D_MODEL)),
                            lambda i, j: (pl.multiple_of(first + j * tn, SUBLANES), 0))

    return pl.pallas_call(
        _merge_kernel,
        grid=(T // tm, D_MODEL // tn),
        in_specs=[row(D_MODEL), row(M_W), row(A_W), row(C_W),
                  gate(0), gate(1), gate(2), col(M_W), col(A_W), col(C_W)],
        out_specs=pl.BlockSpec((tm, tn), lambda i, j: (i, j)),
        out_shape=jax.ShapeDtypeStruct((T, D_MODEL), BF16),
        compiler_params=_params("parallel", "arbitrary"),
        name="merge",
    )(h, hm, ha, hc, wt, wt, wt, wbm, wba, wbc)


def _out_proj_kernel(final, x_ref, m_ref, w_ref, g_ref, o_ref, wbf_ref):
    _cast_once(w_ref, wbf_ref)
    y = x_ref[...] + _dot(m_ref[...], wbf_ref[...])
    o_ref[...] = _rms(y, g_ref[...]) if final else y


def _out_proj(x2, merged, w, gain, final):
    T = x2.shape[0]
    tm = SEQ_TILE
    row = pl.BlockSpec((tm, D_MODEL), lambda i: (i, 0))
    return pl.pallas_call(
        functools.partial(_out_proj_kernel, final),
        grid=(T // tm,),
        in_specs=[row, row, _const_spec(w.shape), _const_spec((1, D_MODEL))],
        out_specs=row,
        out_shape=jax.ShapeDtypeStruct((T, D_MODEL), F32),
        scratch_shapes=[pltpu.VMEM(w.shape, BF16)],
        compiler_params=_params("arbitrary"),
        name="out_proj",
    )(x2, merged, w, gain)


def _rope_lanes(a):
    half = A_ROPE // 2
    z = jnp.zeros(a.shape[:-1] + (half,), a.dtype)
    return jnp.concatenate([a[..., :half], z, a[..., half:], z], axis=-1)


PK_ROWS = 512
W_ROWS = W_KR + PK_ROWS
PK_SRC = ([b * PK_ROWS for b in range(IN_OFFS[4] // PK_ROWS)]
          + [IN_OFFS[9], IN_OFFS[9] + PK_ROWS, IN_OFFS[6], IN_OFFS[7], IN_OFFS[8]])


def _pack_w_kernel(w_ref, wif_ref, o_ref):
    b = pl.program_id(0)
    x = w_ref[...]
    o_ref[...] = x.astype(BF16)

    @pl.when(b == W_KR // PK_ROWS)
    def _():
        half = A_ROPE // 2
        n_if = 2 * M_HEADS
        zeros = lambda n: jnp.zeros((n, D_MODEL), F32)
        o_ref[...] = jnp.concatenate(
            [x[:half], wif_ref[...], zeros(half - n_if), x[half:2 * half], zeros(half),
             zeros(PK_ROWS - 4 * half)], axis=0).astype(BF16)


def _pack_w_in(wt):
    assert len(PK_SRC) * PK_ROWS == W_ROWS and all(s % SUBLANES == 0 for s in PK_SRC)

    def src(b):
        first = sum(jnp.where(b == i, s, 0) for i, s in enumerate(PK_SRC))
        return pl.multiple_of(first, SUBLANES), 0

    return pl.pallas_call(
        _pack_w_kernel,
        grid=(len(PK_SRC),),
        in_specs=[pl.BlockSpec((pl.Element(PK_ROWS), pl.Element(D_MODEL)), src),
                  pl.BlockSpec((pl.Element(2 * M_HEADS), pl.Element(D_MODEL)),
                               lambda b: (IN_OFFS[4], 0))],
        out_specs=pl.BlockSpec((PK_ROWS, D_MODEL), lambda b: (b, 0)),
        out_shape=jax.ShapeDtypeStruct((W_ROWS, D_MODEL), BF16),
        compiler_params=_params("parallel"),
        name="pack_w",
    )(wt, wt)


def _layer(x2, pos_row, kv_mem_in, l, final, batch, seq, w_in, b_igate, b_fgate, conv_w, conv_b,
           mh_norm, cq_norm, w_uq, ckv_norm, w_ukv, mem_norm, w_mem_kv, w_br_m, w_br_a, w_br_c,
           w_out, norm, final_norm):
    bf = lambda a: a.astype(BF16)
    row = lambda a: a.reshape(1, -1).astype(F32)

    wt = w_in[l].T
    wall = _pack_w_in(wt)
    uq = w_uq[l].reshape(Q_LORA, A_HEADS, A_DQK)
    wqn = bf(uq[:, :, :A_NOPE].reshape(Q_LORA, A_HEADS * A_NOPE))
    wqr = bf(_rope_lanes(uq[:, :, A_NOPE:]).reshape(Q_LORA, A_HEADS * LANES))
    ukv = w_ukv[l].reshape(KV_LORA, A_HEADS, A_NOPE + A_DV)
    wkn = bf(ukv[:, :, :A_NOPE].reshape(KV_LORA, A_HEADS * A_NOPE))
    wvvt = bf(ukv[:, :, A_NOPE:].reshape(KV_LORA, A_HEADS * A_DV).T)
    bias_row = jnp.pad(jnp.concatenate([b_igate[l], b_fgate[l]]).astype(F32),
                       (IF_LANE, LANES - IF_LANE - 2 * M_HEADS)).reshape(1, LANES)
    invf = (ROPE_THETA ** (-jnp.arange(0, A_ROPE, 2, dtype=F32) / A_ROPE)).reshape(-1, 1)

    h, mq, mk, mvt, mgate = _mlstm_proj(x2, row(norm[l]), wall, conv_w[l].astype(F32),
                                        row(conv_b[l]), batch, seq)
    aq, ak, avt, agate, pre_if = _mla_proj(h, pos_row, invf, row(cq_norm[l]), row(ckv_norm[l]),
                                           wall, wqn, wqr, wkn, wvvt, batch, seq)
    rowf, colf = _gate_scan(pre_if, bias_row, batch, seq)
    hm = _decay_attn(mq, mk, mvt, mgate, rowf, colf, row(mh_norm[l]), batch, seq)
    ha = _mla_attn(aq, ak, avt, agate, batch, seq)

    kv_mem = _mem_kv(kv_mem_in, row(mem_norm[l]), w_mem_kv[l])
    hc = _mem_attn(h, kv_mem, wt, seq)

    merged = _merge(h, hm, ha, hc, wt, w_br_m[l], w_br_a[l], w_br_c[l])
    return _out_proj(x2, merged, w_out[l], row(final_norm), final)


def kernel(x, mem, positions, w_in, b_igate, b_fgate, conv_w, conv_b, mh_norm, cq_norm, w_uq,
           ckv_norm, w_ukv, mem_norm, w_mem_kv, w_br_m, w_br_a, w_br_c, w_out, norm, final_norm):
    batch, seq, d = x.shape
    depth = w_in.shape[0]
    assert d == D_MODEL and seq % MG_TM == 0 and w_in.shape[2] == sum(IN_SPLITS)
    x2 = x.reshape(batch * seq, d)
    pos_row = positions.astype(F32).reshape(batch * seq // SEQ_TILE, 1, SEQ_TILE)
    for l in range(depth):
        x2 = _layer(x2, pos_row, mem, l, l == depth - 1, batch, seq, w_in, b_igate, b_fgate,
                    conv_w, conv_b, mh_norm, cq_norm, w_uq, ckv_norm, w_ukv, mem_norm, w_mem_kv,
                    w_br_m, w_br_a, w_br_c, w_out, norm, final_norm)
    return x2.reshape(batch, seq, d)
```

```python
import functools
import math

import jax
import jax.numpy as jnp
from jax import lax
from jax.experimental import pallas as pl
from jax.experimental.pallas import tpu as pltpu

F32 = jnp.float32
BF16 = jnp.bfloat16

D_MODEL = 2048
M_HEADS, M_DH = 4, 256
M_W = M_HEADS * M_DH
CONV_K = 4
A_HEADS, A_NOPE, A_ROPE, A_DV = 8, 128, 64, 128
A_DQK = A_NOPE + A_ROPE
A_W = A_HEADS * A_DV
Q_LORA = KV_LORA = 512
ROPE_THETA = 10000.0
N_MEM = 256
C_HEADS, C_DH = 4, 256
C_W = C_HEADS * C_DH
EPS = 1e-6

LANES = 128
SUBLANES = 8
A_QK_PAD = 2 * LANES
VMEM_LIMIT = 56 * 1024 * 1024

SEQ_TILE = 512
HEADS_PER_STEP = 2
MLA_HEADS_PER_STEP = 4

IN_SPLITS = (2 * M_W, M_W, M_W, M_W, M_HEADS, M_HEADS, Q_LORA, KV_LORA, A_ROPE, A_W, C_W, C_W,
             3 * D_MODEL)


def _const_spec(shape):
    nd = len(shape)
    return pl.BlockSpec(shape, lambda *_: (0,) * nd, pipeline_mode=pl.Buffered(1))


IN_OFFS = [sum(IN_SPLITS[:i]) for i in range(len(IN_SPLITS) + 1)]

W_QK, W_V, W_O, W_Z = 0, 2048, 3072, 4096
W_AZ, W_CQ, W_CKV, W_KR = 5120, 6144, 6656, 7168
IF_LANE = A_ROPE // 2


def _w_rows(offset, rows):
    assert offset % rows == 0
    return pl.BlockSpec((rows, D_MODEL), lambda *_: (offset // rows, 0),
                        pipeline_mode=pl.Buffered(1))


def _w_in_rows(first, rows):
    assert first % SUBLANES == 0
    return pl.BlockSpec((pl.Element(rows), pl.Element(D_MODEL)), lambda *_: (first, 0),
                        pipeline_mode=pl.Buffered(1))


def _params(*sem):
    return pltpu.CompilerParams(dimension_semantics=sem, vmem_limit_bytes=VMEM_LIMIT)


def _dot(a, b):
    return jnp.dot(a, b, preferred_element_type=F32)


def _dot_nt(a, b):
    return lax.dot_general(a, b, (((1,), (1,)), ((), ())), preferred_element_type=F32)


def _rms(x, g):
    return x * lax.rsqrt(jnp.mean(x * x, axis=-1, keepdims=True) + EPS) * g


def _sigmoid(x):
    return 1.0 / (1.0 + jnp.exp(-x))


def _silu(x):
    return x * _sigmoid(x)


def _keys_le_queries(nk, nq):
    shape = (nk, nq)
    return lax.broadcasted_iota(jnp.int32, shape, 0) <= lax.broadcasted_iota(jnp.int32, shape, 1)


def _causal_pieces(qi, t, nk_full):
    half = t // 2
    return ([(k0, nk_full, 0, False) for k0 in range(0, qi * t, nk_full)]
            + [(qi * t, half, 0, True), (qi * t + half, half, half, True)])


def _left_pad(x, n, fill):
    if n == 0:
        return x
    return jnp.concatenate([jnp.full((x.shape[0], n), fill, x.dtype), x], axis=1)


MP_CH = 512


def _mlstm_proj_kernel(tiles_per_seq, x_ref, ng_ref, wqk_ref, cw_ref, cb_ref, wvt_ref, wo_ref,
                       wz_ref, h_ref, qt_ref, k_ref, vt_ref, g_ref, xbuf_ref):
    tm = x_ref.shape[0]

    @pl.when(pl.program_id(0) % tiles_per_seq == 0)
    def _():
        xbuf_ref[...] = jnp.zeros_like(xbuf_ref)

    h = _rms(x_ref[...], ng_ref[...]).astype(BF16)
    h_ref[...] = h
    widen = lambda a: jnp.concatenate([a] * (tm // LANES), axis=1)

    def conv_chunk(c):
        fs = slice(c * MP_CH, (c + 1) * MP_CH)
        acc = _dot_nt(wqk_ref[fs, :], h)
        prev = xbuf_ref[fs, :]
        lane = lax.broadcasted_iota(jnp.int32, prev.shape, 1)
        y = widen(cb_ref[fs, :]) + widen(cw_ref[CONV_K - 1, fs, :]) * acc
        for j in range(CONV_K - 1):
            back = CONV_K - 1 - j
            rolled = pltpu.roll(acc, back, 1)
            head = jnp.where(lane < back, pltpu.roll(prev, back, 1), rolled[:, :LANES])
            tap = jnp.concatenate([head, rolled[:, LANES:]], axis=1)
            y = y + widen(cw_ref[j, fs, :]) * tap
        xbuf_ref[fs, :] = acc[:, tm - LANES:]
        y = _silu(y)
        if c < M_W // MP_CH:
            qt_ref[0, 0, fs, :] = y.astype(BF16)
        else:
            ks = slice(c * MP_CH - M_W, (c + 1) * MP_CH - M_W)
            k_ref[:, ks] = (y.T * (M_DH ** -0.5)).astype(BF16)

    def value_chunk(c):
        cs = slice(c * MP_CH, (c + 1) * MP_CH)
        vt_ref[0, 0, cs, :] = _dot_nt(wvt_ref[cs, :], h).astype(BF16)

    def gate_chunk(c):
        cs = slice(c * MP_CH, (c + 1) * MP_CH)
        o = _dot_nt(h, wo_ref[cs, :])
        z = _dot_nt(h, wz_ref[cs, :])
        g_ref[:, cs] = (_sigmoid(o) * _silu(z)).astype(BF16)

    light = [functools.partial(f, c) for c in range(M_W // MP_CH) for f in (value_chunk, gate_chunk)]
    for c in range(2 * M_W // MP_CH):
        conv_chunk(c)
        if c < len(light):
            light[c]()
    for f in light[2 * M_W // MP_CH:]:
        f()


def _mlstm_proj(x2, norm_g, wall, conv_w, conv_b, batch, seq):
    T = x2.shape[0]
    tm = SEQ_TILE
    per_seq = seq // tm
    row = lambda w: pl.BlockSpec((tm, w), lambda i: (i, 0))
    out_shape = (
        jax.ShapeDtypeStruct((T, D_MODEL), BF16),
        jax.ShapeDtypeStruct((batch, per_seq, M_W, tm), BF16),
        jax.ShapeDtypeStruct((T, M_W), BF16),
        jax.ShapeDtypeStruct((batch, per_seq, M_W, tm), BF16),
        jax.ShapeDtypeStruct((T, M_W), BF16),
    )
    vt_spec = pl.BlockSpec((1, 1, M_W, tm), lambda i: (i // per_seq, i % per_seq, 0, 0))
    return pl.pallas_call(
        functools.partial(_mlstm_proj_kernel, per_seq),
        grid=(T // tm,),
        in_specs=[row(D_MODEL), _const_spec((1, D_MODEL)), _w_rows(W_QK, 2 * M_W),
                  _const_spec(conv_w.shape), _const_spec(conv_b.shape), _w_rows(W_V, M_W),
                  _w_rows(W_O, M_W), _w_rows(W_Z, M_W)],
        out_specs=(row(D_MODEL), vt_spec, row(M_W), vt_spec, row(M_W)),
        out_shape=out_shape,
        scratch_shapes=[pltpu.VMEM((2 * M_W, LANES), F32)],
        compiler_params=_params("arbitrary"),
        name="mlstm_proj",
    )(x2, norm_g, wall, conv_w, conv_b, wall, wall, wall)


def _lane_scan(x, op, fill):
    n = x.shape[-1]
    lane = lax.broadcasted_iota(jnp.int32, x.shape, x.ndim - 1)
    d = 1
    while d < n:
        shifted = pltpu.roll(x, d, x.ndim - 1)
        x = op(x, jnp.where(lane >= d, shifted, fill))
        d *= 2
    return x


def _gate_scan_kernel(if_ref, bias_ref, row_ref, col_ref):
    S = if_ref.shape[1]
    t = SEQ_TILE
    pre = if_ref[0] + bias_ref[...]
    t8 = pre.T[IF_LANE:IF_LANE + SUBLANES, :]
    lf = jnp.minimum(t8, 0.0) - jnp.log(1.0 + jnp.exp(-jnp.abs(t8)))
    b = _lane_scan(lf, jnp.add, 0.0)
    b = pltpu.roll(b, M_HEADS, 0)
    a = t8 - b
    mx = jnp.maximum(_lane_scan(a, jnp.maximum, -jnp.inf), 0.0)
    nb = -b - mx
    sub = lax.broadcasted_iota(jnp.int32, (SUBLANES, S), 0)
    zeros = jnp.zeros((LANES - SUBLANES, S), F32)
    n_h = HEADS_PER_STEP
    for p in range(M_HEADS // n_h):
        up = (SUBLANES - n_h * p) % SUBLANES
        m_grp = pltpu.roll(mx, up, 0) if up else mx
        nb_grp = pltpu.roll(nb, (up + n_h) % SUBLANES, 0)
        stack = jnp.where(sub < n_h, m_grp, nb_grp)
        for j in range(S // t):
            row_ref[0, p, j] = stack[:, j * t:(j + 1) * t]
        a_grp = pltpu.roll(a, up, 0) if up else a
        col_ref[0, p] = jnp.concatenate([a_grp, zeros], axis=0).T


def _gate_scan(pre_if, bias_row, batch, seq):
    n_grp = M_HEADS // HEADS_PER_STEP
    n_tile = seq // SEQ_TILE
    return pl.pallas_call(
        _gate_scan_kernel,
        grid=(batch,),
        in_specs=[pl.BlockSpec((1, seq, LANES), lambda b: (b, 0, 0)), _const_spec((1, LANES))],
        out_specs=(pl.BlockSpec((1, n_grp, n_tile, SUBLANES, SEQ_TILE), lambda b: (b, 0, 0, 0, 0)),
                   pl.BlockSpec((1, n_grp, seq, LANES), lambda b: (b, 0, 0, 0))),
        out_shape=(jax.ShapeDtypeStruct((batch, n_grp, n_tile, SUBLANES, SEQ_TILE), F32),
                   jax.ShapeDtypeStruct((batch, n_grp, seq, LANES), F32)),
        compiler_params=_params("parallel"),
        name="gate_scan",
    )(pre_if.reshape(batch, seq, LANES), bias_row)


def _decay_attn_kernel(qt_ref, k_ref, vt_ref, g_ref, row_ref, col_ref, gain_ref, o_ref,
                       arep_ref, fac_ref):
    S = k_ref.shape[1]
    t = SEQ_TILE
    n_head = HEADS_PER_STEP
    hs = [slice(hh * M_DH, (hh + 1) * M_DH) for hh in range(n_head)]
    lane_tiles = t // LANES
    widen = lambda a: jnp.concatenate([a] * lane_tiles, axis=1)

    c_rep = [[None] * (S // t) for _ in range(n_head)]
    for hh in range(n_head):
        arep_ref[hh] = jnp.broadcast_to(col_ref[0, 0, :, hh:hh + 1], (S, LANES))
        for kj in range(S // t - 1):
            a = arep_ref[hh, kj * t:(kj + 1) * t, :]
            c_rep[hh][kj] = jnp.max(a, axis=0, keepdims=True)
            fac_ref[hh, kj * t:(kj + 1) * t, :] = jnp.exp(a - c_rep[hh][kj])

    for qi in range(S // t):
        qs = slice(qi * t, (qi + 1) * t)
        for hh in range(n_head):
            qt = qt_ref[0, qi, hs[hh], :]
            m_row = row_ref[0, 0, qi, hh:hh + 1, :]
            nb_row = row_ref[0, 0, qi, n_head + hh:n_head + hh + 1, :]
            den = num = None
            for k0, nk, q0, diag in _causal_pieces(qi, t, t):
                ks = slice(k0, k0 + nk)
                kc, ko = divmod(k0, t)
                st = _dot(k_ref[0, ks, hs[hh]], qt[:, q0:])
                if diag:
                    arg = widen(arep_ref[hh, ks, :])[:, q0:] - m_row[:, q0:]
                    p = st * jnp.exp(jnp.where(_keys_le_queries(nk, t - q0), arg, -jnp.inf))
                    d_blk = jnp.sum(p, axis=0, keepdims=True)
                    n_blk = _dot(vt_ref[0, kc, hs[hh], ko:ko + nk], p.astype(BF16))
                else:
                    p = st * widen(fac_ref[hh, ks, :])
                    qfac = jnp.exp(widen(c_rep[hh][kc]) - m_row)
                    d_blk = qfac * jnp.sum(p, axis=0, keepdims=True)
                    n_blk = qfac * _dot(vt_ref[0, kc, hs[hh], ko:ko + nk], p.astype(BF16))
                d_blk, n_blk = _left_pad(d_blk, q0, 0.0), _left_pad(n_blk, q0, 0.0)
                den, num = (d_blk, n_blk) if den is None else (den + d_blk, num + n_blk)
            hv = num / jnp.maximum(jnp.abs(den), jnp.exp(nb_row))
            hv = hv * lax.rsqrt(jnp.mean(hv * hv, axis=0, keepdims=True) + EPS)
            o_ref[0, qs, hs[hh]] = (hv.T * gain_ref[:, hs[hh]]
                                    * g_ref[0, qs, hs[hh]].astype(F32)).astype(BF16)


def _decay_attn(qt, k, vt, gate, rowf, colf, gain, batch, seq):
    n_grp = M_HEADS // HEADS_PER_STEP
    n_tile = seq // SEQ_TILE
    w = HEADS_PER_STEP * M_DH
    blk = pl.BlockSpec((1, seq, w), lambda b, p: (b, 0, p))
    t_blk = pl.BlockSpec((1, n_tile, w, SEQ_TILE), lambda b, p: (b, 0, p, 0))
    k3, g3 = (a.reshape(batch, seq, M_W) for a in (k, gate))
    return pl.pallas_call(
        _decay_attn_kernel,
        grid=(batch, n_grp),
        in_specs=[t_blk, blk, t_blk, blk,
                  pl.BlockSpec((1, 1, n_tile, SUBLANES, SEQ_TILE), lambda b, p: (b, p, 0, 0, 0)),
                  pl.BlockSpec((1, 1, seq, LANES), lambda b, p: (b, p, 0, 0)),
                  pl.BlockSpec((1, w), lambda b, p: (0, p))],
        out_specs=blk,
        out_shape=jax.ShapeDtypeStruct((batch, seq, M_W), BF16),
        scratch_shapes=[pltpu.VMEM((HEADS_PER_STEP, seq, LANES), F32),
                        pltpu.VMEM((HEADS_PER_STEP, seq - SEQ_TILE, LANES), F32)],
        compiler_params=_params("parallel", "parallel"),
        name="decay_attn",
    )(qt, k3, vt, g3, rowf, colf, gain).reshape(batch * seq, M_W)


def _mla_proj_kernel(h_ref, pos_ref, invf_ref, gq_ref, gkv_ref, wcq_ref, wckv_ref,
                     wkr_ref, waz_ref, wqn_ref, wqr_ref, wkn_ref, wvt_ref,
                     q_ref, k_ref, vt_ref, g_ref, if_ref):
    h = h_ref[...]
    tm = h.shape[0]
    scale = math.log2(math.e) / math.sqrt(A_DQK)
    ang = invf_ref[...] * pos_ref[0]
    c32, s32 = jnp.cos(ang), jnp.sin(ang)
    z32 = jnp.zeros((A_ROPE // 2, tm), F32)
    cos = jnp.concatenate([c32, z32, c32, z32], axis=0).T
    sin = jnp.concatenate([-s32, z32, s32, z32], axis=0).T

    def rope(r):
        return r * cos + pltpu.roll(r, LANES // 2, 1) * sin

    cq = _dot_nt(h, wcq_ref[...])
    ckv = _dot_nt(h, wckv_ref[...])
    g_ref[...] = _silu(_dot_nt(h, waz_ref[...])).astype(BF16)
    kr_if = _dot_nt(h, wkr_ref[...])
    if_ref[...] = kr_if
    kr = rope(kr_if).astype(BF16)

    cqn = _rms(cq, gq_ref[...]).astype(BF16)
    ckvn = _rms(ckv, gkv_ref[...]).astype(BF16)
    qn = _dot(cqn, wqn_ref[...]) * scale
    qr = _dot(cqn, wqr_ref[...]) * scale
    kn = _dot(ckvn, wkn_ref[...])
    for hd in range(A_HEADS):
        base = hd * A_QK_PAD
        ls = slice(hd * LANES, (hd + 1) * LANES)
        q_ref[:, base:base + LANES] = qn[:, ls].astype(BF16)
        q_ref[:, base + LANES:base + A_QK_PAD] = rope(qr[:, ls]).astype(BF16)
        k_ref[:, base:base + LANES] = kn[:, ls].astype(BF16)
        k_ref[:, base + LANES:base + A_QK_PAD] = kr
    vt_ref[0, 0] = _dot_nt(wvt_ref[...], ckvn).astype(BF16)


def _mla_proj(h, pos_row, invf, gq, gkv, wall, wqn, wqr, wkn, wvt, batch, seq):
    T = h.shape[0]
    tm = SEQ_TILE
    per_seq = seq // tm
    row = lambda w: pl.BlockSpec((tm, w), lambda i: (i, 0))
    small = (invf, gq, gkv)
    ups = (wqn, wqr, wkn, wvt)
    vt_spec = pl.BlockSpec((1, 1, A_W, tm), lambda i: (i // per_seq, i % per_seq, 0, 0))
    return pl.pallas_call(
        _mla_proj_kernel,
        grid=(T // tm,),
        in_specs=([row(D_MODEL), pl.BlockSpec((1, 1, tm), lambda i: (i, 0, 0))]
                  + [_const_spec(c.shape) for c in small]
                  + [_w_rows(W_CQ, Q_LORA), _w_rows(W_CKV, KV_LORA), _w_rows(W_KR, LANES),
                     _w_rows(W_AZ, A_W)]
                  + [_const_spec(c.shape) for c in ups]),
        out_specs=(row(A_HEADS * A_QK_PAD), row(A_HEADS * A_QK_PAD), vt_spec, row(A_W),
                   row(LANES)),
        out_shape=(jax.ShapeDtypeStruct((T, A_HEADS * A_QK_PAD), BF16),
                   jax.ShapeDtypeStruct((T, A_HEADS * A_QK_PAD), BF16),
                   jax.ShapeDtypeStruct((batch, per_seq, A_W, tm), BF16),
                   jax.ShapeDtypeStruct((T, A_W), BF16),
                   jax.ShapeDtypeStruct((T, LANES), F32)),
        compiler_params=_params("parallel"),
        name="mla_proj",
    )(h, pos_row, *small, wall, wall, wall, wall, *ups)


def _mla_attn_kernel(q_ref, k_ref, vt_ref, g_ref, o_ref):
    S = q_ref.shape[1]
    t = SEQ_TILE
    n_head = MLA_HEADS_PER_STEP
    hq = [slice(hh * A_QK_PAD, (hh + 1) * A_QK_PAD) for hh in range(n_head)]
    hv = [slice(hh * A_DV, (hh + 1) * A_DV) for hh in range(n_head)]

    for qi in range(S // t):
        qs = slice(qi * t, (qi + 1) * t)
        for hh in range(n_head):
            q = q_ref[0, qs, hq[hh]]
            pieces = _causal_pieces(qi, t, t // 2)
            st = []
            for k0, nk, q0, diag in pieces:
                s = _dot_nt(k_ref[0, k0:k0 + nk, hq[hh]], q[q0:, :])
                st.append(jnp.where(_keys_le_queries(nk, t - q0), s, -jnp.inf) if diag else s)
            m = functools.reduce(jnp.maximum, [
                _left_pad(jnp.max(s, axis=0, keepdims=True), q0, -jnp.inf)
                for s, (_, _, q0, _) in zip(st, pieces)])
            l = acc = None
            for s, (k0, nk, q0, _) in zip(st, pieces):
                p = jnp.exp2(s - m[:, q0:])
                l_blk = _left_pad(jnp.sum(p, axis=0, keepdims=True), q0, 0.0)
                kc, ko = divmod(k0, t)
                pv = _left_pad(_dot(vt_ref[0, kc, hv[hh], ko:ko + nk], p.astype(BF16)), q0, 0.0)
                l, acc = (l_blk, pv) if l is None else (l + l_blk, acc + pv)
            o_ref[0, qs, hv[hh]] = ((acc / l).T * g_ref[0, qs, hv[hh]].astype(F32)).astype(BF16)


def _mla_attn(q, k, vt, gate, batch, seq):
    n_tile = seq // SEQ_TILE
    n_h = MLA_HEADS_PER_STEP
    qk_blk = pl.BlockSpec((1, seq, n_h * A_QK_PAD), lambda b, p: (b, 0, p))
    v_blk = pl.BlockSpec((1, seq, n_h * A_DV), lambda b, p: (b, 0, p))
    vt_blk = pl.BlockSpec((1, n_tile, n_h * A_DV, SEQ_TILE), lambda b, p: (b, 0, p, 0))
    q3 = q.reshape(batch, seq, A_HEADS * A_QK_PAD)
    k3 = k.reshape(batch, seq, A_HEADS * A_QK_PAD)
    g3 = gate.reshape(batch, seq, A_W)
    return pl.pallas_call(
        _mla_attn_kernel,
        grid=(batch, A_HEADS // n_h),
        in_specs=[qk_blk, qk_blk, vt_blk, v_blk],
        out_specs=v_blk,
        out_shape=jax.ShapeDtypeStruct((batch, seq, A_W), BF16),
        compiler_params=_params("parallel", "parallel"),
        name="mla_attn",
    )(q3, k3, vt, g3).reshape(batch * seq, A_W)


def _cast_once(w_ref, wbf_ref):
    @pl.when(pl.program_id(0) == 0)
    def _():
        wbf_ref[...] = w_ref[...].astype(BF16)


def _mem_kv_kernel(mem_ref, g_ref, w_ref, kv_ref, wbf_ref):
    _cast_once(w_ref, wbf_ref)
    kv_ref[0] = _dot(_rms(mem_ref[0], g_ref[...]).astype(BF16), wbf_ref[...]).astype(BF16)


def _mem_kv(mem, gain, w):
    batch = mem.shape[0]
    return pl.pallas_call(
        _mem_kv_kernel,
        grid=(batch,),
        in_specs=[pl.BlockSpec((1, N_MEM, D_MODEL), lambda b: (b, 0, 0)),
                  _const_spec((1, D_MODEL)), _const_spec(w.shape)],
        out_specs=pl.BlockSpec((1, N_MEM, 2 * C_W), lambda b: (b, 0, 0)),
        out_shape=jax.ShapeDtypeStruct((batch, N_MEM, 2 * C_W), BF16),
        scratch_shapes=[pltpu.VMEM(w.shape, BF16)],
        compiler_params=_params("arbitrary"),
        name="mem_kv",
    )(mem, gain, w)


def _mem_attn_kernel(h_ref, kv_ref, wq_ref, wz_ref, o_ref):
    h = h_ref[...]
    cq = (_dot_nt(h, wq_ref[...].astype(BF16)) * (C_DH ** -0.5)).astype(BF16)
    for hd in range(C_HEADS):
        hs = slice(hd * C_DH, (hd + 1) * C_DH)
        s = _dot_nt(cq[:, hs], kv_ref[0, :, hs])
        cz = _dot_nt(h, wz_ref[hs, :].astype(BF16))
        p = jnp.exp(s - jnp.max(s, axis=-1, keepdims=True))
        l = jnp.sum(p, axis=-1, keepdims=True)
        o = _dot(p.astype(BF16), kv_ref[0, :, C_W + hd * C_DH:C_W + (hd + 1) * C_DH]) / l
        o_ref[:, hs] = (o * _silu(cz)).astype(BF16)


def _mem_attn(h, kv, wt, seq):
    T = h.shape[0]
    tm = SEQ_TILE
    per_seq = seq // tm
    return pl.pallas_call(
        _mem_attn_kernel,
        grid=(T // tm,),
        in_specs=[pl.BlockSpec((tm, D_MODEL), lambda i: (i, 0)),
                  pl.BlockSpec((1, N_MEM, 2 * C_W), lambda i: (i // per_seq, 0, 0)),
                  _w_in_rows(IN_OFFS[10], C_W), _w_in_rows(IN_OFFS[11], C_W)],
        out_specs=pl.BlockSpec((tm, C_W), lambda i: (i, 0)),
        out_shape=jax.ShapeDtypeStruct((T, C_W), BF16),
        compiler_params=_params("parallel"),
        name="mem_attn",
    )(h, kv, wt, wt)


MG_TM = 1024
MG_TN = 256


def _merge_kernel(h_ref, hm_ref, ha_ref, hc_ref, wgm_ref, wga_ref, wgc_ref,
                  wbm_ref, wba_ref, wbc_ref, o_ref):
    h = h_ref[...]
    gate = lambda w_ref: _sigmoid(_dot_nt(h, w_ref[...].astype(BF16)))
    branch = lambda a_ref, w_ref: _dot(a_ref[...], w_ref[...].astype(BF16))
    acc = gate(wgm_ref) * branch(hm_ref, wbm_ref)
    acc = acc + gate(wga_ref) * branch(ha_ref, wba_ref)
    acc = acc + gate(wgc_ref) * branch(hc_ref, wbc_ref)
    o_ref[...] = acc.astype(BF16)


def _merge(h, hm, ha, hc, wt, wbm, wba, wbc):
    T = h.shape[0]
    tm, tn = MG_TM, MG_TN
    row = lambda w: pl.BlockSpec((tm, w), lambda i, j: (i, 0))
    col = lambda kdim: pl.BlockSpec((kdim, tn), lambda i, j: (0, j))

    def gate(branch):
        first = IN_OFFS[12] + branch * D_MODEL
        return pl.BlockSpec((pl.Element(tn), pl.Element(D_MODEL)),
                            lambda i, j: (pl.multiple_of(first + j * tn, SUBLANES), 0))

    return pl.pallas_call(
        _merge_kernel,
        grid=(T // tm, D_MODEL // tn),
        in_specs=[row(D_MODEL), row(M_W), row(A_W), row(C_W),
                  gate(0), gate(1), gate(2), col(M_W), col(A_W), col(C_W)],
        out_specs=pl.BlockSpec((tm, tn), lambda i, j: (i, j)),
        out_shape=jax.ShapeDtypeStruct((T, D_MODEL), BF16),
        compiler_params=_params("parallel", "arbitrary"),
        name="merge",
    )(h, hm, ha, hc, wt, wt, wt, wbm, wba, wbc)


def _out_proj_kernel(final, x_ref, m_ref, w_ref, g_ref, o_ref, wbf_ref):
    _cast_once(w_ref, wbf_ref)
    y = x_ref[...] + _dot(m_ref[...], wbf_ref[...])
    o_ref[...] = _rms(y, g_ref[...]) if final else y


def _out_proj(x2, merged, w, gain, final):
    T = x2.shape[0]
    tm = SEQ_TILE
    row = pl.BlockSpec((tm, D_MODEL), lambda i: (i, 0))
    return pl.pallas_call(
        functools.partial(_out_proj_kernel, final),
        grid=(T // tm,),
        in_specs=[row, row, _const_spec(w.shape), _const_spec((1, D_MODEL))],
        out_specs=row,
        out_shape=jax.ShapeDtypeStruct((T, D_MODEL), F32),
        scratch_shapes=[pltpu.VMEM(w.shape, BF16)],
        compiler_params=_params("arbitrary"),
        name="out_proj",
    )(x2, merged, w, gain)


def _rope_lanes(a):
    half = A_ROPE // 2
    z = jnp.zeros(a.shape[:-1] + (half,), a.dtype)
    return jnp.concatenate([a[..., :half], z, a[..., half:], z], axis=-1)


PK_ROWS = 512
W_ROWS = W_KR + PK_ROWS
PK_SRC = ([b * PK_ROWS for b in range(IN_OFFS[4] // PK_ROWS)]
          + [IN_OFFS[9], IN_OFFS[9] + PK_ROWS, IN_OFFS[6], IN_OFFS[7], IN_OFFS[8]])


def _pack_w_kernel(w_ref, wif_ref, o_ref):
    b = pl.program_id(0)
    x = w_ref[...]
    o_ref[...] = x.astype(BF16)

    @pl.when(b == W_KR // PK_ROWS)
    def _():
        half = A_ROPE // 2
        n_if = 2 * M_HEADS
        zeros = lambda n: jnp.zeros((n, D_MODEL), F32)
        o_ref[...] = jnp.concatenate(
            [x[:half], wif_ref[...], zeros(half - n_if), x[half:2 * half], zeros(half),
             zeros(PK_ROWS - 4 * half)], axis=0).astype(BF16)


def _pack_w_in(wt):
    assert len(PK_SRC) * PK_ROWS == W_ROWS and all(s % SUBLANES == 0 for s in PK_SRC)

    def src(b):
        first = sum(jnp.where(b == i, s, 0) for i, s in enumerate(PK_SRC))
        return pl.multiple_of(first, SUBLANES), 0

    return pl.pallas_call(
        _pack_w_kernel,
        grid=(len(PK_SRC),),
        in_specs=[pl.BlockSpec((pl.Element(PK_ROWS), pl.Element(D_MODEL)), src),
                  pl.BlockSpec((pl.Element(2 * M_HEADS), pl.Element(D_MODEL)),
                               lambda b: (IN_OFFS[4], 0))],
        out_specs=pl.BlockSpec((PK_ROWS, D_MODEL), lambda b: (b, 0)),
        out_shape=jax.ShapeDtypeStruct((W_ROWS, D_MODEL), BF16),
        compiler_params=_params("parallel"),
        name="pack_w",
    )(wt, wt)


def _layer(x2, pos_row, kv_mem_in, l, final, batch, seq, w_in, b_igate, b_fgate, conv_w, conv_b,
           mh_norm, cq_norm, w_uq, ckv_norm, w_ukv, mem_norm, w_mem_kv, w_br_m, w_br_a, w_br_c,
           w_out, norm, final_norm):
    bf = lambda a: a.astype(BF16)
    row = lambda a: a.reshape(1, -1).astype(F32)

    wt = w_in[l].T
    wall = _pack_w_in(wt)
    uq = w_uq[l].reshape(Q_LORA, A_HEADS, A_DQK)
    wqn = bf(uq[:, :, :A_NOPE].reshape(Q_LORA, A_HEADS * A_NOPE))
    wqr = bf(_rope_lanes(uq[:, :, A_NOPE:]).reshape(Q_LORA, A_HEADS * LANES))
    ukv = w_ukv[l].reshape(KV_LORA, A_HEADS, A_NOPE + A_DV)
    wkn = bf(ukv[:, :, :A_NOPE].reshape(KV_LORA, A_HEADS * A_NOPE))
    wvvt = bf(ukv[:, :, A_NOPE:].reshape(KV_LORA, A_HEADS * A_DV).T)
    bias_row = jnp.pad(jnp.concatenate([b_igate[l], b_fgate[l]]).astype(F32),
                       (IF_LANE, LANES - IF_LANE - 2 * M_HEADS)).reshape(1, LANES)
    invf = (ROPE_THETA ** (-jnp.arange(0, A_ROPE, 2, dtype=F32) / A_ROPE)).reshape(-1, 1)

    lane_rep = lambda a: jnp.broadcast_to(a.astype(F32)[..., None], a.shape + (LANES,))
    h, mqt, mk, mvt, mgate = _mlstm_proj(x2, row(norm[l]), wall, lane_rep(conv_w[l]),
                                         lane_rep(conv_b[l]), batch, seq)
    aq, ak, avt, agate, pre_if = _mla_proj(h, pos_row, invf, row(cq_norm[l]), row(ckv_norm[l]),
                                           wall, wqn, wqr, wkn, wvvt, batch, seq)
    rowf, colf = _gate_scan(pre_if, bias_row, batch, seq)
    hm = _decay_attn(mqt, mk, mvt, mgate, rowf, colf, row(mh_norm[l]), batch, seq)
    ha = _mla_attn(aq, ak, avt, agate, batch, seq)

    kv_mem = _mem_kv(kv_mem_in, row(mem_norm[l]), w_mem_kv[l])
    hc = _mem_attn(h, kv_mem, wt, seq)

    merged = _merge(h, hm, ha, hc, wt, w_br_m[l], w_br_a[l], w_br_c[l])
    return _out_proj(x2, merged, w_out[l], row(final_norm), final)


def kernel(x, mem, positions, w_in, b_igate, b_fgate, conv_w, conv_b, mh_norm, cq_norm, w_uq,
           ckv_norm, w_ukv, mem_norm, w_mem_kv, w_br_m, w_br_a, w_br_c, w_out, norm, final_norm):
    batch, seq, d = x.shape
    depth = w_in.shape[0]
    assert d == D_MODEL and seq % MG_TM == 0 and w_in.shape[2] == sum(IN_SPLITS)
    x2 = x.reshape(batch * seq, d)
    pos_row = positions.astype(F32).reshape(batch * seq // SEQ_TILE, 1, SEQ_TILE)
    for l in range(depth):
        x2 = _layer(x2, pos_row, mem, l, l == depth - 1, batch, seq, w_in, b_igate, b_fgate,
                    conv_w, conv_b, mh_norm, cq_norm, w_uq, ckv_norm, w_ukv, mem_norm, w_mem_kv,
                    w_br_m, w_br_a, w_br_c, w_out, norm, final_norm)
    return x2.reshape(batch, seq, d)
```

```python
import functools
import math

import jax
import jax.numpy as jnp
from jax import lax
from jax.experimental import pallas as pl
from jax.experimental.pallas import tpu as pltpu

F32 = jnp.float32
BF16 = jnp.bfloat16

D_MODEL = 2048
M_HEADS, M_DH = 4, 256
M_W = M_HEADS * M_DH
CONV_K = 4
A_HEADS, A_NOPE, A_ROPE, A_DV = 8, 128, 64, 128
A_DQK = A_NOPE + A_ROPE
A_W = A_HEADS * A_DV
Q_LORA = KV_LORA = 512
ROPE_THETA = 10000.0
N_MEM = 256
C_HEADS, C_DH = 4, 256
C_W = C_HEADS * C_DH
EPS = 1e-6

LANES = 128
SUBLANES = 8
A_QK_PAD = 2 * LANES
VMEM_LIMIT = 56 * 1024 * 1024

SEQ_TILE = 512
HEADS_PER_STEP = 2
MLA_HEADS_PER_STEP = 4

IN_SPLITS = (2 * M_W, M_W, M_W, M_W, M_HEADS, M_HEADS, Q_LORA, KV_LORA, A_ROPE, A_W, C_W, C_W,
             3 * D_MODEL)


def _const_spec(shape):
    nd = len(shape)
    return pl.BlockSpec(shape, lambda *_: (0,) * nd, pipeline_mode=pl.Buffered(1))


IN_OFFS = [sum(IN_SPLITS[:i]) for i in range(len(IN_SPLITS) + 1)]

W_QK, W_V, W_O, W_Z = 0, 2048, 3072, 4096
W_AZ, W_CQ, W_CKV, W_KR = 5120, 6144, 6656, 7168
IF_LANE = A_ROPE // 2


def _w_rows(offset, rows):
    assert offset % rows == 0
    return pl.BlockSpec((rows, D_MODEL), lambda *_: (offset // rows, 0),
                        pipeline_mode=pl.Buffered(1))


def _w_in_rows(first, rows):
    assert first % SUBLANES == 0
    return pl.BlockSpec((pl.Element(rows), pl.Element(D_MODEL)), lambda *_: (first, 0),
                        pipeline_mode=pl.Buffered(1))


def _params(*sem):
    return pltpu.CompilerParams(dimension_semantics=sem, vmem_limit_bytes=VMEM_LIMIT)


def _dot(a, b):
    return jnp.dot(a, b, preferred_element_type=F32)


def _dot_nt(a, b):
    return lax.dot_general(a, b, (((1,), (1,)), ((), ())), preferred_element_type=F32)


def _rms(x, g):
    return x * lax.rsqrt(jnp.mean(x * x, axis=-1, keepdims=True) + EPS) * g


def _sigmoid(x):
    return 1.0 / (1.0 + jnp.exp(-x))


def _silu(x):
    return x * _sigmoid(x)


def _keys_le_queries(nk, nq):
    shape = (nk, nq)
    return lax.broadcasted_iota(jnp.int32, shape, 0) <= lax.broadcasted_iota(jnp.int32, shape, 1)


def _causal_pieces(qi, t, nk_full):
    half = t // 2
    return ([(k0, nk_full, 0, False) for k0 in range(0, qi * t, nk_full)]
            + [(qi * t, half, 0, True), (qi * t + half, half, half, True)])


def _two_phase(chains, first, second, ahead):
    pending = [first(*c) for c in chains[:ahead]]
    for n, chain in enumerate(chains):
        if n + ahead < len(chains):
            pending.append(first(*chains[n + ahead]))
        second(*chain, pending.pop(0))


def _left_pad(x, n, fill):
    if n == 0:
        return x
    return jnp.concatenate([jnp.full((x.shape[0], n), fill, x.dtype), x], axis=1)


MP_CH = 512


def _mlstm_proj_kernel(tiles_per_seq, x_ref, ng_ref, wqk_ref, cw_ref, cb_ref, wvt_ref, wo_ref,
                       wz_ref, h_ref, qt_ref, k_ref, vt_ref, g_ref, xbuf_ref):
    tm = x_ref.shape[0]

    @pl.when(pl.program_id(0) % tiles_per_seq == 0)
    def _():
        xbuf_ref[...] = jnp.zeros_like(xbuf_ref)

    h = _rms(x_ref[...], ng_ref[...]).astype(BF16)
    h_ref[...] = h
    widen = lambda a: jnp.concatenate([a] * (tm // LANES), axis=1)

    def conv_chunk(c):
        fs = slice(c * MP_CH, (c + 1) * MP_CH)
        acc = _dot_nt(wqk_ref[fs, :], h)
        prev = xbuf_ref[fs, :]
        lane = lax.broadcasted_iota(jnp.int32, prev.shape, 1)
        y = widen(cb_ref[fs, :]) + widen(cw_ref[CONV_K - 1, fs, :]) * acc
        for j in range(CONV_K - 1):
            back = CONV_K - 1 - j
            rolled = pltpu.roll(acc, back, 1)
            head = jnp.where(lane < back, pltpu.roll(prev, back, 1), rolled[:, :LANES])
            tap = jnp.concatenate([head, rolled[:, LANES:]], axis=1)
            y = y + widen(cw_ref[j, fs, :]) * tap
        xbuf_ref[fs, :] = acc[:, tm - LANES:]
        y = _silu(y)
        if c < M_W // MP_CH:
            qt_ref[0, 0, fs, :] = y.astype(BF16)
        else:
            ks = slice(c * MP_CH - M_W, (c + 1) * MP_CH - M_W)
            k_ref[:, ks] = (y.T * (M_DH ** -0.5)).astype(BF16)

    def value_chunk(c):
        cs = slice(c * MP_CH, (c + 1) * MP_CH)
        vt_ref[0, 0, cs, :] = _dot_nt(wvt_ref[cs, :], h).astype(BF16)

    def gate_chunk(c):
        cs = slice(c * MP_CH, (c + 1) * MP_CH)
        o = _dot_nt(h, wo_ref[cs, :])
        z = _dot_nt(h, wz_ref[cs, :])
        g_ref[:, cs] = (_sigmoid(o) * _silu(z)).astype(BF16)

    light = [functools.partial(f, c) for c in range(M_W // MP_CH) for f in (value_chunk, gate_chunk)]
    for c in range(2 * M_W // MP_CH):
        conv_chunk(c)
        if c < len(light):
            light[c]()
    for f in light[2 * M_W // MP_CH:]:
        f()


def _mlstm_proj(x2, norm_g, wall, conv_w, conv_b, batch, seq):
    T = x2.shape[0]
    tm = SEQ_TILE
    per_seq = seq // tm
    row = lambda w: pl.BlockSpec((tm, w), lambda i: (i, 0))
    out_shape = (
        jax.ShapeDtypeStruct((T, D_MODEL), BF16),
        jax.ShapeDtypeStruct((batch, per_seq, M_W, tm), BF16),
        jax.ShapeDtypeStruct((T, M_W), BF16),
        jax.ShapeDtypeStruct((batch, per_seq, M_W, tm), BF16),
        jax.ShapeDtypeStruct((T, M_W), BF16),
    )
    vt_spec = pl.BlockSpec((1, 1, M_W, tm), lambda i: (i // per_seq, i % per_seq, 0, 0))
    return pl.pallas_call(
        functools.partial(_mlstm_proj_kernel, per_seq),
        grid=(T // tm,),
        in_specs=[row(D_MODEL), _const_spec((1, D_MODEL)), _w_rows(W_QK, 2 * M_W),
                  _const_spec(conv_w.shape), _const_spec(conv_b.shape), _w_rows(W_V, M_W),
                  _w_rows(W_O, M_W), _w_rows(W_Z, M_W)],
        out_specs=(row(D_MODEL), vt_spec, row(M_W), vt_spec, row(M_W)),
        out_shape=out_shape,
        scratch_shapes=[pltpu.VMEM((2 * M_W, LANES), F32)],
        compiler_params=_params("arbitrary"),
        name="mlstm_proj",
    )(x2, norm_g, wall, conv_w, conv_b, wall, wall, wall)


def _lane_scan(x, op, fill):
    n = x.shape[-1]
    lane = lax.broadcasted_iota(jnp.int32, x.shape, x.ndim - 1)
    d = 1
    while d < n:
        shifted = pltpu.roll(x, d, x.ndim - 1)
        x = op(x, jnp.where(lane >= d, shifted, fill))
        d *= 2
    return x


def _gate_scan_kernel(if_ref, bias_ref, row_ref, col_ref):
    S = if_ref.shape[1]
    t = SEQ_TILE
    pre = if_ref[0] + bias_ref[...]
    t8 = pre.T[IF_LANE:IF_LANE + SUBLANES, :]
    lf = jnp.minimum(t8, 0.0) - jnp.log(1.0 + jnp.exp(-jnp.abs(t8)))
    b = _lane_scan(lf, jnp.add, 0.0)
    b = pltpu.roll(b, M_HEADS, 0)
    a = t8 - b
    mx = jnp.maximum(_lane_scan(a, jnp.maximum, -jnp.inf), 0.0)
    nb = -b - mx
    sub = lax.broadcasted_iota(jnp.int32, (SUBLANES, S), 0)
    zeros = jnp.zeros((LANES - SUBLANES, S), F32)
    n_h = HEADS_PER_STEP
    for p in range(M_HEADS // n_h):
        up = (SUBLANES - n_h * p) % SUBLANES
        m_grp = pltpu.roll(mx, up, 0) if up else mx
        nb_grp = pltpu.roll(nb, (up + n_h) % SUBLANES, 0)
        stack = jnp.where(sub < n_h, m_grp, nb_grp)
        for j in range(S // t):
            row_ref[0, p, j] = stack[:, j * t:(j + 1) * t]
        a_grp = pltpu.roll(a, up, 0) if up else a
        col_ref[0, p] = jnp.concatenate([a_grp, zeros], axis=0).T


def _gate_scan(pre_if, bias_row, batch, seq):
    n_grp = M_HEADS // HEADS_PER_STEP
    n_tile = seq // SEQ_TILE
    return pl.pallas_call(
        _gate_scan_kernel,
        grid=(batch,),
        in_specs=[pl.BlockSpec((1, seq, LANES), lambda b: (b, 0, 0)), _const_spec((1, LANES))],
        out_specs=(pl.BlockSpec((1, n_grp, n_tile, SUBLANES, SEQ_TILE), lambda b: (b, 0, 0, 0, 0)),
                   pl.BlockSpec((1, n_grp, seq, LANES), lambda b: (b, 0, 0, 0))),
        out_shape=(jax.ShapeDtypeStruct((batch, n_grp, n_tile, SUBLANES, SEQ_TILE), F32),
                   jax.ShapeDtypeStruct((batch, n_grp, seq, LANES), F32)),
        compiler_params=_params("parallel"),
        name="gate_scan",
    )(pre_if.reshape(batch, seq, LANES), bias_row)


def _decay_attn_kernel(qt_ref, k_ref, vt_ref, g_ref, row_ref, col_ref, gain_ref, o_ref,
                       arep_ref, fac_ref):
    S = k_ref.shape[1]
    t = SEQ_TILE
    n_head = HEADS_PER_STEP
    hs = [slice(hh * M_DH, (hh + 1) * M_DH) for hh in range(n_head)]
    lane_tiles = t // LANES
    widen = lambda a: jnp.concatenate([a] * lane_tiles, axis=1)

    c_rep = [[None] * (S // t) for _ in range(n_head)]
    for hh in range(n_head):
        arep_ref[hh] = jnp.broadcast_to(col_ref[0, 0, :, hh:hh + 1], (S, LANES))
        for kj in range(S // t - 1):
            a = arep_ref[hh, kj * t:(kj + 1) * t, :]
            c_rep[hh][kj] = jnp.max(a, axis=0, keepdims=True)
            fac_ref[hh, kj * t:(kj + 1) * t, :] = jnp.exp(a - c_rep[hh][kj])

    def scores(qi, hh):
        qt = qt_ref[0, qi, hs[hh], :]
        return [_dot(k_ref[0, k0:k0 + nk, hs[hh]], qt[:, q0:])
                for k0, nk, q0, _ in _causal_pieces(qi, t, t)]

    def weigh_pv(qi, hh, sts):
        qs = slice(qi * t, (qi + 1) * t)
        m_row = row_ref[0, 0, qi, hh:hh + 1, :]
        nb_row = row_ref[0, 0, qi, n_head + hh:n_head + hh + 1, :]
        den = num = None
        for st, (k0, nk, q0, diag) in zip(sts, _causal_pieces(qi, t, t)):
            ks = slice(k0, k0 + nk)
            kc, ko = divmod(k0, t)
            if diag:
                arg = widen(arep_ref[hh, ks, :])[:, q0:] - m_row[:, q0:]
                p = st * jnp.exp(jnp.where(_keys_le_queries(nk, t - q0), arg, -jnp.inf))
                d_blk = jnp.sum(p, axis=0, keepdims=True)
                n_blk = _dot(vt_ref[0, kc, hs[hh], ko:ko + nk], p.astype(BF16))
            else:
                p = st * widen(fac_ref[hh, ks, :])
                qfac = jnp.exp(widen(c_rep[hh][kc]) - m_row)
                d_blk = qfac * jnp.sum(p, axis=0, keepdims=True)
                n_blk = qfac * _dot(vt_ref[0, kc, hs[hh], ko:ko + nk], p.astype(BF16))
            d_blk, n_blk = _left_pad(d_blk, q0, 0.0), _left_pad(n_blk, q0, 0.0)
            den, num = (d_blk, n_blk) if den is None else (den + d_blk, num + n_blk)
        hv = num / jnp.maximum(jnp.abs(den), jnp.exp(nb_row))
        hv = hv * lax.rsqrt(jnp.mean(hv * hv, axis=0, keepdims=True) + EPS)
        o_ref[0, qs, hs[hh]] = (hv.T * gain_ref[:, hs[hh]]
                                * g_ref[0, qs, hs[hh]].astype(F32)).astype(BF16)

    chains = [(qi, hh) for qi in range(S // t) for hh in range(n_head)]
    _two_phase(chains, scores, weigh_pv, ahead=1)


def _decay_attn(qt, k, vt, gate, rowf, colf, gain, batch, seq):
    n_grp = M_HEADS // HEADS_PER_STEP
    n_tile = seq // SEQ_TILE
    w = HEADS_PER_STEP * M_DH
    blk = pl.BlockSpec((1, seq, w), lambda b, p: (b, 0, p))
    t_blk = pl.BlockSpec((1, n_tile, w, SEQ_TILE), lambda b, p: (b, 0, p, 0))
    k3, g3 = (a.reshape(batch, seq, M_W) for a in (k, gate))
    return pl.pallas_call(
        _decay_attn_kernel,
        grid=(batch, n_grp),
        in_specs=[t_blk, blk, t_blk, blk,
                  pl.BlockSpec((1, 1, n_tile, SUBLANES, SEQ_TILE), lambda b, p: (b, p, 0, 0, 0)),
                  pl.BlockSpec((1, 1, seq, LANES), lambda b, p: (b, p, 0, 0)),
                  pl.BlockSpec((1, w), lambda b, p: (0, p))],
        out_specs=blk,
        out_shape=jax.ShapeDtypeStruct((batch, seq, M_W), BF16),
        scratch_shapes=[pltpu.VMEM((HEADS_PER_STEP, seq, LANES), F32),
                        pltpu.VMEM((HEADS_PER_STEP, seq - SEQ_TILE, LANES), F32)],
        compiler_params=_params("parallel", "parallel"),
        name="decay_attn",
    )(qt, k3, vt, g3, rowf, colf, gain).reshape(batch * seq, M_W)


def _mla_proj_kernel(h_ref, pos_ref, invf_ref, gq_ref, gkv_ref, wcq_ref, wckv_ref,
                     wkr_ref, waz_ref, wqn_ref, wqr_ref, wkn_ref, wvt_ref,
                     q_ref, k_ref, vt_ref, g_ref, if_ref):
    h = h_ref[...]
    tm = h.shape[0]
    scale = math.log2(math.e) / math.sqrt(A_DQK)
    ang = invf_ref[...] * pos_ref[0]
    c32, s32 = jnp.cos(ang), jnp.sin(ang)
    z32 = jnp.zeros((A_ROPE // 2, tm), F32)
    cos = jnp.concatenate([c32, z32, c32, z32], axis=0).T
    sin = jnp.concatenate([-s32, z32, s32, z32], axis=0).T

    def rope(r):
        return r * cos + pltpu.roll(r, LANES // 2, 1) * sin

    cq = _dot_nt(h, wcq_ref[...])
    ckv = _dot_nt(h, wckv_ref[...])
    g_ref[...] = _silu(_dot_nt(h, waz_ref[...])).astype(BF16)
    kr_if = _dot_nt(h, wkr_ref[...])
    if_ref[...] = kr_if
    kr = rope(kr_if).astype(BF16)

    cqn = _rms(cq, gq_ref[...]).astype(BF16)
    ckvn = _rms(ckv, gkv_ref[...]).astype(BF16)
    qn = _dot(cqn, wqn_ref[...]) * scale
    qr = _dot(cqn, wqr_ref[...]) * scale
    kn = _dot(ckvn, wkn_ref[...])
    for hd in range(A_HEADS):
        base = hd * A_QK_PAD
        ls = slice(hd * LANES, (hd + 1) * LANES)
        q_ref[:, base:base + LANES] = qn[:, ls].astype(BF16)
        q_ref[:, base + LANES:base + A_QK_PAD] = rope(qr[:, ls]).astype(BF16)
        k_ref[:, base:base + LANES] = kn[:, ls].astype(BF16)
        k_ref[:, base + LANES:base + A_QK_PAD] = kr
    vt_ref[0, 0] = _dot_nt(wvt_ref[...], ckvn).astype(BF16)


def _mla_proj(h, pos_row, invf, gq, gkv, wall, wqn, wqr, wkn, wvt, batch, seq):
    T = h.shape[0]
    tm = SEQ_TILE
    per_seq = seq // tm
    row = lambda w: pl.BlockSpec((tm, w), lambda i: (i, 0))
    small = (invf, gq, gkv)
    ups = (wqn, wqr, wkn, wvt)
    vt_spec = pl.BlockSpec((1, 1, A_W, tm), lambda i: (i // per_seq, i % per_seq, 0, 0))
    return pl.pallas_call(
        _mla_proj_kernel,
        grid=(T // tm,),
        in_specs=([row(D_MODEL), pl.BlockSpec((1, 1, tm), lambda i: (i, 0, 0))]
                  + [_const_spec(c.shape) for c in small]
                  + [_w_rows(W_CQ, Q_LORA), _w_rows(W_CKV, KV_LORA), _w_rows(W_KR, LANES),
                     _w_rows(W_AZ, A_W)]
                  + [_const_spec(c.shape) for c in ups]),
        out_specs=(row(A_HEADS * A_QK_PAD), row(A_HEADS * A_QK_PAD), vt_spec, row(A_W),
                   row(LANES)),
        out_shape=(jax.ShapeDtypeStruct((T, A_HEADS * A_QK_PAD), BF16),
                   jax.ShapeDtypeStruct((T, A_HEADS * A_QK_PAD), BF16),
                   jax.ShapeDtypeStruct((batch, per_seq, A_W, tm), BF16),
                   jax.ShapeDtypeStruct((T, A_W), BF16),
                   jax.ShapeDtypeStruct((T, LANES), F32)),
        compiler_params=_params("parallel"),
        name="mla_proj",
    )(h, pos_row, *small, wall, wall, wall, wall, *ups)


def _mla_attn_kernel(q_ref, k_ref, vt_ref, g_ref, o_ref):
    S = q_ref.shape[1]
    t = SEQ_TILE
    n_head = MLA_HEADS_PER_STEP
    hq = [slice(hh * A_QK_PAD, (hh + 1) * A_QK_PAD) for hh in range(n_head)]
    hv = [slice(hh * A_DV, (hh + 1) * A_DV) for hh in range(n_head)]

    def scores(qi, hh):
        q = q_ref[0, qi * t:(qi + 1) * t, hq[hh]]
        st = []
        for k0, nk, q0, diag in _causal_pieces(qi, t, t // 2):
            s = _dot_nt(k_ref[0, k0:k0 + nk, hq[hh]], q[q0:, :])
            st.append(jnp.where(_keys_le_queries(nk, t - q0), s, -jnp.inf) if diag else s)
        return st

    def softmax_pv(qi, hh, st):
        qs = slice(qi * t, (qi + 1) * t)
        pieces = _causal_pieces(qi, t, t // 2)
        m = functools.reduce(jnp.maximum, [
            _left_pad(jnp.max(s, axis=0, keepdims=True), q0, -jnp.inf)
            for s, (_, _, q0, _) in zip(st, pieces)])
        l = acc = None
        for s, (k0, nk, q0, _) in zip(st, pieces):
            p = jnp.exp2(s - m[:, q0:])
            l_blk = _left_pad(jnp.sum(p, axis=0, keepdims=True), q0, 0.0)
            kc, ko = divmod(k0, t)
            pv = _left_pad(_dot(vt_ref[0, kc, hv[hh], ko:ko + nk], p.astype(BF16)), q0, 0.0)
            l, acc = (l_blk, pv) if l is None else (l + l_blk, acc + pv)
        o_ref[0, qs, hv[hh]] = ((acc / l).T * g_ref[0, qs, hv[hh]].astype(F32)).astype(BF16)

    chains = [(qi, hh) for qi in range(S // t) for hh in range(n_head)]
    _two_phase(chains, scores, softmax_pv, ahead=2)


def _mla_attn(q, k, vt, gate, batch, seq):
    n_tile = seq // SEQ_TILE
    n_h = MLA_HEADS_PER_STEP
    qk_blk = pl.BlockSpec((1, seq, n_h * A_QK_PAD), lambda b, p: (b, 0, p))
    v_blk = pl.BlockSpec((1, seq, n_h * A_DV), lambda b, p: (b, 0, p))
    vt_blk = pl.BlockSpec((1, n_tile, n_h * A_DV, SEQ_TILE), lambda b, p: (b, 0, p, 0))
    q3 = q.reshape(batch, seq, A_HEADS * A_QK_PAD)
    k3 = k.reshape(batch, seq, A_HEADS * A_QK_PAD)
    g3 = gate.reshape(batch, seq, A_W)
    return pl.pallas_call(
        _mla_attn_kernel,
        grid=(batch, A_HEADS // n_h),
        in_specs=[qk_blk, qk_blk, vt_blk, v_blk],
        out_specs=v_blk,
        out_shape=jax.ShapeDtypeStruct((batch, seq, A_W), BF16),
        compiler_params=_params("parallel", "parallel"),
        name="mla_attn",
    )(q3, k3, vt, g3).reshape(batch * seq, A_W)


def _cast_once(w_ref, wbf_ref):
    @pl.when(pl.program_id(0) == 0)
    def _():
        wbf_ref[...] = w_ref[...].astype(BF16)


def _mem_kv_kernel(mem_ref, g_ref, w_ref, kv_ref, wbf_ref):
    _cast_once(w_ref, wbf_ref)
    nb, nm, d = mem_ref.shape
    m = _rms(mem_ref[...].reshape(nb * nm, d), g_ref[...]).astype(BF16)
    kv_ref[...] = _dot(m, wbf_ref[...]).astype(BF16).reshape(nb, nm, 2 * C_W)


MKV_BATCH = 4


def _mem_kv(mem, gain, w):
    batch = mem.shape[0]
    nb = MKV_BATCH if batch % MKV_BATCH == 0 else 1
    return pl.pallas_call(
        _mem_kv_kernel,
        grid=(batch // nb,),
        in_specs=[pl.BlockSpec((nb, N_MEM, D_MODEL), lambda b: (b, 0, 0)),
                  _const_spec((1, D_MODEL)), _const_spec(w.shape)],
        out_specs=pl.BlockSpec((nb, N_MEM, 2 * C_W), lambda b: (b, 0, 0)),
        out_shape=jax.ShapeDtypeStruct((batch, N_MEM, 2 * C_W), BF16),
        scratch_shapes=[pltpu.VMEM(w.shape, BF16)],
        compiler_params=_params("arbitrary"),
        name="mem_kv",
    )(mem, gain, w)


def _mem_attn_kernel(h_ref, kv_ref, w_ref, o_ref, wbf_ref):
    _cast_once(w_ref, wbf_ref)
    h = h_ref[...]
    cq = (_dot_nt(h, wbf_ref[0:C_W, :]) * (C_DH ** -0.5)).astype(BF16)
    heads = [slice(hd * C_DH, (hd + 1) * C_DH) for hd in range(C_HEADS)]

    def matmuls(hs):
        gate_rows = slice(C_W + hs.start, C_W + hs.stop)
        return _dot_nt(cq[:, hs], kv_ref[0, :, hs]), _dot_nt(h, wbf_ref[gate_rows, :])

    s, cz = matmuls(heads[0])
    for hd, hs in enumerate(heads):
        nxt = matmuls(heads[hd + 1]) if hd + 1 < C_HEADS else None
        p = jnp.exp(s - jnp.max(s, axis=-1, keepdims=True))
        l = jnp.sum(p, axis=-1, keepdims=True)
        o = _dot(p.astype(BF16), kv_ref[0, :, C_W + hd * C_DH:C_W + (hd + 1) * C_DH]) / l
        o_ref[:, hs] = (o * _silu(cz)).astype(BF16)
        if nxt is not None:
            s, cz = nxt


MA_TM = 1024


def _mem_attn(h, kv, wt, seq):
    T = h.shape[0]
    tm = MA_TM
    per_seq = seq // tm
    return pl.pallas_call(
        _mem_attn_kernel,
        grid=(T // tm,),
        in_specs=[pl.BlockSpec((tm, D_MODEL), lambda i: (i, 0)),
                  pl.BlockSpec((1, N_MEM, 2 * C_W), lambda i: (i // per_seq, 0, 0)),
                  _w_in_rows(IN_OFFS[10], 2 * C_W)],
        out_specs=pl.BlockSpec((tm, C_W), lambda i: (i, 0)),
        out_shape=jax.ShapeDtypeStruct((T, C_W), BF16),
        scratch_shapes=[pltpu.VMEM((2 * C_W, D_MODEL), BF16)],
        compiler_params=_params("arbitrary"),
        name="mem_attn",
    )(h, kv, wt)


MG_TM = 1024
MG_TN = 256


def _merge_kernel(h_ref, hm_ref, ha_ref, hc_ref, wgm_ref, wga_ref, wgc_ref,
                  wbm_ref, wba_ref, wbc_ref, o_ref):
    h = h_ref[...]
    gate = lambda w_ref: _sigmoid(_dot_nt(h, w_ref[...].astype(BF16)))
    branch = lambda a_ref, w_ref: _dot(a_ref[...], w_ref[...].astype(BF16))
    acc = gate(wgm_ref) * branch(hm_ref, wbm_ref)
    acc = acc + gate(wga_ref) * branch(ha_ref, wba_ref)
    acc = acc + gate(wgc_ref) * branch(hc_ref, wbc_ref)
    o_ref[...] = acc.astype(BF16)


def _merge(h, hm, ha, hc, wt, wbm, wba, wbc):
    T = h.shape[0]
    tm, tn = MG_TM, MG_TN
    row = lambda w: pl.BlockSpec((tm, w), lambda i, j: (i, 0))
    col = lambda kdim: pl.BlockSpec((kdim, tn), lambda i, j: (0, j))

    def gate(branch):
        first = IN_OFFS[12] + branch * D_MODEL
        return pl.BlockSpec((pl.Element(tn), pl.Element(D_MODEL)),
                            lambda i, j: (pl.multiple_of(first + j * tn, SUBLANES), 0))

    return pl.pallas_call(
        _merge_kernel,
        grid=(T // tm, D_MODEL // tn),
        in_specs=[row(D_MODEL), row(M_W), row(A_W), row(C_W),
                  gate(0), gate(1), gate(2), col(M_W), col(A_W), col(C_W)],
        out_specs=pl.BlockSpec((tm, tn), lambda i, j: (i, j)),
        out_shape=jax.ShapeDtypeStruct((T, D_MODEL), BF16),
        compiler_params=_params("parallel", "arbitrary"),
        name="merge",
    )(h, hm, ha, hc, wt, wt, wt, wbm, wba, wbc)


def _out_proj_kernel(final, x_ref, m_ref, w_ref, g_ref, o_ref, wbf_ref):
    _cast_once(w_ref, wbf_ref)
    y = x_ref[...] + _dot(m_ref[...], wbf_ref[...])
    o_ref[...] = _rms(y, g_ref[...]) if final else y


def _out_proj(x2, merged, w, gain, final):
    T = x2.shape[0]
    tm = SEQ_TILE
    row = pl.BlockSpec((tm, D_MODEL), lambda i: (i, 0))
    return pl.pallas_call(
        functools.partial(_out_proj_kernel, final),
        grid=(T // tm,),
        in_specs=[row, row, _const_spec(w.shape), _const_spec((1, D_MODEL))],
        out_specs=row,
        out_shape=jax.ShapeDtypeStruct((T, D_MODEL), F32),
        scratch_shapes=[pltpu.VMEM(w.shape, BF16)],
        compiler_params=_params("arbitrary"),
        name="out_proj",
    )(x2, merged, w, gain)


def _rope_lanes(a):
    half = A_ROPE // 2
    z = jnp.zeros(a.shape[:-1] + (half,), a.dtype)
    return jnp.concatenate([a[..., :half], z, a[..., half:], z], axis=-1)


PK_ROWS = 1024
W_ROWS = W_KR + PK_ROWS
PK_SRC = ([b * PK_ROWS for b in range(IN_OFFS[4] // PK_ROWS)]
          + [IN_OFFS[9], IN_OFFS[6], IN_OFFS[8]])


def _pack_w_kernel(w_ref, wif_ref, o_ref):
    b = pl.program_id(0)
    x = w_ref[...]
    o_ref[...] = x.astype(BF16)

    @pl.when(b == W_KR // PK_ROWS)
    def _():
        half = A_ROPE // 2
        n_if = 2 * M_HEADS
        zeros = lambda n: jnp.zeros((n, D_MODEL), F32)
        o_ref[...] = jnp.concatenate(
            [x[:half], wif_ref[...], zeros(half - n_if), x[half:2 * half], zeros(half),
             zeros(PK_ROWS - 4 * half)], axis=0).astype(BF16)


def _pack_w_in(wt):
    assert len(PK_SRC) * PK_ROWS == W_ROWS and all(s % SUBLANES == 0 for s in PK_SRC)

    def src(b):
        first = sum(jnp.where(b == i, s, 0) for i, s in enumerate(PK_SRC))
        return pl.multiple_of(first, SUBLANES), 0

    return pl.pallas_call(
        _pack_w_kernel,
        grid=(len(PK_SRC),),
        in_specs=[pl.BlockSpec((pl.Element(PK_ROWS), pl.Element(D_MODEL)), src),
                  pl.BlockSpec((pl.Element(2 * M_HEADS), pl.Element(D_MODEL)),
                               lambda b: (IN_OFFS[4], 0))],
        out_specs=pl.BlockSpec((PK_ROWS, D_MODEL), lambda b: (b, 0)),
        out_shape=jax.ShapeDtypeStruct((W_ROWS, D_MODEL), BF16),
        compiler_params=_params("parallel"),
        name="pack_w",
    )(wt, wt)


def _layer(x2, pos_row, kv_mem_in, l, final, batch, seq, w_in, b_igate, b_fgate, conv_w, conv_b,
           mh_norm, cq_norm, w_uq, ckv_norm, w_ukv, mem_norm, w_mem_kv, w_br_m, w_br_a, w_br_c,
           w_out, norm, final_norm):
    bf = lambda a: a.astype(BF16)
    row = lambda a: a.reshape(1, -1).astype(F32)

    wt = w_in[l].T
    wall = _pack_w_in(wt)
    uq = w_uq[l].reshape(Q_LORA, A_HEADS, A_DQK)
    wqn = bf(uq[:, :, :A_NOPE].reshape(Q_LORA, A_HEADS * A_NOPE))
    wqr = bf(_rope_lanes(uq[:, :, A_NOPE:]).reshape(Q_LORA, A_HEADS * LANES))
    ukv = w_ukv[l].reshape(KV_LORA, A_HEADS, A_NOPE + A_DV)
    wkn = bf(ukv[:, :, :A_NOPE].reshape(KV_LORA, A_HEADS * A_NOPE))
    wvvt = bf(ukv[:, :, A_NOPE:].reshape(KV_LORA, A_HEADS * A_DV).T)
    bias_row = jnp.pad(jnp.concatenate([b_igate[l], b_fgate[l]]).astype(F32),
                       (IF_LANE, LANES - IF_LANE - 2 * M_HEADS)).reshape(1, LANES)
    invf = (ROPE_THETA ** (-jnp.arange(0, A_ROPE, 2, dtype=F32) / A_ROPE)).reshape(-1, 1)

    lane_rep = lambda a: jnp.broadcast_to(a.astype(F32)[..., None], a.shape + (LANES,))
    h, mqt, mk, mvt, mgate = _mlstm_proj(x2, row(norm[l]), wall, lane_rep(conv_w[l]),
                                         lane_rep(conv_b[l]), batch, seq)
    aq, ak, avt, agate, pre_if = _mla_proj(h, pos_row, invf, row(cq_norm[l]), row(ckv_norm[l]),
                                           wall, wqn, wqr, wkn, wvvt, batch, seq)
    rowf, colf = _gate_scan(pre_if, bias_row, batch, seq)
    hm = _decay_attn(mqt, mk, mvt, mgate, rowf, colf, row(mh_norm[l]), batch, seq)
    ha = _mla_attn(aq, ak, avt, agate, batch, seq)

    kv_mem = _mem_kv(kv_mem_in, row(mem_norm[l]), w_mem_kv[l])
    hc = _mem_attn(h, kv_mem, wt, seq)

    merged = _merge(h, hm, ha, hc, wt, w_br_m[l], w_br_a[l], w_br_c[l])
    return _out_proj(x2, merged, w_out[l], row(final_norm), final)


def kernel(x, mem, positions, w_in, b_igate, b_fgate, conv_w, conv_b, mh_norm, cq_norm, w_uq,
           ckv_norm, w_ukv, mem_norm, w_mem_kv, w_br_m, w_br_a, w_br_c, w_out, norm, final_norm):
    batch, seq, d = x.shape
    depth = w_in.shape[0]
    assert d == D_MODEL and seq % MG_TM == 0 and w_in.shape[2] == sum(IN_SPLITS)
    x2 = x.reshape(batch * seq, d)
    pos_row = positions.astype(F32).reshape(batch * seq // SEQ_TILE, 1, SEQ_TILE)
    for l in range(depth):
        x2 = _layer(x2, pos_row, mem, l, l == depth - 1, batch, seq, w_in, b_igate, b_fgate,
                    conv_w, conv_b, mh_norm, cq_norm, w_uq, ckv_norm, w_ukv, mem_norm, w_mem_kv,
                    w_br_m, w_br_a, w_br_c, w_out, norm, final_norm)
    return x2.reshape(batch, seq, d)
```

```python
import functools
import math

import jax
import jax.numpy as jnp
from jax import lax
from jax.experimental import pallas as pl
from jax.experimental.pallas import tpu as pltpu

F32 = jnp.float32
BF16 = jnp.bfloat16

D_MODEL = 2048
M_HEADS, M_DH = 4, 256
M_W = M_HEADS * M_DH
CONV_K = 4
A_HEADS, A_NOPE, A_ROPE, A_DV = 8, 128, 64, 128
A_DQK = A_NOPE + A_ROPE
A_W = A_HEADS * A_DV
Q_LORA = KV_LORA = 512
ROPE_THETA = 10000.0
N_MEM = 256
C_HEADS, C_DH = 4, 256
C_W = C_HEADS * C_DH
EPS = 1e-6

LANES = 128
SUBLANES = 8
A_QK_PAD = 2 * LANES
VMEM_LIMIT = 56 * 1024 * 1024

SEQ_TILE = 512
HEADS_PER_STEP = 2
MLA_HEADS_PER_STEP = 4

IN_SPLITS = (2 * M_W, M_W, M_W, M_W, M_HEADS, M_HEADS, Q_LORA, KV_LORA, A_ROPE, A_W, C_W, C_W,
             3 * D_MODEL)


def _const_spec(shape):
    nd = len(shape)
    return pl.BlockSpec(shape, lambda *_: (0,) * nd, pipeline_mode=pl.Buffered(1))


IN_OFFS = [sum(IN_SPLITS[:i]) for i in range(len(IN_SPLITS) + 1)]

W_QK, W_V, W_O, W_Z = 0, 2048, 3072, 4096
W_AZ, W_CQ, W_CKV, W_KR = 5120, 6144, 6656, 7168
IF_LANE = A_ROPE // 2


def _w_rows(offset, rows):
    assert offset % rows == 0
    return pl.BlockSpec((rows, D_MODEL), lambda *_: (offset // rows, 0),
                        pipeline_mode=pl.Buffered(1))


def _w_in_rows(first, rows):
    assert first % SUBLANES == 0
    return pl.BlockSpec((pl.Element(rows), pl.Element(D_MODEL)), lambda *_: (first, 0),
                        pipeline_mode=pl.Buffered(1))


def _params(*sem):
    return pltpu.CompilerParams(dimension_semantics=sem, vmem_limit_bytes=VMEM_LIMIT)


def _dot(a, b):
    return jnp.dot(a, b, preferred_element_type=F32)


def _dot_nt(a, b):
    return lax.dot_general(a, b, (((1,), (1,)), ((), ())), preferred_element_type=F32)


def _rms(x, g):
    return x * lax.rsqrt(jnp.mean(x * x, axis=-1, keepdims=True) + EPS) * g


def _sigmoid(x):
    return 1.0 / (1.0 + jnp.exp(-x))


def _silu(x):
    return x * _sigmoid(x)


def _keys_le_queries(nk, nq):
    shape = (nk, nq)
    return lax.broadcasted_iota(jnp.int32, shape, 0) <= lax.broadcasted_iota(jnp.int32, shape, 1)


def _causal_pieces(qi, t, nk_full):
    half = t // 2
    return ([(k0, nk_full, 0, False) for k0 in range(0, qi * t, nk_full)]
            + [(qi * t, half, 0, True), (qi * t + half, half, half, True)])


def _two_phase(chains, first, second, ahead):
    pending = [first(*c) for c in chains[:ahead]]
    for n, chain in enumerate(chains):
        if n + ahead < len(chains):
            pending.append(first(*chains[n + ahead]))
        second(*chain, pending.pop(0))


def _left_pad(x, n, fill):
    if n == 0:
        return x
    return jnp.concatenate([jnp.full((x.shape[0], n), fill, x.dtype), x], axis=1)


MP_CH = 512


def _mlstm_proj_kernel(tiles_per_seq, x_ref, ng_ref, wqk_ref, cw_ref, cb_ref, wvt_ref, wo_ref,
                       wz_ref, h_ref, qt_ref, k_ref, vt_ref, g_ref, xbuf_ref):
    tm = x_ref.shape[0]

    @pl.when(pl.program_id(0) % tiles_per_seq == 0)
    def _():
        xbuf_ref[...] = jnp.zeros_like(xbuf_ref)

    h = _rms(x_ref[...], ng_ref[...]).astype(BF16)
    h_ref[...] = h
    widen = lambda a: jnp.concatenate([a] * (tm // LANES), axis=1)

    def conv_chunk(c):
        fs = slice(c * MP_CH, (c + 1) * MP_CH)
        acc = _dot_nt(wqk_ref[fs, :], h)
        prev = xbuf_ref[fs, :]
        lane = lax.broadcasted_iota(jnp.int32, prev.shape, 1)
        y = widen(cb_ref[fs, :]) + widen(cw_ref[CONV_K - 1, fs, :]) * acc
        for j in range(CONV_K - 1):
            back = CONV_K - 1 - j
            rolled = pltpu.roll(acc, back, 1)
            head = jnp.where(lane < back, pltpu.roll(prev, back, 1), rolled[:, :LANES])
            tap = jnp.concatenate([head, rolled[:, LANES:]], axis=1)
            y = y + widen(cw_ref[j, fs, :]) * tap
        xbuf_ref[fs, :] = acc[:, tm - LANES:]
        y = _silu(y)
        if c < M_W // MP_CH:
            qt_ref[0, 0, fs, :] = y.astype(BF16)
        else:
            ks = slice(c * MP_CH - M_W, (c + 1) * MP_CH - M_W)
            k_ref[:, ks] = (y.T * (M_DH ** -0.5)).astype(BF16)

    def value_chunk(c):
        cs = slice(c * MP_CH, (c + 1) * MP_CH)
        vt_ref[0, 0, cs, :] = _dot_nt(wvt_ref[cs, :], h).astype(BF16)

    def gate_chunk(c):
        cs = slice(c * MP_CH, (c + 1) * MP_CH)
        o = _dot_nt(h, wo_ref[cs, :])
        z = _dot_nt(h, wz_ref[cs, :])
        g_ref[:, cs] = (_sigmoid(o) * _silu(z)).astype(BF16)

    light = [functools.partial(f, c) for c in range(M_W // MP_CH) for f in (value_chunk, gate_chunk)]
    for c in range(2 * M_W // MP_CH):
        conv_chunk(c)
        if c < len(light):
            light[c]()
    for f in light[2 * M_W // MP_CH:]:
        f()


def _mlstm_proj(x2, norm_g, wall, conv_w, conv_b, batch, seq):
    T = x2.shape[0]
    tm = SEQ_TILE
    per_seq = seq // tm
    row = lambda w: pl.BlockSpec((tm, w), lambda i: (i, 0))
    out_shape = (
        jax.ShapeDtypeStruct((T, D_MODEL), BF16),
        jax.ShapeDtypeStruct((batch, per_seq, M_W, tm), BF16),
        jax.ShapeDtypeStruct((T, M_W), BF16),
        jax.ShapeDtypeStruct((batch, per_seq, M_W, tm), BF16),
        jax.ShapeDtypeStruct((T, M_W), BF16),
    )
    vt_spec = pl.BlockSpec((1, 1, M_W, tm), lambda i: (i // per_seq, i % per_seq, 0, 0))
    return pl.pallas_call(
        functools.partial(_mlstm_proj_kernel, per_seq),
        grid=(T // tm,),
        in_specs=[row(D_MODEL), _const_spec((1, D_MODEL)), _w_rows(W_QK, 2 * M_W),
                  _const_spec(conv_w.shape), _const_spec(conv_b.shape), _w_rows(W_V, M_W),
                  _w_rows(W_O, M_W), _w_rows(W_Z, M_W)],
        out_specs=(row(D_MODEL), vt_spec, row(M_W), vt_spec, row(M_W)),
        out_shape=out_shape,
        scratch_shapes=[pltpu.VMEM((2 * M_W, LANES), F32)],
        compiler_params=_params("arbitrary"),
        name="mlstm_proj",
    )(x2, norm_g, wall, conv_w, conv_b, wall, wall, wall)


def _lane_scan(x, op, fill):
    n = x.shape[-1]
    lane = lax.broadcasted_iota(jnp.int32, x.shape, x.ndim - 1)
    d = 1
    while d < n:
        shifted = pltpu.roll(x, d, x.ndim - 1)
        x = op(x, jnp.where(lane >= d, shifted, fill))
        d *= 2
    return x


def _gate_scan_kernel(if_ref, bias_ref, row_ref, col_ref):
    n_seq, S = if_ref.shape[0], if_ref.shape[1]
    t = SEQ_TILE
    rows8 = lambda x, s: x[s * SUBLANES:(s + 1) * SUBLANES]
    t8 = jnp.concatenate([(if_ref[s] + bias_ref[...]).T[IF_LANE:IF_LANE + SUBLANES, :]
                          for s in range(n_seq)], axis=0)
    lf = jnp.minimum(t8, 0.0) - jnp.log(1.0 + jnp.exp(-jnp.abs(t8)))
    b_all = _lane_scan(lf, jnp.add, 0.0)
    b_all = jnp.concatenate([pltpu.roll(rows8(b_all, s), M_HEADS, 0) for s in range(n_seq)],
                            axis=0)
    a_all = t8 - b_all
    mx_all = jnp.maximum(_lane_scan(a_all, jnp.maximum, -jnp.inf), 0.0)
    nb_all = -b_all - mx_all
    sub = lax.broadcasted_iota(jnp.int32, (SUBLANES, S), 0)
    zeros = jnp.zeros((LANES - SUBLANES, S), F32)
    n_h = HEADS_PER_STEP
    for s in range(n_seq):
        a, mx, nb = rows8(a_all, s), rows8(mx_all, s), rows8(nb_all, s)
        for p in range(M_HEADS // n_h):
            up = (SUBLANES - n_h * p) % SUBLANES
            m_grp = pltpu.roll(mx, up, 0) if up else mx
            nb_grp = pltpu.roll(nb, (up + n_h) % SUBLANES, 0)
            stack = jnp.where(sub < n_h, m_grp, nb_grp)
            for j in range(S // t):
                row_ref[s, p, j] = stack[:, j * t:(j + 1) * t]
            a_grp = pltpu.roll(a, up, 0) if up else a
            col_ref[s, p] = jnp.concatenate([a_grp, zeros], axis=0).T


GS_SEQS = 2


def _gate_scan(pre_if, bias_row, batch, seq):
    n_grp = M_HEADS // HEADS_PER_STEP
    n_tile = seq // SEQ_TILE
    ns = GS_SEQS if batch % GS_SEQS == 0 else 1
    return pl.pallas_call(
        _gate_scan_kernel,
        grid=(batch // ns,),
        in_specs=[pl.BlockSpec((ns, seq, LANES), lambda b: (b, 0, 0)), _const_spec((1, LANES))],
        out_specs=(pl.BlockSpec((ns, n_grp, n_tile, SUBLANES, SEQ_TILE), lambda b: (b, 0, 0, 0, 0)),
                   pl.BlockSpec((ns, n_grp, seq, LANES), lambda b: (b, 0, 0, 0))),
        out_shape=(jax.ShapeDtypeStruct((batch, n_grp, n_tile, SUBLANES, SEQ_TILE), F32),
                   jax.ShapeDtypeStruct((batch, n_grp, seq, LANES), F32)),
        compiler_params=_params("parallel"),
        name="gate_scan",
    )(pre_if.reshape(batch, seq, LANES), bias_row)


def _decay_attn_kernel(qt_ref, k_ref, vt_ref, g_ref, row_ref, col_ref, gain_ref, o_ref,
                       arep_ref, fac_ref):
    S = k_ref.shape[1]
    t = SEQ_TILE
    n_head = HEADS_PER_STEP
    hs = [slice(hh * M_DH, (hh + 1) * M_DH) for hh in range(n_head)]
    lane_tiles = t // LANES
    widen = lambda a: jnp.concatenate([a] * lane_tiles, axis=1)

    c_rep = [[None] * (S // t) for _ in range(n_head)]
    for hh in range(n_head):
        arep_ref[hh] = jnp.broadcast_to(col_ref[0, 0, :, hh:hh + 1], (S, LANES))
        for kj in range(S // t - 1):
            a = arep_ref[hh, kj * t:(kj + 1) * t, :]
            c_rep[hh][kj] = jnp.max(a, axis=0, keepdims=True)
            fac_ref[hh, kj * t:(kj + 1) * t, :] = jnp.exp(a - c_rep[hh][kj])

    def scores(qi, hh):
        qt = qt_ref[0, qi, hs[hh], :]
        return [_dot(k_ref[0, k0:k0 + nk, hs[hh]], qt[:, q0:])
                for k0, nk, q0, _ in _causal_pieces(qi, t, t)]

    def weigh_pv(qi, hh, sts):
        qs = slice(qi * t, (qi + 1) * t)
        m_row = row_ref[0, 0, qi, hh:hh + 1, :]
        nb_row = row_ref[0, 0, qi, n_head + hh:n_head + hh + 1, :]
        den = num = None
        for st, (k0, nk, q0, diag) in zip(sts, _causal_pieces(qi, t, t)):
            ks = slice(k0, k0 + nk)
            kc, ko = divmod(k0, t)
            if diag:
                arg = widen(arep_ref[hh, ks, :])[:, q0:] - m_row[:, q0:]
                p = st * jnp.exp(jnp.where(_keys_le_queries(nk, t - q0), arg, -jnp.inf))
                d_blk = jnp.sum(p, axis=0, keepdims=True)
                n_blk = _dot(vt_ref[0, kc, hs[hh], ko:ko + nk], p.astype(BF16))
            else:
                p = st * widen(fac_ref[hh, ks, :])
                qfac = jnp.exp(widen(c_rep[hh][kc]) - m_row)
                d_blk = qfac * jnp.sum(p, axis=0, keepdims=True)
                n_blk = qfac * _dot(vt_ref[0, kc, hs[hh], ko:ko + nk], p.astype(BF16))
            d_blk, n_blk = _left_pad(d_blk, q0, 0.0), _left_pad(n_blk, q0, 0.0)
            den, num = (d_blk, n_blk) if den is None else (den + d_blk, num + n_blk)
        hv = num / jnp.maximum(jnp.abs(den), jnp.exp(nb_row))
        hv = hv * lax.rsqrt(jnp.mean(hv * hv, axis=0, keepdims=True) + EPS)
        o_ref[0, qs, hs[hh]] = (hv.T * gain_ref[:, hs[hh]]
                                * g_ref[0, qs, hs[hh]].astype(F32)).astype(BF16)

    chains = [(qi, hh) for qi in range(S // t) for hh in range(n_head)]
    _two_phase(chains, scores, weigh_pv, ahead=1)


def _decay_attn(qt, k, vt, gate, rowf, colf, gain, batch, seq):
    n_grp = M_HEADS // HEADS_PER_STEP
    n_tile = seq // SEQ_TILE
    w = HEADS_PER_STEP * M_DH
    blk = pl.BlockSpec((1, seq, w), lambda b, p: (b, 0, p))
    t_blk = pl.BlockSpec((1, n_tile, w, SEQ_TILE), lambda b, p: (b, 0, p, 0))
    k3, g3 = (a.reshape(batch, seq, M_W) for a in (k, gate))
    return pl.pallas_call(
        _decay_attn_kernel,
        grid=(batch, n_grp),
        in_specs=[t_blk, blk, t_blk, blk,
                  pl.BlockSpec((1, 1, n_tile, SUBLANES, SEQ_TILE), lambda b, p: (b, p, 0, 0, 0)),
                  pl.BlockSpec((1, 1, seq, LANES), lambda b, p: (b, p, 0, 0)),
                  pl.BlockSpec((1, w), lambda b, p: (0, p))],
        out_specs=blk,
        out_shape=jax.ShapeDtypeStruct((batch, seq, M_W), BF16),
        scratch_shapes=[pltpu.VMEM((HEADS_PER_STEP, seq, LANES), F32),
                        pltpu.VMEM((HEADS_PER_STEP, seq - SEQ_TILE, LANES), F32)],
        compiler_params=_params("parallel", "parallel"),
        name="decay_attn",
    )(qt, k3, vt, g3, rowf, colf, gain).reshape(batch * seq, M_W)


def _mla_proj_kernel(h_ref, pos_ref, invf_ref, gq_ref, gkv_ref, wcq_ref, wckv_ref,
                     wkr_ref, waz_ref, wqn_ref, wqr_ref, wkn_ref, wvt_ref,
                     q_ref, k_ref, vt_ref, g_ref, if_ref):
    h = h_ref[...]
    tm = h.shape[0]
    scale = math.log2(math.e) / math.sqrt(A_DQK)
    ang = invf_ref[...] * pos_ref[0]
    c32, s32 = jnp.cos(ang), jnp.sin(ang)
    z32 = jnp.zeros((A_ROPE // 2, tm), F32)
    cos = jnp.concatenate([c32, z32, c32, z32], axis=0).T
    sin = jnp.concatenate([-s32, z32, s32, z32], axis=0).T

    def rope(r):
        return r * cos + pltpu.roll(r, LANES // 2, 1) * sin

    cq = _dot_nt(h, wcq_ref[...])
    ckv = _dot_nt(h, wckv_ref[...])
    g_ref[...] = _silu(_dot_nt(h, waz_ref[...])).astype(BF16)
    kr_if = _dot_nt(h, wkr_ref[...])
    if_ref[...] = kr_if
    kr = rope(kr_if).astype(BF16)

    cqn = _rms(cq, gq_ref[...]).astype(BF16)
    ckvn = _rms(ckv, gkv_ref[...]).astype(BF16)
    qn = _dot(cqn, wqn_ref[...]) * scale
    qr = _dot(cqn, wqr_ref[...]) * scale
    kn = _dot(ckvn, wkn_ref[...])
    for hd in range(A_HEADS):
        base = hd * A_QK_PAD
        ls = slice(hd * LANES, (hd + 1) * LANES)
        q_ref[:, base:base + LANES] = qn[:, ls].astype(BF16)
        q_ref[:, base + LANES:base + A_QK_PAD] = rope(qr[:, ls]).astype(BF16)
        k_ref[:, base:base + LANES] = kn[:, ls].astype(BF16)
        k_ref[:, base + LANES:base + A_QK_PAD] = kr
    vt_ref[0, 0] = _dot_nt(wvt_ref[...], ckvn).astype(BF16)


def _mla_proj(h, pos_row, invf, gq, gkv, wall, wqn, wqr, wkn, wvt, batch, seq):
    T = h.shape[0]
    tm = SEQ_TILE
    per_seq = seq // tm
    row = lambda w: pl.BlockSpec((tm, w), lambda i: (i, 0))
    small = (invf, gq, gkv)
    ups = (wqn, wqr, wkn, wvt)
    vt_spec = pl.BlockSpec((1, 1, A_W, tm), lambda i: (i // per_seq, i % per_seq, 0, 0))
    return pl.pallas_call(
        _mla_proj_kernel,
        grid=(T // tm,),
        in_specs=([row(D_MODEL), pl.BlockSpec((1, 1, tm), lambda i: (i, 0, 0))]
                  + [_const_spec(c.shape) for c in small]
                  + [_w_rows(W_CQ, Q_LORA), _w_rows(W_CKV, KV_LORA), _w_rows(W_KR, LANES),
                     _w_rows(W_AZ, A_W)]
                  + [_const_spec(c.shape) for c in ups]),
        out_specs=(row(A_HEADS * A_QK_PAD), row(A_HEADS * A_QK_PAD), vt_spec, row(A_W),
                   row(LANES)),
        out_shape=(jax.ShapeDtypeStruct((T, A_HEADS * A_QK_PAD), BF16),
                   jax.ShapeDtypeStruct((T, A_HEADS * A_QK_PAD), BF16),
                   jax.ShapeDtypeStruct((batch, per_seq, A_W, tm), BF16),
                   jax.ShapeDtypeStruct((T, A_W), BF16),
                   jax.ShapeDtypeStruct((T, LANES), F32)),
        compiler_params=_params("parallel"),
        name="mla_proj",
    )(h, pos_row, *small, wall, wall, wall, wall, *ups)


def _mla_attn_kernel(q_ref, k_ref, vt_ref, g_ref, o_ref):
    S = q_ref.shape[1]
    t = SEQ_TILE
    n_head = MLA_HEADS_PER_STEP
    hq = [slice(hh * A_QK_PAD, (hh + 1) * A_QK_PAD) for hh in range(n_head)]
    hv = [slice(hh * A_DV, (hh + 1) * A_DV) for hh in range(n_head)]

    def scores(qi, hh):
        q = q_ref[0, qi * t:(qi + 1) * t, hq[hh]]
        st = []
        for k0, nk, q0, diag in _causal_pieces(qi, t, t // 2):
            s = _dot_nt(k_ref[0, k0:k0 + nk, hq[hh]], q[q0:, :])
            st.append(jnp.where(_keys_le_queries(nk, t - q0), s, -jnp.inf) if diag else s)
        return st

    def softmax_pv(qi, hh, st):
        qs = slice(qi * t, (qi + 1) * t)
        pieces = _causal_pieces(qi, t, t // 2)
        m = functools.reduce(jnp.maximum, [
            _left_pad(jnp.max(s, axis=0, keepdims=True), q0, -jnp.inf)
            for s, (_, _, q0, _) in zip(st, pieces)])
        l = acc = None
        for s, (k0, nk, q0, _) in zip(st, pieces):
            p = jnp.exp2(s - m[:, q0:])
            l_blk = _left_pad(jnp.sum(p, axis=0, keepdims=True), q0, 0.0)
            kc, ko = divmod(k0, t)
            pv = _left_pad(_dot(vt_ref[0, kc, hv[hh], ko:ko + nk], p.astype(BF16)), q0, 0.0)
            l, acc = (l_blk, pv) if l is None else (l + l_blk, acc + pv)
        o_ref[0, qs, hv[hh]] = ((acc / l).T * g_ref[0, qs, hv[hh]].astype(F32)).astype(BF16)

    chains = [(qi, hh) for qi in range(S // t) for hh in range(n_head)]
    _two_phase(chains, scores, softmax_pv, ahead=2)


def _mla_attn(q, k, vt, gate, batch, seq):
    n_tile = seq // SEQ_TILE
    n_h = MLA_HEADS_PER_STEP
    qk_blk = pl.BlockSpec((1, seq, n_h * A_QK_PAD), lambda b, p: (b, 0, p))
    v_blk = pl.BlockSpec((1, seq, n_h * A_DV), lambda b, p: (b, 0, p))
    vt_blk = pl.BlockSpec((1, n_tile, n_h * A_DV, SEQ_TILE), lambda b, p: (b, 0, p, 0))
    q3 = q.reshape(batch, seq, A_HEADS * A_QK_PAD)
    k3 = k.reshape(batch, seq, A_HEADS * A_QK_PAD)
    g3 = gate.reshape(batch, seq, A_W)
    return pl.pallas_call(
        _mla_attn_kernel,
        grid=(batch, A_HEADS // n_h),
        in_specs=[qk_blk, qk_blk, vt_blk, v_blk],
        out_specs=v_blk,
        out_shape=jax.ShapeDtypeStruct((batch, seq, A_W), BF16),
        compiler_params=_params("parallel", "parallel"),
        name="mla_attn",
    )(q3, k3, vt, g3).reshape(batch * seq, A_W)


def _cast_once(w_ref, wbf_ref):
    @pl.when(pl.program_id(0) == 0)
    def _():
        wbf_ref[...] = w_ref[...].astype(BF16)


def _mem_kv_kernel(mem_ref, g_ref, w_ref, kv_ref, wbf_ref):
    _cast_once(w_ref, wbf_ref)
    nb, nm, d = mem_ref.shape
    m = _rms(mem_ref[...].reshape(nb * nm, d), g_ref[...]).astype(BF16)
    kv_ref[...] = _dot(m, wbf_ref[...]).astype(BF16).reshape(nb, nm, 2 * C_W)


MKV_BATCH = 4


def _mem_kv(mem, gain, w):
    batch = mem.shape[0]
    nb = MKV_BATCH if batch % MKV_BATCH == 0 else 1
    return pl.pallas_call(
        _mem_kv_kernel,
        grid=(batch // nb,),
        in_specs=[pl.BlockSpec((nb, N_MEM, D_MODEL), lambda b: (b, 0, 0)),
                  _const_spec((1, D_MODEL)), _const_spec(w.shape)],
        out_specs=pl.BlockSpec((nb, N_MEM, 2 * C_W), lambda b: (b, 0, 0)),
        out_shape=jax.ShapeDtypeStruct((batch, N_MEM, 2 * C_W), BF16),
        scratch_shapes=[pltpu.VMEM(w.shape, BF16)],
        compiler_params=_params("arbitrary"),
        name="mem_kv",
    )(mem, gain, w)


def _mem_attn_kernel(h_ref, kv_ref, w_ref, o_ref, wbf_ref):
    _cast_once(w_ref, wbf_ref)
    h = h_ref[...]
    cq = (_dot_nt(h, wbf_ref[0:C_W, :]) * (C_DH ** -0.5)).astype(BF16)
    heads = [slice(hd * C_DH, (hd + 1) * C_DH) for hd in range(C_HEADS)]

    def matmuls(hs):
        gate_rows = slice(C_W + hs.start, C_W + hs.stop)
        return _dot_nt(cq[:, hs], kv_ref[0, :, hs]), _dot_nt(h, wbf_ref[gate_rows, :])

    s, cz = matmuls(heads[0])
    for hd, hs in enumerate(heads):
        nxt = matmuls(heads[hd + 1]) if hd + 1 < C_HEADS else None
        p = jnp.exp(s - jnp.max(s, axis=-1, keepdims=True))
        l = jnp.sum(p, axis=-1, keepdims=True)
        o = _dot(p.astype(BF16), kv_ref[0, :, C_W + hd * C_DH:C_W + (hd + 1) * C_DH]) / l
        o_ref[:, hs] = (o * _silu(cz)).astype(BF16)
        if nxt is not None:
            s, cz = nxt


MA_TM = 1024


def _mem_attn(h, kv, wt, seq):
    T = h.shape[0]
    tm = MA_TM
    per_seq = seq // tm
    return pl.pallas_call(
        _mem_attn_kernel,
        grid=(T // tm,),
        in_specs=[pl.BlockSpec((tm, D_MODEL), lambda i: (i, 0)),
                  pl.BlockSpec((1, N_MEM, 2 * C_W), lambda i: (i // per_seq, 0, 0)),
                  _w_in_rows(IN_OFFS[10], 2 * C_W)],
        out_specs=pl.BlockSpec((tm, C_W), lambda i: (i, 0)),
        out_shape=jax.ShapeDtypeStruct((T, C_W), BF16),
        scratch_shapes=[pltpu.VMEM((2 * C_W, D_MODEL), BF16)],
        compiler_params=_params("arbitrary"),
        name="mem_attn",
    )(h, kv, wt)


MG_TM = 1024
MG_TN = 512


def _merge_kernel(h_ref, hm_ref, ha_ref, hc_ref, wgm_ref, wga_ref, wgc_ref,
                  wbm_ref, wba_ref, wbc_ref, o_ref):
    h = h_ref[...]
    gate = lambda w_ref: _sigmoid(_dot_nt(h, w_ref[...]))
    branch = lambda a_ref, w_ref: _dot(a_ref[...], w_ref[...])
    acc = gate(wgm_ref) * branch(hm_ref, wbm_ref)
    acc = acc + gate(wga_ref) * branch(ha_ref, wba_ref)
    acc = acc + gate(wgc_ref) * branch(hc_ref, wbc_ref)
    o_ref[...] = acc.astype(BF16)


def _merge(h, hm, ha, hc, wt, wbm, wba, wbc):
    T = h.shape[0]
    tm, tn = MG_TM, MG_TN
    row = lambda w: pl.BlockSpec((tm, w), lambda i, j: (i, 0))
    col = lambda kdim: pl.BlockSpec((kdim, tn), lambda i, j: (0, j))

    def gate(branch):
        first = (W_GATE + branch * D_MODEL) // tn
        return pl.BlockSpec((tn, D_MODEL), lambda i, j: (first + j, 0))

    return pl.pallas_call(
        _merge_kernel,
        grid=(T // tm, D_MODEL // tn),
        in_specs=[row(D_MODEL), row(M_W), row(A_W), row(C_W),
                  gate(0), gate(1), gate(2), col(M_W), col(A_W), col(C_W)],
        out_specs=pl.BlockSpec((tm, tn), lambda i, j: (i, j)),
        out_shape=jax.ShapeDtypeStruct((T, D_MODEL), BF16),
        compiler_params=_params("parallel", "arbitrary"),
        name="merge",
    )(h, hm, ha, hc, wt, wt, wt, wbm, wba, wbc)


def _out_proj_kernel(final, x_ref, m_ref, w_ref, g_ref, o_ref, wbf_ref):
    _cast_once(w_ref, wbf_ref)
    y = x_ref[...] + _dot(m_ref[...], wbf_ref[...])
    o_ref[...] = _rms(y, g_ref[...]) if final else y


def _out_proj(x2, merged, w, gain, final):
    T = x2.shape[0]
    tm = SEQ_TILE
    row = pl.BlockSpec((tm, D_MODEL), lambda i: (i, 0))
    return pl.pallas_call(
        functools.partial(_out_proj_kernel, final),
        grid=(T // tm,),
        in_specs=[row, row, _const_spec(w.shape), _const_spec((1, D_MODEL))],
        out_specs=row,
        out_shape=jax.ShapeDtypeStruct((T, D_MODEL), F32),
        scratch_shapes=[pltpu.VMEM(w.shape, BF16)],
        compiler_params=_params("arbitrary"),
        name="out_proj",
    )(x2, merged, w, gain)


def _rope_lanes(a):
    half = A_ROPE // 2
    z = jnp.zeros(a.shape[:-1] + (half,), a.dtype)
    return jnp.concatenate([a[..., :half], z, a[..., half:], z], axis=-1)


PK_ROWS = 1024
W_GATE = W_KR + PK_ROWS
W_ROWS = W_GATE + 3 * D_MODEL
PK_SRC = ([b * PK_ROWS for b in range(IN_OFFS[4] // PK_ROWS)]
          + [IN_OFFS[9], IN_OFFS[6], IN_OFFS[8]]
          + [IN_OFFS[12] + b * PK_ROWS for b in range(3 * D_MODEL // PK_ROWS)])


def _pack_w_kernel(w_ref, wif_ref, o_ref):
    b = pl.program_id(0)
    x = w_ref[...]
    o_ref[...] = x.astype(BF16)

    @pl.when(b == W_KR // PK_ROWS)
    def _():
        half = A_ROPE // 2
        n_if = 2 * M_HEADS
        zeros = lambda n: jnp.zeros((n, D_MODEL), F32)
        o_ref[...] = jnp.concatenate(
            [x[:half], wif_ref[...], zeros(half - n_if), x[half:2 * half], zeros(half),
             zeros(PK_ROWS - 4 * half)], axis=0).astype(BF16)


def _pack_w_in(wt):
    assert len(PK_SRC) * PK_ROWS == W_ROWS and all(s % SUBLANES == 0 for s in PK_SRC)

    def src(b):
        first = sum(jnp.where(b == i, s, 0) for i, s in enumerate(PK_SRC))
        return pl.multiple_of(first, SUBLANES), 0

    return pl.pallas_call(
        _pack_w_kernel,
        grid=(len(PK_SRC),),
        in_specs=[pl.BlockSpec((pl.Element(PK_ROWS), pl.Element(D_MODEL)), src),
                  pl.BlockSpec((pl.Element(2 * M_HEADS), pl.Element(D_MODEL)),
                               lambda b: (IN_OFFS[4], 0))],
        out_specs=pl.BlockSpec((PK_ROWS, D_MODEL), lambda b: (b, 0)),
        out_shape=jax.ShapeDtypeStruct((W_ROWS, D_MODEL), BF16),
        compiler_params=_params("parallel"),
        name="pack_w",
    )(wt, wt)


def _layer(x2, pos_row, kv_mem_in, l, final, batch, seq, w_in, b_igate, b_fgate, conv_w, conv_b,
           mh_norm, cq_norm, w_uq, ckv_norm, w_ukv, mem_norm, w_mem_kv, w_br_m, w_br_a, w_br_c,
           w_out, norm, final_norm):
    bf = lambda a: a.astype(BF16)
    row = lambda a: a.reshape(1, -1).astype(F32)

    wt = w_in[l].T
    wall = _pack_w_in(wt)
    uq = w_uq[l].reshape(Q_LORA, A_HEADS, A_DQK)
    wqn = bf(uq[:, :, :A_NOPE].reshape(Q_LORA, A_HEADS * A_NOPE))
    wqr = bf(_rope_lanes(uq[:, :, A_NOPE:]).reshape(Q_LORA, A_HEADS * LANES))
    ukv = w_ukv[l].reshape(KV_LORA, A_HEADS, A_NOPE + A_DV)
    wkn = bf(ukv[:, :, :A_NOPE].reshape(KV_LORA, A_HEADS * A_NOPE))
    wvvt = bf(ukv[:, :, A_NOPE:].reshape(KV_LORA, A_HEADS * A_DV).T)
    bias_row = jnp.pad(jnp.concatenate([b_igate[l], b_fgate[l]]).astype(F32),
                       (IF_LANE, LANES - IF_LANE - 2 * M_HEADS)).reshape(1, LANES)
    invf = (ROPE_THETA ** (-jnp.arange(0, A_ROPE, 2, dtype=F32) / A_ROPE)).reshape(-1, 1)

    lane_rep = lambda a: jnp.broadcast_to(a.astype(F32)[..., None], a.shape + (LANES,))
    h, mqt, mk, mvt, mgate = _mlstm_proj(x2, row(norm[l]), wall, lane_rep(conv_w[l]),
                                         lane_rep(conv_b[l]), batch, seq)
    aq, ak, avt, agate, pre_if = _mla_proj(h, pos_row, invf, row(cq_norm[l]), row(ckv_norm[l]),
                                           wall, wqn, wqr, wkn, wvvt, batch, seq)
    rowf, colf = _gate_scan(pre_if, bias_row, batch, seq)
    hm = _decay_attn(mqt, mk, mvt, mgate, rowf, colf, row(mh_norm[l]), batch, seq)
    ha = _mla_attn(aq, ak, avt, agate, batch, seq)

    kv_mem = _mem_kv(kv_mem_in, row(mem_norm[l]), w_mem_kv[l])
    hc = _mem_attn(h, kv_mem, wt, seq)

    merged = _merge(h, hm, ha, hc, wall, bf(w_br_m[l]), bf(w_br_a[l]), bf(w_br_c[l]))
    return _out_proj(x2, merged, w_out[l], row(final_norm), final)


def kernel(x, mem, positions, w_in, b_igate, b_fgate, conv_w, conv_b, mh_norm, cq_norm, w_uq,
           ckv_norm, w_ukv, mem_norm, w_mem_kv, w_br_m, w_br_a, w_br_c, w_out, norm, final_norm):
    batch, seq, d = x.shape
    depth = w_in.shape[0]
    assert d == D_MODEL and seq % MG_TM == 0 and w_in.shape[2] == sum(IN_SPLITS)
    x2 = x.reshape(batch * seq, d)
    pos_row = positions.astype(F32).reshape(batch * seq // SEQ_TILE, 1, SEQ_TILE)
    for l in range(depth):
        x2 = _layer(x2, pos_row, mem, l, l == depth - 1, batch, seq, w_in, b_igate, b_fgate,
                    conv_w, conv_b, mh_norm, cq_norm, w_uq, ckv_norm, w_ukv, mem_norm, w_mem_kv,
                    w_br_m, w_br_a, w_br_c, w_out, norm, final_norm)
    return x2.reshape(batch, seq, d)
```

```python
import functools
import math

import jax
import jax.numpy as jnp
from jax import lax
from jax.experimental import pallas as pl
from jax.experimental.pallas import tpu as pltpu

F32 = jnp.float32
BF16 = jnp.bfloat16

D_MODEL = 2048
M_HEADS, M_DH = 4, 256
M_W = M_HEADS * M_DH
CONV_K = 4
A_HEADS, A_NOPE, A_ROPE, A_DV = 8, 128, 64, 128
A_DQK = A_NOPE + A_ROPE
A_W = A_HEADS * A_DV
Q_LORA = KV_LORA = 512
ROPE_THETA = 10000.0
N_MEM = 256
C_HEADS, C_DH = 4, 256
C_W = C_HEADS * C_DH
EPS = 1e-6

LANES = 128
SUBLANES = 8
A_QK_PAD = 2 * LANES
VMEM_LIMIT = 56 * 1024 * 1024

SEQ_TILE = 512
HEADS_PER_STEP = 2
MLA_HEADS_PER_STEP = 4

IN_SPLITS = (2 * M_W, M_W, M_W, M_W, M_HEADS, M_HEADS, Q_LORA, KV_LORA, A_ROPE, A_W, C_W, C_W,
             3 * D_MODEL)


def _const_spec(shape):
    nd = len(shape)
    return pl.BlockSpec(shape, lambda *_: (0,) * nd, pipeline_mode=pl.Buffered(1))


IN_OFFS = [sum(IN_SPLITS[:i]) for i in range(len(IN_SPLITS) + 1)]

W_QK, W_V, W_O, W_Z, W_AZ = 0, 2 * M_W, 3 * M_W, 4 * M_W, 5 * M_W
W_CQ = W_AZ + A_W
W_CKV = W_CQ + Q_LORA
W_KR = W_CKV + KV_LORA
IF_LANE = A_ROPE // 2


def _w_rows(offset, rows):
    assert offset % rows == 0
    return pl.BlockSpec((rows, D_MODEL), lambda *_: (offset // rows, 0),
                        pipeline_mode=pl.Buffered(1))


def _w_in_rows(first, rows):
    assert first % SUBLANES == 0
    return pl.BlockSpec((pl.Element(rows), pl.Element(D_MODEL)), lambda *_: (first, 0),
                        pipeline_mode=pl.Buffered(1))


def _params(*sem):
    return pltpu.CompilerParams(dimension_semantics=sem, vmem_limit_bytes=VMEM_LIMIT)


def _dot(a, b):
    return jnp.dot(a, b, preferred_element_type=F32)


def _dot_nt(a, b):
    return lax.dot_general(a, b, (((1,), (1,)), ((), ())), preferred_element_type=F32)


def _rms(x, g):
    return x * lax.rsqrt(jnp.mean(x * x, axis=-1, keepdims=True) + EPS) * g


def _sigmoid(x):
    return 1.0 / (1.0 + jnp.exp(-x))


def _silu(x):
    return x * _sigmoid(x)


def _keys_le_queries(nk, nq):
    shape = (nk, nq)
    return lax.broadcasted_iota(jnp.int32, shape, 0) <= lax.broadcasted_iota(jnp.int32, shape, 1)


def _causal_pieces(qi, t, nk_full):
    half = t // 2
    return ([(k0, nk_full, 0, False) for k0 in range(0, qi * t, nk_full)]
            + [(qi * t, half, 0, True), (qi * t + half, half, half, True)])


def _two_phase(chains, first, second, ahead):
    pending = [first(*c) for c in chains[:ahead]]
    for n, chain in enumerate(chains):
        if n + ahead < len(chains):
            pending.append(first(*chains[n + ahead]))
        second(*chain, pending.pop(0))


def _left_pad(x, n, fill):
    if n == 0:
        return x
    return jnp.concatenate([jnp.full((x.shape[0], n), fill, x.dtype), x], axis=1)


MP_CH = 1024


def _mlstm_proj_kernel(tiles_per_seq, x_ref, ng_ref, wqk_ref, cw_ref, cb_ref, wvt_ref, wo_ref,
                       wz_ref, h_ref, qt_ref, k_ref, vt_ref, g_ref, xbuf_ref):
    tm = x_ref.shape[0]

    @pl.when(pl.program_id(0) % tiles_per_seq == 0)
    def _():
        xbuf_ref[...] = jnp.zeros_like(xbuf_ref)

    h = _rms(x_ref[...], ng_ref[...]).astype(BF16)
    h_ref[...] = h
    widen = lambda a: jnp.concatenate([a] * (tm // LANES), axis=1)

    def conv_chunk(c):
        fs = slice(c * MP_CH, (c + 1) * MP_CH)
        acc = _dot_nt(wqk_ref[fs, :], h)
        prev = xbuf_ref[fs, :]
        lane = lax.broadcasted_iota(jnp.int32, prev.shape, 1)
        y = widen(cb_ref[fs, :]) + widen(cw_ref[CONV_K - 1, fs, :]) * acc
        for j in range(CONV_K - 1):
            back = CONV_K - 1 - j
            rolled = pltpu.roll(acc, back, 1)
            head = jnp.where(lane < back, pltpu.roll(prev, back, 1), rolled[:, :LANES])
            tap = jnp.concatenate([head, rolled[:, LANES:]], axis=1)
            y = y + widen(cw_ref[j, fs, :]) * tap
        xbuf_ref[fs, :] = acc[:, tm - LANES:]
        y = _silu(y)
        if c < M_W // MP_CH:
            qt_ref[0, 0, fs, :] = y.astype(BF16)
        else:
            ks = slice(c * MP_CH - M_W, (c + 1) * MP_CH - M_W)
            k_ref[:, ks] = (y.T * (M_DH ** -0.5)).astype(BF16)

    def value_chunk(c):
        cs = slice(c * MP_CH, (c + 1) * MP_CH)
        vt_ref[0, 0, cs, :] = _dot_nt(wvt_ref[cs, :], h).astype(BF16)

    def gate_chunk(c):
        cs = slice(c * MP_CH, (c + 1) * MP_CH)
        o = _dot_nt(h, wo_ref[cs, :])
        z = _dot_nt(h, wz_ref[cs, :])
        g_ref[:, cs] = (_sigmoid(o) * _silu(z)).astype(BF16)

    light = [functools.partial(f, c) for c in range(M_W // MP_CH) for f in (value_chunk, gate_chunk)]
    for c in range(2 * M_W // MP_CH):
        conv_chunk(c)
        if c < len(light):
            light[c]()
    for f in light[2 * M_W // MP_CH:]:
        f()


def _mlstm_proj(x2, norm_g, wall, conv_w, conv_b, batch, seq):
    T = x2.shape[0]
    tm = SEQ_TILE
    per_seq = seq // tm
    row = lambda w: pl.BlockSpec((tm, w), lambda i: (i, 0))
    out_shape = (
        jax.ShapeDtypeStruct((T, D_MODEL), BF16),
        jax.ShapeDtypeStruct((batch, per_seq, M_W, tm), BF16),
        jax.ShapeDtypeStruct((T, M_W), BF16),
        jax.ShapeDtypeStruct((batch, per_seq, M_W, tm), BF16),
        jax.ShapeDtypeStruct((T, M_W), BF16),
    )
    vt_spec = pl.BlockSpec((1, 1, M_W, tm), lambda i: (i // per_seq, i % per_seq, 0, 0))
    return pl.pallas_call(
        functools.partial(_mlstm_proj_kernel, per_seq),
        grid=(T // tm,),
        in_specs=[row(D_MODEL), _const_spec((1, D_MODEL)), _w_rows(W_QK, 2 * M_W),
                  _const_spec(conv_w.shape), _const_spec(conv_b.shape), _w_rows(W_V, M_W),
                  _w_rows(W_O, M_W), _w_rows(W_Z, M_W)],
        out_specs=(row(D_MODEL), vt_spec, row(M_W), vt_spec, row(M_W)),
        out_shape=out_shape,
        scratch_shapes=[pltpu.VMEM((2 * M_W, LANES), F32)],
        compiler_params=_params("arbitrary"),
        name="mlstm_proj",
    )(x2, norm_g, wall, conv_w, conv_b, wall, wall, wall)


def _lane_scan(x, op, fill):
    n = x.shape[-1]
    lane = lax.broadcasted_iota(jnp.int32, x.shape, x.ndim - 1)
    d = 1
    while d < n:
        shifted = pltpu.roll(x, d, x.ndim - 1)
        x = op(x, jnp.where(lane >= d, shifted, fill))
        d *= 2
    return x


def _gate_scan_kernel(if_ref, bias_ref, row_ref, col_ref):
    n_seq, S = if_ref.shape[0], if_ref.shape[1]
    t = SEQ_TILE
    rows8 = lambda x, s: x[s * SUBLANES:(s + 1) * SUBLANES]
    t8 = jnp.concatenate([(if_ref[s] + bias_ref[...]).T[IF_LANE:IF_LANE + SUBLANES, :]
                          for s in range(n_seq)], axis=0)
    lf = jnp.minimum(t8, 0.0) - jnp.log(1.0 + jnp.exp(-jnp.abs(t8)))
    b_all = _lane_scan(lf, jnp.add, 0.0)
    b_all = jnp.concatenate([pltpu.roll(rows8(b_all, s), M_HEADS, 0) for s in range(n_seq)],
                            axis=0)
    a_all = t8 - b_all
    mx_all = jnp.maximum(_lane_scan(a_all, jnp.maximum, -jnp.inf), 0.0)
    nb_all = -b_all - mx_all
    sub = lax.broadcasted_iota(jnp.int32, (SUBLANES, S), 0)
    zeros = jnp.zeros((LANES - SUBLANES, S), F32)
    n_h = HEADS_PER_STEP
    for s in range(n_seq):
        a, mx, nb = rows8(a_all, s), rows8(mx_all, s), rows8(nb_all, s)
        for p in range(M_HEADS // n_h):
            up = (SUBLANES - n_h * p) % SUBLANES
            m_grp = pltpu.roll(mx, up, 0) if up else mx
            nb_grp = pltpu.roll(nb, (up + n_h) % SUBLANES, 0)
            stack = jnp.where(sub < n_h, m_grp, nb_grp)
            for j in range(S // t):
                row_ref[s, p, j] = stack[:, j * t:(j + 1) * t]
            a_grp = pltpu.roll(a, up, 0) if up else a
            col_ref[s, p] = jnp.concatenate([a_grp, zeros], axis=0).T


GS_SEQS = 2


def _gate_scan(pre_if, bias_row, batch, seq):
    n_grp = M_HEADS // HEADS_PER_STEP
    n_tile = seq // SEQ_TILE
    ns = GS_SEQS if batch % GS_SEQS == 0 else 1
    return pl.pallas_call(
        _gate_scan_kernel,
        grid=(batch // ns,),
        in_specs=[pl.BlockSpec((ns, seq, LANES), lambda b: (b, 0, 0)), _const_spec((1, LANES))],
        out_specs=(pl.BlockSpec((ns, n_grp, n_tile, SUBLANES, SEQ_TILE), lambda b: (b, 0, 0, 0, 0)),
                   pl.BlockSpec((ns, n_grp, seq, LANES), lambda b: (b, 0, 0, 0))),
        out_shape=(jax.ShapeDtypeStruct((batch, n_grp, n_tile, SUBLANES, SEQ_TILE), F32),
                   jax.ShapeDtypeStruct((batch, n_grp, seq, LANES), F32)),
        compiler_params=_params("parallel"),
        name="gate_scan",
    )(pre_if.reshape(batch, seq, LANES), bias_row)


def _decay_attn_kernel(qt_ref, k_ref, vt_ref, g_ref, row_ref, col_ref, gain_ref, o_ref,
                       arep_ref, fac_ref):
    S = k_ref.shape[1]
    t = SEQ_TILE
    n_head = HEADS_PER_STEP
    hs = [slice(hh * M_DH, (hh + 1) * M_DH) for hh in range(n_head)]
    lane_tiles = t // LANES
    widen = lambda a: jnp.concatenate([a] * lane_tiles, axis=1)

    c_rep = [[None] * (S // t) for _ in range(n_head)]
    for hh in range(n_head):
        arep_ref[hh] = jnp.broadcast_to(col_ref[0, 0, :, hh:hh + 1], (S, LANES))
        for kj in range(S // t - 1):
            a = arep_ref[hh, kj * t:(kj + 1) * t, :]
            c_rep[hh][kj] = jnp.max(a, axis=0, keepdims=True)
            fac_ref[hh, kj * t:(kj + 1) * t, :] = jnp.exp(a - c_rep[hh][kj])

    def scores(qi, hh):
        qt = qt_ref[0, qi, hs[hh], :]
        return [_dot(k_ref[0, k0:k0 + nk, hs[hh]], qt[:, q0:])
                for k0, nk, q0, _ in _causal_pieces(qi, t, t)]

    def weigh_pv(qi, hh, sts):
        qs = slice(qi * t, (qi + 1) * t)
        m_row = row_ref[0, 0, qi, hh:hh + 1, :]
        nb_row = row_ref[0, 0, qi, n_head + hh:n_head + hh + 1, :]
        den = num = None
        for st, (k0, nk, q0, diag) in zip(sts, _causal_pieces(qi, t, t)):
            ks = slice(k0, k0 + nk)
            kc, ko = divmod(k0, t)
            if diag:
                arg = widen(arep_ref[hh, ks, :])[:, q0:] - m_row[:, q0:]
                p = st * jnp.exp(jnp.where(_keys_le_queries(nk, t - q0), arg, -jnp.inf))
                d_blk = jnp.sum(p, axis=0, keepdims=True)
                n_blk = _dot(vt_ref[0, kc, hs[hh], ko:ko + nk], p.astype(BF16))
            else:
                p = st * widen(fac_ref[hh, ks, :])
                qfac = jnp.exp(widen(c_rep[hh][kc]) - m_row)
                d_blk = qfac * jnp.sum(p, axis=0, keepdims=True)
                n_blk = qfac * _dot(vt_ref[0, kc, hs[hh], ko:ko + nk], p.astype(BF16))
            d_blk, n_blk = _left_pad(d_blk, q0, 0.0), _left_pad(n_blk, q0, 0.0)
            den, num = (d_blk, n_blk) if den is None else (den + d_blk, num + n_blk)
        hv = num / jnp.maximum(jnp.abs(den), jnp.exp(nb_row))
        hv = hv * lax.rsqrt(jnp.mean(hv * hv, axis=0, keepdims=True) + EPS)
        o_ref[0, qs, hs[hh]] = (hv.T * gain_ref[:, hs[hh]]
                                * g_ref[0, qs, hs[hh]].astype(F32)).astype(BF16)

    chains = [(qi, hh) for qi in range(S // t) for hh in range(n_head)]
    _two_phase(chains, scores, weigh_pv, ahead=1)


def _decay_attn(qt, k, vt, gate, rowf, colf, gain, batch, seq):
    n_grp = M_HEADS // HEADS_PER_STEP
    n_tile = seq // SEQ_TILE
    w = HEADS_PER_STEP * M_DH
    blk = pl.BlockSpec((1, seq, w), lambda b, p: (b, 0, p))
    t_blk = pl.BlockSpec((1, n_tile, w, SEQ_TILE), lambda b, p: (b, 0, p, 0))
    k3, g3 = (a.reshape(batch, seq, M_W) for a in (k, gate))
    return pl.pallas_call(
        _decay_attn_kernel,
        grid=(batch, n_grp),
        in_specs=[t_blk, blk, t_blk, blk,
                  pl.BlockSpec((1, 1, n_tile, SUBLANES, SEQ_TILE), lambda b, p: (b, p, 0, 0, 0)),
                  pl.BlockSpec((1, 1, seq, LANES), lambda b, p: (b, p, 0, 0)),
                  pl.BlockSpec((1, w), lambda b, p: (0, p))],
        out_specs=blk,
        out_shape=jax.ShapeDtypeStruct((batch, seq, M_W), BF16),
        scratch_shapes=[pltpu.VMEM((HEADS_PER_STEP, seq, LANES), F32),
                        pltpu.VMEM((HEADS_PER_STEP, seq - SEQ_TILE, LANES), F32)],
        compiler_params=_params("parallel", "parallel"),
        name="decay_attn",
    )(qt, k3, vt, g3, rowf, colf, gain).reshape(batch * seq, M_W)


def _mla_proj_kernel(h_ref, pos_ref, invf_ref, gq_ref, gkv_ref, wcq_ref, wckv_ref,
                     wkr_ref, waz_ref, wqn_ref, wqr_ref, wkn_ref, wvt_ref,
                     q_ref, k_ref, vt_ref, g_ref, if_ref):
    h = h_ref[...]
    tm = h.shape[0]
    scale = math.log2(math.e) / math.sqrt(A_DQK)
    ang = invf_ref[...] * pos_ref[0]
    c32, s32 = jnp.cos(ang), jnp.sin(ang)
    z32 = jnp.zeros((A_ROPE // 2, tm), F32)
    cos = jnp.concatenate([c32, z32, c32, z32], axis=0).T
    sin = jnp.concatenate([-s32, z32, s32, z32], axis=0).T
    cos_pair = jnp.concatenate([c32, c32, c32, c32], axis=0).T
    sin_pair = jnp.concatenate([-s32, -s32, s32, s32], axis=0).T

    def rope(r, cos=cos, sin=sin):
        return r * cos + pltpu.roll(r, LANES // 2, 1) * sin

    cq = _dot_nt(h, wcq_ref[...])
    ckv = _dot_nt(h, wckv_ref[...])
    g_ref[...] = _silu(_dot_nt(h, waz_ref[...])).astype(BF16)
    kr_if = _dot_nt(h, wkr_ref[...])
    if_ref[...] = kr_if
    kr = rope(kr_if)
    kr_by_parity = (kr.astype(BF16), pltpu.roll(kr, A_ROPE // 2, 1).astype(BF16))

    cqn = _rms(cq, gq_ref[...]).astype(BF16)
    ckvn = _rms(ckv, gkv_ref[...]).astype(BF16)
    qn = _dot(cqn, wqn_ref[...]) * scale
    qr = _dot(cqn, wqr_ref[...]) * scale
    kn = _dot(ckvn, wkn_ref[...])
    for hd in range(A_HEADS):
        base = hd * A_QK_PAD
        ls = slice(hd * LANES, (hd + 1) * LANES)
        pair = slice((hd // 2) * LANES, (hd // 2 + 1) * LANES)
        q_ref[:, base:base + LANES] = qn[:, ls].astype(BF16)
        q_ref[:, base + LANES:base + A_QK_PAD] = rope(qr[:, pair], cos_pair, sin_pair).astype(BF16)
        k_ref[:, base:base + LANES] = kn[:, ls].astype(BF16)
        k_ref[:, base + LANES:base + A_QK_PAD] = kr_by_parity[hd % 2]
    vt_ref[0, 0] = _dot_nt(wvt_ref[...], ckvn).astype(BF16)


def _mla_proj(h, pos_row, invf, gq, gkv, wall, wqn, wqr, wkn, wvt, batch, seq):
    T = h.shape[0]
    tm = SEQ_TILE
    per_seq = seq // tm
    row = lambda w: pl.BlockSpec((tm, w), lambda i: (i, 0))
    small = (invf, gq, gkv)
    ups = (wqn, wqr, wkn, wvt)
    vt_spec = pl.BlockSpec((1, 1, A_W, tm), lambda i: (i // per_seq, i % per_seq, 0, 0))
    return pl.pallas_call(
        _mla_proj_kernel,
        grid=(T // tm,),
        in_specs=([row(D_MODEL), pl.BlockSpec((1, 1, tm), lambda i: (i, 0, 0))]
                  + [_const_spec(c.shape) for c in small]
                  + [_w_rows(W_CQ, Q_LORA), _w_rows(W_CKV, KV_LORA), _w_rows(W_KR, LANES),
                     _w_rows(W_AZ, A_W)]
                  + [_const_spec(c.shape) for c in ups]),
        out_specs=(row(A_HEADS * A_QK_PAD), row(A_HEADS * A_QK_PAD), vt_spec, row(A_W),
                   row(LANES)),
        out_shape=(jax.ShapeDtypeStruct((T, A_HEADS * A_QK_PAD), BF16),
                   jax.ShapeDtypeStruct((T, A_HEADS * A_QK_PAD), BF16),
                   jax.ShapeDtypeStruct((batch, per_seq, A_W, tm), BF16),
                   jax.ShapeDtypeStruct((T, A_W), BF16),
                   jax.ShapeDtypeStruct((T, LANES), F32)),
        compiler_params=_params("parallel"),
        name="mla_proj",
    )(h, pos_row, *small, wall, wall, wall, wall, *ups)


def _mla_attn_kernel(q_ref, k_ref, vt_ref, g_ref, o_ref):
    S = q_ref.shape[1]
    t = SEQ_TILE
    n_head = MLA_HEADS_PER_STEP
    hq = [slice(hh * A_QK_PAD, (hh + 1) * A_QK_PAD) for hh in range(n_head)]
    hv = [slice(hh * A_DV, (hh + 1) * A_DV) for hh in range(n_head)]

    def scores(qi, hh):
        q = q_ref[0, qi * t:(qi + 1) * t, hq[hh]]
        st = []
        for k0, nk, q0, diag in _causal_pieces(qi, t, t):
            s = _dot_nt(k_ref[0, k0:k0 + nk, hq[hh]], q[q0:, :])
            st.append(jnp.where(_keys_le_queries(nk, t - q0), s, -jnp.inf) if diag else s)
        return st

    def softmax_pv(qi, hh, st):
        qs = slice(qi * t, (qi + 1) * t)
        pieces = _causal_pieces(qi, t, t)
        m = functools.reduce(jnp.maximum, [
            _left_pad(jnp.max(s, axis=0, keepdims=True), q0, -jnp.inf)
            for s, (_, _, q0, _) in zip(st, pieces)])
        l = acc = None
        for s, (k0, nk, q0, _) in zip(st, pieces):
            p = jnp.exp2(s - m[:, q0:])
            l_blk = _left_pad(jnp.sum(p, axis=0, keepdims=True), q0, 0.0)
            kc, ko = divmod(k0, t)
            pv = _left_pad(_dot(vt_ref[0, kc, hv[hh], ko:ko + nk], p.astype(BF16)), q0, 0.0)
            l, acc = (l_blk, pv) if l is None else (l + l_blk, acc + pv)
        o_ref[0, qs, hv[hh]] = ((acc / l).T * g_ref[0, qs, hv[hh]].astype(F32)).astype(BF16)

    chains = [(qi, hh) for qi in range(S // t) for hh in range(n_head)]
    _two_phase(chains, scores, softmax_pv, ahead=2)


def _mla_attn(q, k, vt, gate, batch, seq):
    n_tile = seq // SEQ_TILE
    n_h = MLA_HEADS_PER_STEP
    qk_blk = pl.BlockSpec((1, seq, n_h * A_QK_PAD), lambda b, p: (b, 0, p))
    v_blk = pl.BlockSpec((1, seq, n_h * A_DV), lambda b, p: (b, 0, p))
    vt_blk = pl.BlockSpec((1, n_tile, n_h * A_DV, SEQ_TILE), lambda b, p: (b, 0, p, 0))
    q3 = q.reshape(batch, seq, A_HEADS * A_QK_PAD)
    k3 = k.reshape(batch, seq, A_HEADS * A_QK_PAD)
    g3 = gate.reshape(batch, seq, A_W)
    return pl.pallas_call(
        _mla_attn_kernel,
        grid=(batch, A_HEADS // n_h),
        in_specs=[qk_blk, qk_blk, vt_blk, v_blk],
        out_specs=v_blk,
        out_shape=jax.ShapeDtypeStruct((batch, seq, A_W), BF16),
        compiler_params=_params("parallel", "parallel"),
        name="mla_attn",
    )(q3, k3, vt, g3).reshape(batch * seq, A_W)


def _cast_once(w_ref, wbf_ref):
    @pl.when(pl.program_id(0) == 0)
    def _():
        wbf_ref[...] = w_ref[...].astype(BF16)


def _mem_kv_kernel(mem_ref, g_ref, w_ref, kv_ref, wbf_ref):
    _cast_once(w_ref, wbf_ref)
    nb, nm, d = mem_ref.shape
    m = _rms(mem_ref[...].reshape(nb * nm, d), g_ref[...]).astype(BF16)
    kv_ref[...] = _dot(m, wbf_ref[...]).astype(BF16).reshape(nb, nm, 2 * C_W)


MKV_BATCH = 4


def _mem_kv(mem, gain, w):
    batch = mem.shape[0]
    nb = MKV_BATCH if batch % MKV_BATCH == 0 else 1
    return pl.pallas_call(
        _mem_kv_kernel,
        grid=(batch // nb,),
        in_specs=[pl.BlockSpec((nb, N_MEM, D_MODEL), lambda b: (b, 0, 0)),
                  _const_spec((1, D_MODEL)), _const_spec(w.shape)],
        out_specs=pl.BlockSpec((nb, N_MEM, 2 * C_W), lambda b: (b, 0, 0)),
        out_shape=jax.ShapeDtypeStruct((batch, N_MEM, 2 * C_W), BF16),
        scratch_shapes=[pltpu.VMEM(w.shape, BF16)],
        compiler_params=_params("arbitrary"),
        name="mem_kv",
    )(mem, gain, w)


def _mem_attn_kernel(h_ref, kv_ref, w_ref, o_ref, wbf_ref):
    _cast_once(w_ref, wbf_ref)
    h = h_ref[...]
    cq = (_dot_nt(h, wbf_ref[0:C_W, :]) * (C_DH ** -0.5)).astype(BF16)
    heads = [slice(hd * C_DH, (hd + 1) * C_DH) for hd in range(C_HEADS)]

    def matmuls(hs):
        gate_rows = slice(C_W + hs.start, C_W + hs.stop)
        return _dot_nt(cq[:, hs], kv_ref[0, :, hs]), _dot_nt(h, wbf_ref[gate_rows, :])

    s, cz = matmuls(heads[0])
    for hd, hs in enumerate(heads):
        nxt = matmuls(heads[hd + 1]) if hd + 1 < C_HEADS else None
        p = jnp.exp(s - jnp.max(s, axis=-1, keepdims=True))
        l = jnp.sum(p, axis=-1, keepdims=True)
        o = _dot(p.astype(BF16), kv_ref[0, :, C_W + hd * C_DH:C_W + (hd + 1) * C_DH]) / l
        o_ref[:, hs] = (o * _silu(cz)).astype(BF16)
        if nxt is not None:
            s, cz = nxt


MA_TM = 1024


def _mem_attn(h, kv, wt, seq):
    T = h.shape[0]
    tm = MA_TM
    per_seq = seq // tm
    return pl.pallas_call(
        _mem_attn_kernel,
        grid=(T // tm,),
        in_specs=[pl.BlockSpec((tm, D_MODEL), lambda i: (i, 0)),
                  pl.BlockSpec((1, N_MEM, 2 * C_W), lambda i: (i // per_seq, 0, 0)),
                  _w_in_rows(IN_OFFS[10], 2 * C_W)],
        out_specs=pl.BlockSpec((tm, C_W), lambda i: (i, 0)),
        out_shape=jax.ShapeDtypeStruct((T, C_W), BF16),
        scratch_shapes=[pltpu.VMEM((2 * C_W, D_MODEL), BF16)],
        compiler_params=_params("arbitrary"),
        name="mem_attn",
    )(h, kv, wt)


MG_TM = 1024
MG_TN = 512


def _merge_kernel(h_ref, hm_ref, ha_ref, hc_ref, wgm_ref, wga_ref, wgc_ref,
                  wbm_ref, wba_ref, wbc_ref, o_ref):
    h = h_ref[...]
    gate = lambda w_ref: _sigmoid(_dot_nt(h, w_ref[...]))
    branch = lambda a_ref, w_ref: _dot(a_ref[...], w_ref[...])
    acc = gate(wgm_ref) * branch(hm_ref, wbm_ref)
    acc = acc + gate(wga_ref) * branch(ha_ref, wba_ref)
    acc = acc + gate(wgc_ref) * branch(hc_ref, wbc_ref)
    o_ref[...] = acc.astype(BF16)


def _merge(h, hm, ha, hc, wt, wbm, wba, wbc):
    T = h.shape[0]
    tm, tn = MG_TM, MG_TN
    row = lambda w: pl.BlockSpec((tm, w), lambda i, j: (i, 0))
    col = lambda kdim: pl.BlockSpec((kdim, tn), lambda i, j: (0, j))

    def gate(branch):
        first = (W_GATE + branch * D_MODEL) // tn
        return pl.BlockSpec((tn, D_MODEL), lambda i, j: (first + j, 0))

    return pl.pallas_call(
        _merge_kernel,
        grid=(T // tm, D_MODEL // tn),
        in_specs=[row(D_MODEL), row(M_W), row(A_W), row(C_W),
                  gate(0), gate(1), gate(2), col(M_W), col(A_W), col(C_W)],
        out_specs=pl.BlockSpec((tm, tn), lambda i, j: (i, j)),
        out_shape=jax.ShapeDtypeStruct((T, D_MODEL), BF16),
        compiler_params=_params("parallel", "arbitrary"),
        name="merge",
    )(h, hm, ha, hc, wt, wt, wt, wbm, wba, wbc)


def _out_proj_kernel(final, x_ref, m_ref, w_ref, g_ref, o_ref, wbf_ref):
    _cast_once(w_ref, wbf_ref)
    y = x_ref[...] + _dot(m_ref[...], wbf_ref[...])
    o_ref[...] = _rms(y, g_ref[...]) if final else y


def _out_proj(x2, merged, w, gain, final):
    T = x2.shape[0]
    tm = SEQ_TILE
    row = pl.BlockSpec((tm, D_MODEL), lambda i: (i, 0))
    return pl.pallas_call(
        functools.partial(_out_proj_kernel, final),
        grid=(T // tm,),
        in_specs=[row, row, _const_spec(w.shape), _const_spec((1, D_MODEL))],
        out_specs=row,
        out_shape=jax.ShapeDtypeStruct((T, D_MODEL), F32),
        scratch_shapes=[pltpu.VMEM(w.shape, BF16)],
        compiler_params=_params("arbitrary"),
        name="out_proj",
    )(x2, merged, w, gain)


PK_ROWS = 1024
W_GATE = W_KR + PK_ROWS
W_ROWS = W_GATE + 3 * D_MODEL
PK_SRC = ([b * PK_ROWS for b in range(IN_OFFS[4] // PK_ROWS)]
          + [IN_OFFS[9], IN_OFFS[6], IN_OFFS[8]]
          + [IN_OFFS[12] + b * PK_ROWS for b in range(3 * D_MODEL // PK_ROWS)])


def _pack_w_kernel(w_ref, wif_ref, o_ref):
    b = pl.program_id(0)
    x = w_ref[...]
    o_ref[...] = x.astype(BF16)

    @pl.when(b == W_KR // PK_ROWS)
    def _():
        half = A_ROPE // 2
        n_if = 2 * M_HEADS
        zeros = lambda n: jnp.zeros((n, D_MODEL), F32)
        o_ref[...] = jnp.concatenate(
            [x[:half], wif_ref[...], zeros(half - n_if), x[half:2 * half], zeros(half),
             zeros(PK_ROWS - 4 * half)], axis=0).astype(BF16)


def _pack_w_in(wt):
    assert len(PK_SRC) * PK_ROWS == W_ROWS and all(s % SUBLANES == 0 for s in PK_SRC)

    def src(b):
        first = sum(jnp.where(b == i, s, 0) for i, s in enumerate(PK_SRC))
        return pl.multiple_of(first, SUBLANES), 0

    return pl.pallas_call(
        _pack_w_kernel,
        grid=(len(PK_SRC),),
        in_specs=[pl.BlockSpec((pl.Element(PK_ROWS), pl.Element(D_MODEL)), src),
                  pl.BlockSpec((pl.Element(2 * M_HEADS), pl.Element(D_MODEL)),
                               lambda b: (IN_OFFS[4], 0))],
        out_specs=pl.BlockSpec((PK_ROWS, D_MODEL), lambda b: (b, 0)),
        out_shape=jax.ShapeDtypeStruct((W_ROWS, D_MODEL), BF16),
        compiler_params=_params("parallel"),
        name="pack_w",
    )(wt, wt)


def _layer(x2, pos_row, kv_mem_in, l, final, batch, seq, w_in, b_igate, b_fgate, conv_w, conv_b,
           mh_norm, cq_norm, w_uq, ckv_norm, w_ukv, mem_norm, w_mem_kv, w_br_m, w_br_a, w_br_c,
           w_out, norm, final_norm):
    bf = lambda a: a.astype(BF16)
    row = lambda a: a.reshape(1, -1).astype(F32)

    wt = w_in[l].T
    wall = _pack_w_in(wt)
    uq = w_uq[l].reshape(Q_LORA, A_HEADS, A_DQK)
    wqn = bf(uq[:, :, :A_NOPE].reshape(Q_LORA, A_HEADS * A_NOPE))
    half = A_ROPE // 2
    wqr = bf(uq[:, :, A_NOPE:].reshape(Q_LORA, A_HEADS // 2, 2, 2, half)
             .transpose(0, 1, 3, 2, 4).reshape(Q_LORA, (A_HEADS // 2) * LANES))
    ukv = w_ukv[l].reshape(KV_LORA, A_HEADS, A_NOPE + A_DV)
    wkn = bf(ukv[:, :, :A_NOPE].reshape(KV_LORA, A_HEADS * A_NOPE))
    wvvt = bf(ukv[:, :, A_NOPE:].reshape(KV_LORA, A_HEADS * A_DV).T)
    bias_row = jnp.pad(jnp.concatenate([b_igate[l], b_fgate[l]]).astype(F32),
                       (IF_LANE, LANES - IF_LANE - 2 * M_HEADS)).reshape(1, LANES)
    invf = (ROPE_THETA ** (-jnp.arange(0, A_ROPE, 2, dtype=F32) / A_ROPE)).reshape(-1, 1)

    lane_rep = lambda a: jnp.broadcast_to(a.astype(F32)[..., None], a.shape + (LANES,))
    h, mqt, mk, mvt, mgate = _mlstm_proj(x2, row(norm[l]), wall, lane_rep(conv_w[l]),
                                         lane_rep(conv_b[l]), batch, seq)
    aq, ak, avt, agate, pre_if = _mla_proj(h, pos_row, invf, row(cq_norm[l]), row(ckv_norm[l]),
                                           wall, wqn, wqr, wkn, wvvt, batch, seq)
    rowf, colf = _gate_scan(pre_if, bias_row, batch, seq)
    hm = _decay_attn(mqt, mk, mvt, mgate, rowf, colf, row(mh_norm[l]), batch, seq)
    ha = _mla_attn(aq, ak, avt, agate, batch, seq)

    kv_mem = _mem_kv(kv_mem_in, row(mem_norm[l]), w_mem_kv[l])
    hc = _mem_attn(h, kv_mem, wt, seq)

    merged = _merge(h, hm, ha, hc, wall, bf(w_br_m[l]), bf(w_br_a[l]), bf(w_br_c[l]))
    return _out_proj(x2, merged, w_out[l], row(final_norm), final)


def kernel(x, mem, positions, w_in, b_igate, b_fgate, conv_w, conv_b, mh_norm, cq_norm, w_uq,
           ckv_norm, w_ukv, mem_norm, w_mem_kv, w_br_m, w_br_a, w_br_c, w_out, norm, final_norm):
    batch, seq, d = x.shape
    depth = w_in.shape[0]
    assert d == D_MODEL and seq % MG_TM == 0 and w_in.shape[2] == sum(IN_SPLITS)
    x2 = x.reshape(batch * seq, d)
    pos_row = positions.astype(F32).reshape(batch * seq // SEQ_TILE, 1, SEQ_TILE)
    for l in range(depth):
        x2 = _layer(x2, pos_row, mem, l, l == depth - 1, batch, seq, w_in, b_igate, b_fgate,
                    conv_w, conv_b, mh_norm, cq_norm, w_uq, ckv_norm, w_ukv, mem_norm, w_mem_kv,
                    w_br_m, w_br_a, w_br_c, w_out, norm, final_norm)
    return x2.reshape(batch, seq, d)
```

```python
import functools
import math

import jax
import jax.numpy as jnp
from jax import lax
from jax.experimental import pallas as pl
from jax.experimental.pallas import tpu as pltpu

F32 = jnp.float32
BF16 = jnp.bfloat16

D_MODEL = 2048
M_HEADS, M_DH = 4, 256
M_W = M_HEADS * M_DH
CONV_K = 4
A_HEADS, A_NOPE, A_ROPE, A_DV = 8, 128, 64, 128
A_DQK = A_NOPE + A_ROPE
A_W = A_HEADS * A_DV
Q_LORA = KV_LORA = 512
ROPE_THETA = 10000.0
N_MEM = 256
C_HEADS, C_DH = 4, 256
C_W = C_HEADS * C_DH
EPS = 1e-6

LANES = 128
SUBLANES = 8
A_QK_PAD = 2 * LANES
VMEM_LIMIT = 56 * 1024 * 1024

SEQ_TILE = 512
HEADS_PER_STEP = 2
MLA_HEADS_PER_STEP = 4

IN_SPLITS = (2 * M_W, M_W, M_W, M_W, M_HEADS, M_HEADS, Q_LORA, KV_LORA, A_ROPE, A_W, C_W, C_W,
             3 * D_MODEL)


def _const_spec(shape):
    nd = len(shape)
    return pl.BlockSpec(shape, lambda *_: (0,) * nd, pipeline_mode=pl.Buffered(1))


IN_OFFS = [sum(IN_SPLITS[:i]) for i in range(len(IN_SPLITS) + 1)]

W_QK, W_V, W_O, W_Z, W_AZ = 0, 2 * M_W, 3 * M_W, 4 * M_W, 5 * M_W
W_CQ = W_AZ + A_W
W_CKV = W_CQ + Q_LORA
W_KR = W_CKV + KV_LORA
IF_LANE = A_ROPE // 2


def _w_rows(offset, rows):
    assert offset % rows == 0
    return pl.BlockSpec((rows, D_MODEL), lambda *_: (offset // rows, 0),
                        pipeline_mode=pl.Buffered(1))


def _w_in_rows(first, rows):
    assert first % SUBLANES == 0
    return pl.BlockSpec((pl.Element(rows), pl.Element(D_MODEL)), lambda *_: (first, 0),
                        pipeline_mode=pl.Buffered(1))


def _params(*sem):
    return pltpu.CompilerParams(dimension_semantics=sem, vmem_limit_bytes=VMEM_LIMIT)


def _dot(a, b):
    return jnp.dot(a, b, preferred_element_type=F32)


def _dot_nt(a, b):
    return lax.dot_general(a, b, (((1,), (1,)), ((), ())), preferred_element_type=F32)


def _rms(x, g):
    return x * lax.rsqrt(jnp.mean(x * x, axis=-1, keepdims=True) + EPS) * g


def _sigmoid(x):
    return 1.0 / (1.0 + jnp.exp(-x))


def _silu(x):
    return x * _sigmoid(x)


def _keys_le_queries(nk, nq):
    shape = (nk, nq)
    return lax.broadcasted_iota(jnp.int32, shape, 0) <= lax.broadcasted_iota(jnp.int32, shape, 1)


def _causal_pieces(qi, t, nk_full):
    half = t // 2
    return ([(k0, nk_full, 0, False) for k0 in range(0, qi * t, nk_full)]
            + [(qi * t, half, 0, True), (qi * t + half, half, half, True)])


def _two_phase(chains, first, second, ahead):
    pending = [first(*c) for c in chains[:ahead]]
    for n, chain in enumerate(chains):
        if n + ahead < len(chains):
            pending.append(first(*chains[n + ahead]))
        second(*chain, pending.pop(0))


def _left_pad(x, n, fill):
    if n == 0:
        return x
    return jnp.concatenate([jnp.full((x.shape[0], n), fill, x.dtype), x], axis=1)


MP_CH = 1024


def _mlstm_proj_kernel(tiles_per_seq, x_ref, ng_ref, wqk_ref, cw_ref, cb_ref, wvt_ref, wo_ref,
                       wz_ref, h_ref, qt_ref, k_ref, vt_ref, g_ref, xbuf_ref):
    tm = x_ref.shape[0]

    @pl.when(pl.program_id(0) % tiles_per_seq == 0)
    def _():
        xbuf_ref[...] = jnp.zeros_like(xbuf_ref)

    h = _rms(x_ref[...], ng_ref[...]).astype(BF16)
    h_ref[...] = h
    widen = lambda a: jnp.concatenate([a] * (tm // LANES), axis=1)

    def conv_chunk(c):
        fs = slice(c * MP_CH, (c + 1) * MP_CH)
        acc = _dot_nt(wqk_ref[fs, :], h)
        prev = xbuf_ref[fs, :]
        lane = lax.broadcasted_iota(jnp.int32, prev.shape, 1)
        y = widen(cb_ref[fs, :]) + widen(cw_ref[CONV_K - 1, fs, :]) * acc
        for j in range(CONV_K - 1):
            back = CONV_K - 1 - j
            rolled = pltpu.roll(acc, back, 1)
            head = jnp.where(lane < back, pltpu.roll(prev, back, 1), rolled[:, :LANES])
            tap = jnp.concatenate([head, rolled[:, LANES:]], axis=1)
            y = y + widen(cw_ref[j, fs, :]) * tap
        xbuf_ref[fs, :] = acc[:, tm - LANES:]
        y = _silu(y)
        if c < M_W // MP_CH:
            qt_ref[0, 0, fs, :] = y.astype(BF16)
        else:
            ks = slice(c * MP_CH - M_W, (c + 1) * MP_CH - M_W)
            k_ref[:, ks] = (y.T * (M_DH ** -0.5)).astype(BF16)

    def value_chunk(c):
        cs = slice(c * MP_CH, (c + 1) * MP_CH)
        vt_ref[0, 0, cs, :] = _dot_nt(wvt_ref[cs, :], h).astype(BF16)

    def gate_chunk(c):
        cs = slice(c * MP_CH, (c + 1) * MP_CH)
        o = _dot_nt(h, wo_ref[cs, :])
        z = _dot_nt(h, wz_ref[cs, :])
        g_ref[:, cs] = (z / ((1.0 + jnp.exp(-o)) * (1.0 + jnp.exp(-z)))).astype(BF16)

    light = [functools.partial(f, c) for c in range(M_W // MP_CH) for f in (value_chunk, gate_chunk)]
    for c in range(2 * M_W // MP_CH):
        conv_chunk(c)
        if c < len(light):
            light[c]()
    for f in light[2 * M_W // MP_CH:]:
        f()


def _mlstm_proj(x2, norm_g, wall, conv_w, conv_b, batch, seq):
    T = x2.shape[0]
    tm = SEQ_TILE
    per_seq = seq // tm
    row = lambda w: pl.BlockSpec((tm, w), lambda i: (i, 0))
    out_shape = (
        jax.ShapeDtypeStruct((T, D_MODEL), BF16),
        jax.ShapeDtypeStruct((batch, per_seq, M_W, tm), BF16),
        jax.ShapeDtypeStruct((T, M_W), BF16),
        jax.ShapeDtypeStruct((batch, per_seq, M_W, tm), BF16),
        jax.ShapeDtypeStruct((T, M_W), BF16),
    )
    vt_spec = pl.BlockSpec((1, 1, M_W, tm), lambda i: (i // per_seq, i % per_seq, 0, 0))
    return pl.pallas_call(
        functools.partial(_mlstm_proj_kernel, per_seq),
        grid=(T // tm,),
        in_specs=[row(D_MODEL), _const_spec((1, D_MODEL)), _w_rows(W_QK, 2 * M_W),
                  _const_spec(conv_w.shape), _const_spec(conv_b.shape), _w_rows(W_V, M_W),
                  _w_rows(W_O, M_W), _w_rows(W_Z, M_W)],
        out_specs=(row(D_MODEL), vt_spec, row(M_W), vt_spec, row(M_W)),
        out_shape=out_shape,
        scratch_shapes=[pltpu.VMEM((2 * M_W, LANES), F32)],
        compiler_params=_params("arbitrary"),
        name="mlstm_proj",
    )(x2, norm_g, wall, conv_w, conv_b, wall, wall, wall)


def _lane_scan(x, op, fill):
    n = x.shape[-1]
    lane = lax.broadcasted_iota(jnp.int32, x.shape, x.ndim - 1)
    d = 1
    while d < n:
        shifted = pltpu.roll(x, d, x.ndim - 1)
        x = op(x, jnp.where(lane >= d, shifted, fill))
        d *= 2
    return x


def _gate_scan_kernel(if_ref, bias_ref, row_ref, col_ref):
    n_seq, S = if_ref.shape[0], if_ref.shape[1]
    t = SEQ_TILE
    rows8 = lambda x, s: x[s * SUBLANES:(s + 1) * SUBLANES]
    t8 = jnp.concatenate([(if_ref[s] + bias_ref[...]).T[IF_LANE:IF_LANE + SUBLANES, :]
                          for s in range(n_seq)], axis=0)
    lf = jnp.minimum(t8, 0.0) - jnp.log(1.0 + jnp.exp(-jnp.abs(t8)))
    b_all = _lane_scan(lf, jnp.add, 0.0)
    b_all = jnp.concatenate([pltpu.roll(rows8(b_all, s), M_HEADS, 0) for s in range(n_seq)],
                            axis=0)
    a_all = t8 - b_all
    mx_all = jnp.maximum(_lane_scan(a_all, jnp.maximum, -jnp.inf), 0.0)
    nb_all = -b_all - mx_all
    sub = lax.broadcasted_iota(jnp.int32, (SUBLANES, S), 0)
    zeros = jnp.zeros((LANES - SUBLANES, S), F32)
    n_h = HEADS_PER_STEP
    for s in range(n_seq):
        a, mx, nb = rows8(a_all, s), rows8(mx_all, s), rows8(nb_all, s)
        for p in range(M_HEADS // n_h):
            up = (SUBLANES - n_h * p) % SUBLANES
            m_grp = pltpu.roll(mx, up, 0) if up else mx
            nb_grp = pltpu.roll(nb, (up + n_h) % SUBLANES, 0)
            stack = jnp.where(sub < n_h, m_grp, nb_grp)
            for j in range(S // t):
                row_ref[s, p, j] = stack[:, j * t:(j + 1) * t]
            a_grp = pltpu.roll(a, up, 0) if up else a
            col_ref[s, p] = jnp.concatenate([a_grp, zeros], axis=0).T


GS_SEQS = 4


def _gate_scan(pre_if, bias_row, batch, seq):
    n_grp = M_HEADS // HEADS_PER_STEP
    n_tile = seq // SEQ_TILE
    ns = GS_SEQS if batch % GS_SEQS == 0 else 1
    return pl.pallas_call(
        _gate_scan_kernel,
        grid=(batch // ns,),
        in_specs=[pl.BlockSpec((ns, seq, LANES), lambda b: (b, 0, 0)), _const_spec((1, LANES))],
        out_specs=(pl.BlockSpec((ns, n_grp, n_tile, SUBLANES, SEQ_TILE), lambda b: (b, 0, 0, 0, 0)),
                   pl.BlockSpec((ns, n_grp, seq, LANES), lambda b: (b, 0, 0, 0))),
        out_shape=(jax.ShapeDtypeStruct((batch, n_grp, n_tile, SUBLANES, SEQ_TILE), F32),
                   jax.ShapeDtypeStruct((batch, n_grp, seq, LANES), F32)),
        compiler_params=_params("parallel"),
        name="gate_scan",
    )(pre_if.reshape(batch, seq, LANES), bias_row)


def _decay_attn_kernel(qt_ref, k_ref, vt_ref, g_ref, row_ref, col_ref, gain_ref, o_ref,
                       arep_ref, fac_ref):
    S = k_ref.shape[1]
    t = SEQ_TILE
    n_head = HEADS_PER_STEP
    hs = [slice(hh * M_DH, (hh + 1) * M_DH) for hh in range(n_head)]
    lane_tiles = t // LANES
    widen = lambda a: jnp.concatenate([a] * lane_tiles, axis=1)

    c_rep = [[None] * (S // t) for _ in range(n_head)]
    for hh in range(n_head):
        arep_ref[hh] = jnp.broadcast_to(col_ref[0, 0, :, hh:hh + 1], (S, LANES))
        for kj in range(S // t - 1):
            a = arep_ref[hh, kj * t:(kj + 1) * t, :]
            c_rep[hh][kj] = jnp.max(a, axis=0, keepdims=True)
            fac_ref[hh, kj * t:(kj + 1) * t, :] = jnp.exp(a - c_rep[hh][kj])

    def scores(qi, hh):
        qt = qt_ref[0, qi, hs[hh], :]
        return [_dot(k_ref[0, k0:k0 + nk, hs[hh]], qt[:, q0:])
                for k0, nk, q0, _ in _causal_pieces(qi, t, t)]

    def weigh_pv(qi, hh, sts):
        qs = slice(qi * t, (qi + 1) * t)
        m_row = row_ref[0, 0, qi, hh:hh + 1, :]
        nb_row = row_ref[0, 0, qi, n_head + hh:n_head + hh + 1, :]
        den = num = None
        for st, (k0, nk, q0, diag) in zip(sts, _causal_pieces(qi, t, t)):
            ks = slice(k0, k0 + nk)
            kc, ko = divmod(k0, t)
            if diag:
                arg = widen(arep_ref[hh, ks, :])[:, q0:] - m_row[:, q0:]
                p = st * jnp.exp(jnp.where(_keys_le_queries(nk, t - q0), arg, -jnp.inf))
                d_blk = jnp.sum(p, axis=0, keepdims=True)
                n_blk = _dot(vt_ref[0, kc, hs[hh], ko:ko + nk], p.astype(BF16))
            else:
                p = st * widen(fac_ref[hh, ks, :])
                qfac = jnp.exp(widen(c_rep[hh][kc]) - m_row)
                d_blk = qfac * jnp.sum(p, axis=0, keepdims=True)
                n_blk = qfac * _dot(vt_ref[0, kc, hs[hh], ko:ko + nk], p.astype(BF16))
            d_blk, n_blk = _left_pad(d_blk, q0, 0.0), _left_pad(n_blk, q0, 0.0)
            den, num = (d_blk, n_blk) if den is None else (den + d_blk, num + n_blk)
        hv = num / jnp.maximum(jnp.abs(den), jnp.exp(nb_row))
        hv = hv * lax.rsqrt(jnp.mean(hv * hv, axis=0, keepdims=True) + EPS)
        o_ref[0, qs, hs[hh]] = (hv.T * gain_ref[:, hs[hh]]
                                * g_ref[0, qs, hs[hh]].astype(F32)).astype(BF16)

    chains = [(qi, hh) for qi in range(S // t) for hh in range(n_head)]
    _two_phase(chains, scores, weigh_pv, ahead=1)


def _decay_attn(qt, k, vt, gate, rowf, colf, gain, batch, seq):
    n_grp = M_HEADS // HEADS_PER_STEP
    n_tile = seq // SEQ_TILE
    w = HEADS_PER_STEP * M_DH
    blk = pl.BlockSpec((1, seq, w), lambda b, p: (b, 0, p))
    t_blk = pl.BlockSpec((1, n_tile, w, SEQ_TILE), lambda b, p: (b, 0, p, 0))
    k3, g3 = (a.reshape(batch, seq, M_W) for a in (k, gate))
    return pl.pallas_call(
        _decay_attn_kernel,
        grid=(batch, n_grp),
        in_specs=[t_blk, blk, t_blk, blk,
                  pl.BlockSpec((1, 1, n_tile, SUBLANES, SEQ_TILE), lambda b, p: (b, p, 0, 0, 0)),
                  pl.BlockSpec((1, 1, seq, LANES), lambda b, p: (b, p, 0, 0)),
                  pl.BlockSpec((1, w), lambda b, p: (0, p))],
        out_specs=blk,
        out_shape=jax.ShapeDtypeStruct((batch, seq, M_W), BF16),
        scratch_shapes=[pltpu.VMEM((HEADS_PER_STEP, seq, LANES), F32),
                        pltpu.VMEM((HEADS_PER_STEP, seq - SEQ_TILE, LANES), F32)],
        compiler_params=_params("parallel", "parallel"),
        name="decay_attn",
    )(qt, k3, vt, g3, rowf, colf, gain).reshape(batch * seq, M_W)


def _mla_proj_kernel(h_ref, pos_ref, invf_ref, gq_ref, gkv_ref, wcq_ref, wckv_ref,
                     wkr_ref, waz_ref, wqn_ref, wqr_ref, wkn_ref, wvt_ref,
                     q_ref, k_ref, vt_ref, g_ref, if_ref):
    h = h_ref[...]
    tm = h.shape[0]
    scale = math.log2(math.e) / math.sqrt(A_DQK)
    ang = invf_ref[...] * pos_ref[0]
    c32, s32 = jnp.cos(ang), jnp.sin(ang)
    z32 = jnp.zeros((A_ROPE // 2, tm), F32)
    cos = jnp.concatenate([c32, z32, c32, z32], axis=0).T
    sin = jnp.concatenate([-s32, z32, s32, z32], axis=0).T
    cos_pair = jnp.concatenate([c32, c32, c32, c32], axis=0).T
    sin_pair = jnp.concatenate([-s32, -s32, s32, s32], axis=0).T

    def rope(r, cos=cos, sin=sin):
        return r * cos + pltpu.roll(r, LANES // 2, 1) * sin

    cq = _dot_nt(h, wcq_ref[...])
    ckv = _dot_nt(h, wckv_ref[...])
    g_ref[...] = _silu(_dot_nt(h, waz_ref[...])).astype(BF16)
    kr_if = _dot_nt(h, wkr_ref[...])
    if_ref[...] = kr_if
    kr = rope(kr_if)
    kr_by_parity = (kr.astype(BF16), pltpu.roll(kr, A_ROPE // 2, 1).astype(BF16))

    cqn = _rms(cq, gq_ref[...]).astype(BF16)
    ckvn = _rms(ckv, gkv_ref[...]).astype(BF16)
    qn = _dot(cqn, wqn_ref[...]) * scale
    qr = _dot(cqn, wqr_ref[...]) * scale
    kn = _dot(ckvn, wkn_ref[...])
    for hd in range(A_HEADS):
        base = hd * A_QK_PAD
        ls = slice(hd * LANES, (hd + 1) * LANES)
        pair = slice((hd // 2) * LANES, (hd // 2 + 1) * LANES)
        q_ref[:, base:base + LANES] = qn[:, ls].astype(BF16)
        q_ref[:, base + LANES:base + A_QK_PAD] = rope(qr[:, pair], cos_pair, sin_pair).astype(BF16)
        k_ref[:, base:base + LANES] = kn[:, ls].astype(BF16)
        k_ref[:, base + LANES:base + A_QK_PAD] = kr_by_parity[hd % 2]
    vt_ref[0, 0] = _dot_nt(wvt_ref[...], ckvn).astype(BF16)


def _mla_proj(h, pos_row, invf, gq, gkv, wall, wqn, wqr, wkn, wvt, batch, seq):
    T = h.shape[0]
    tm = SEQ_TILE
    per_seq = seq // tm
    row = lambda w: pl.BlockSpec((tm, w), lambda i: (i, 0))
    small = (invf, gq, gkv)
    ups = (wqn, wqr, wkn, wvt)
    vt_spec = pl.BlockSpec((1, 1, A_W, tm), lambda i: (i // per_seq, i % per_seq, 0, 0))
    return pl.pallas_call(
        _mla_proj_kernel,
        grid=(T // tm,),
        in_specs=([row(D_MODEL), pl.BlockSpec((1, 1, tm), lambda i: (i, 0, 0))]
                  + [_const_spec(c.shape) for c in small]
                  + [_w_rows(W_CQ, Q_LORA), _w_rows(W_CKV, KV_LORA), _w_rows(W_KR, LANES),
                     _w_rows(W_AZ, A_W)]
                  + [_const_spec(c.shape) for c in ups]),
        out_specs=(row(A_HEADS * A_QK_PAD), row(A_HEADS * A_QK_PAD), vt_spec, row(A_W),
                   row(LANES)),
        out_shape=(jax.ShapeDtypeStruct((T, A_HEADS * A_QK_PAD), BF16),
                   jax.ShapeDtypeStruct((T, A_HEADS * A_QK_PAD), BF16),
                   jax.ShapeDtypeStruct((batch, per_seq, A_W, tm), BF16),
                   jax.ShapeDtypeStruct((T, A_W), BF16),
                   jax.ShapeDtypeStruct((T, LANES), F32)),
        compiler_params=_params("parallel"),
        name="mla_proj",
    )(h, pos_row, *small, wall, wall, wall, wall, *ups)


def _mla_attn_kernel(q_ref, k_ref, vt_ref, g_ref, o_ref):
    S = q_ref.shape[1]
    t = SEQ_TILE
    n_head = MLA_HEADS_PER_STEP
    hq = [slice(hh * A_QK_PAD, (hh + 1) * A_QK_PAD) for hh in range(n_head)]
    hv = [slice(hh * A_DV, (hh + 1) * A_DV) for hh in range(n_head)]

    def scores(qi, hh):
        q = q_ref[0, qi * t:(qi + 1) * t, hq[hh]]
        st = []
        for k0, nk, q0, diag in _causal_pieces(qi, t, t):
            s = _dot_nt(k_ref[0, k0:k0 + nk, hq[hh]], q[q0:, :])
            st.append(jnp.where(_keys_le_queries(nk, t - q0), s, -jnp.inf) if diag else s)
        return st

    def softmax_pv(qi, hh, st):
        qs = slice(qi * t, (qi + 1) * t)
        pieces = _causal_pieces(qi, t, t)
        m = functools.reduce(jnp.maximum, [
            _left_pad(jnp.max(s, axis=0, keepdims=True), q0, -jnp.inf)
            for s, (_, _, q0, _) in zip(st, pieces)])
        l = acc = None
        for s, (k0, nk, q0, _) in zip(st, pieces):
            p = jnp.exp2(s - m[:, q0:])
            l_blk = _left_pad(jnp.sum(p, axis=0, keepdims=True), q0, 0.0)
            kc, ko = divmod(k0, t)
            pv = _left_pad(_dot(vt_ref[0, kc, hv[hh], ko:ko + nk], p.astype(BF16)), q0, 0.0)
            l, acc = (l_blk, pv) if l is None else (l + l_blk, acc + pv)
        o_ref[0, qs, hv[hh]] = ((acc / l).T * g_ref[0, qs, hv[hh]].astype(F32)).astype(BF16)

    chains = [(qi, hh) for qi in range(S // t) for hh in range(n_head)]
    _two_phase(chains, scores, softmax_pv, ahead=2)


def _mla_attn(q, k, vt, gate, batch, seq):
    n_tile = seq // SEQ_TILE
    n_h = MLA_HEADS_PER_STEP
    qk_blk = pl.BlockSpec((1, seq, n_h * A_QK_PAD), lambda b, p: (b, 0, p))
    v_blk = pl.BlockSpec((1, seq, n_h * A_DV), lambda b, p: (b, 0, p))
    vt_blk = pl.BlockSpec((1, n_tile, n_h * A_DV, SEQ_TILE), lambda b, p: (b, 0, p, 0))
    q3 = q.reshape(batch, seq, A_HEADS * A_QK_PAD)
    k3 = k.reshape(batch, seq, A_HEADS * A_QK_PAD)
    g3 = gate.reshape(batch, seq, A_W)
    return pl.pallas_call(
        _mla_attn_kernel,
        grid=(batch, A_HEADS // n_h),
        in_specs=[qk_blk, qk_blk, vt_blk, v_blk],
        out_specs=v_blk,
        out_shape=jax.ShapeDtypeStruct((batch, seq, A_W), BF16),
        compiler_params=_params("parallel", "parallel"),
        name="mla_attn",
    )(q3, k3, vt, g3).reshape(batch * seq, A_W)


def _cast_once(w_ref, wbf_ref):
    @pl.when(pl.program_id(0) == 0)
    def _():
        wbf_ref[...] = w_ref[...].astype(BF16)


def _mem_kv_kernel(mem_ref, g_ref, w_ref, kv_ref, wbf_ref):
    _cast_once(w_ref, wbf_ref)
    nb, nm, d = mem_ref.shape
    m = _rms(mem_ref[...].reshape(nb * nm, d), g_ref[...]).astype(BF16)
    kv_ref[...] = _dot(m, wbf_ref[...]).astype(BF16).reshape(nb, nm, 2 * C_W)


MKV_BATCH = 4


def _mem_kv(mem, gain, w):
    batch = mem.shape[0]
    nb = MKV_BATCH if batch % MKV_BATCH == 0 else 1
    return pl.pallas_call(
        _mem_kv_kernel,
        grid=(batch // nb,),
        in_specs=[pl.BlockSpec((nb, N_MEM, D_MODEL), lambda b: (b, 0, 0)),
                  _const_spec((1, D_MODEL)), _const_spec(w.shape)],
        out_specs=pl.BlockSpec((nb, N_MEM, 2 * C_W), lambda b: (b, 0, 0)),
        out_shape=jax.ShapeDtypeStruct((batch, N_MEM, 2 * C_W), BF16),
        scratch_shapes=[pltpu.VMEM(w.shape, BF16)],
        compiler_params=_params("arbitrary"),
        name="mem_kv",
    )(mem, gain, w)


def _mem_attn_kernel(h_ref, kv_ref, w_ref, o_ref, wbf_ref):
    _cast_once(w_ref, wbf_ref)
    h = h_ref[...]
    cq = (_dot_nt(h, wbf_ref[0:C_W, :]) * (C_DH ** -0.5)).astype(BF16)
    heads = [slice(hd * C_DH, (hd + 1) * C_DH) for hd in range(C_HEADS)]

    def matmuls(hs):
        gate_rows = slice(C_W + hs.start, C_W + hs.stop)
        return _dot_nt(cq[:, hs], kv_ref[0, :, hs]), _dot_nt(h, wbf_ref[gate_rows, :])

    s, cz = matmuls(heads[0])
    for hd, hs in enumerate(heads):
        nxt = matmuls(heads[hd + 1]) if hd + 1 < C_HEADS else None
        p = jnp.exp(s - jnp.max(s, axis=-1, keepdims=True))
        l = jnp.sum(p, axis=-1, keepdims=True)
        o = _dot(p.astype(BF16), kv_ref[0, :, C_W + hd * C_DH:C_W + (hd + 1) * C_DH]) / l
        o_ref[:, hs] = (o * _silu(cz)).astype(BF16)
        if nxt is not None:
            s, cz = nxt


MA_TM = 1024


def _mem_attn(h, kv, wt, seq):
    T = h.shape[0]
    tm = MA_TM
    per_seq = seq // tm
    return pl.pallas_call(
        _mem_attn_kernel,
        grid=(T // tm,),
        in_specs=[pl.BlockSpec((tm, D_MODEL), lambda i: (i, 0)),
                  pl.BlockSpec((1, N_MEM, 2 * C_W), lambda i: (i // per_seq, 0, 0)),
                  _w_in_rows(IN_OFFS[10], 2 * C_W)],
        out_specs=pl.BlockSpec((tm, C_W), lambda i: (i, 0)),
        out_shape=jax.ShapeDtypeStruct((T, C_W), BF16),
        scratch_shapes=[pltpu.VMEM((2 * C_W, D_MODEL), BF16)],
        compiler_params=_params("arbitrary"),
        name="mem_attn",
    )(h, kv, wt)


MG_TM = 1024
MG_TN = 512


def _merge_kernel(h_ref, hm_ref, ha_ref, hc_ref, wgm_ref, wga_ref, wgc_ref,
                  wbm_ref, wba_ref, wbc_ref, o_ref):
    h = h_ref[...]
    gate = lambda w_ref: _sigmoid(_dot_nt(h, w_ref[...]))
    branch = lambda a_ref, w_ref: _dot(a_ref[...], w_ref[...])
    acc = gate(wgm_ref) * branch(hm_ref, wbm_ref)
    acc = acc + gate(wga_ref) * branch(ha_ref, wba_ref)
    acc = acc + gate(wgc_ref) * branch(hc_ref, wbc_ref)
    o_ref[...] = acc.astype(BF16)


def _merge(h, hm, ha, hc, wt, wbm, wba, wbc):
    T = h.shape[0]
    tm, tn = MG_TM, MG_TN
    row = lambda w: pl.BlockSpec((tm, w), lambda i, j: (i, 0))
    col = lambda kdim: pl.BlockSpec((kdim, tn), lambda i, j: (0, j))

    def gate(branch):
        first = (W_GATE + branch * D_MODEL) // tn
        return pl.BlockSpec((tn, D_MODEL), lambda i, j: (first + j, 0))

    return pl.pallas_call(
        _merge_kernel,
        grid=(T // tm, D_MODEL // tn),
        in_specs=[row(D_MODEL), row(M_W), row(A_W), row(C_W),
                  gate(0), gate(1), gate(2), col(M_W), col(A_W), col(C_W)],
        out_specs=pl.BlockSpec((tm, tn), lambda i, j: (i, j)),
        out_shape=jax.ShapeDtypeStruct((T, D_MODEL), BF16),
        compiler_params=_params("parallel", "arbitrary"),
        name="merge",
    )(h, hm, ha, hc, wt, wt, wt, wbm, wba, wbc)


def _out_proj_kernel(final, x_ref, m_ref, w_ref, g_ref, o_ref, wbf_ref):
    _cast_once(w_ref, wbf_ref)
    y = x_ref[...] + _dot(m_ref[...], wbf_ref[...])
    o_ref[...] = _rms(y, g_ref[...]) if final else y


def _out_proj(x2, merged, w, gain, final):
    T = x2.shape[0]
    tm = SEQ_TILE
    row = pl.BlockSpec((tm, D_MODEL), lambda i: (i, 0))
    return pl.pallas_call(
        functools.partial(_out_proj_kernel, final),
        grid=(T // tm,),
        in_specs=[row, row, _const_spec(w.shape), _const_spec((1, D_MODEL))],
        out_specs=row,
        out_shape=jax.ShapeDtypeStruct((T, D_MODEL), F32),
        scratch_shapes=[pltpu.VMEM(w.shape, BF16)],
        compiler_params=_params("arbitrary"),
        name="out_proj",
    )(x2, merged, w, gain)


PK_ROWS = 1024
W_GATE = W_KR + PK_ROWS
W_ROWS = W_GATE + 3 * D_MODEL
PK_SRC = ([b * PK_ROWS for b in range(IN_OFFS[4] // PK_ROWS)]
          + [IN_OFFS[9], IN_OFFS[6], IN_OFFS[6]]
          + [IN_OFFS[12] + b * PK_ROWS for b in range(3 * D_MODEL // PK_ROWS)])


def _pack_w_kernel(w_ref, wif_ref, wkr_ref, o_ref):
    b = pl.program_id(0)
    o_ref[...] = w_ref[...].astype(BF16)

    @pl.when(b == W_KR // PK_ROWS)
    def _():
        half = A_ROPE // 2
        n_if = 2 * M_HEADS
        kr = wkr_ref[...]
        zeros = lambda n: jnp.zeros((n, D_MODEL), F32)
        o_ref[...] = jnp.concatenate(
            [kr[:half], wif_ref[...], zeros(half - n_if), kr[half:], zeros(half),
             zeros(PK_ROWS - 4 * half)], axis=0).astype(BF16)


def _pack_w_in(wt):
    assert len(PK_SRC) * PK_ROWS == W_ROWS and all(s % SUBLANES == 0 for s in PK_SRC)

    def src(b):
        first = sum(jnp.where(b == i, s, 0) for i, s in enumerate(PK_SRC))
        return pl.multiple_of(first, SUBLANES), 0

    return pl.pallas_call(
        _pack_w_kernel,
        grid=(len(PK_SRC),),
        in_specs=[pl.BlockSpec((pl.Element(PK_ROWS), pl.Element(D_MODEL)), src),
                  pl.BlockSpec((pl.Element(2 * M_HEADS), pl.Element(D_MODEL)),
                               lambda b: (IN_OFFS[4], 0)),
                  pl.BlockSpec((pl.Element(A_ROPE), pl.Element(D_MODEL)),
                               lambda b: (IN_OFFS[8], 0))],
        out_specs=pl.BlockSpec((PK_ROWS, D_MODEL), lambda b: (b, 0)),
        out_shape=jax.ShapeDtypeStruct((W_ROWS, D_MODEL), BF16),
        compiler_params=_params("arbitrary"),
        name="pack_w",
    )(wt, wt, wt)


def _layer(x2, pos_row, kv_mem_in, l, final, batch, seq, w_in, b_igate, b_fgate, conv_w, conv_b,
           mh_norm, cq_norm, w_uq, ckv_norm, w_ukv, mem_norm, w_mem_kv, w_br_m, w_br_a, w_br_c,
           w_out, norm, final_norm):
    bf = lambda a: a.astype(BF16)
    row = lambda a: a.reshape(1, -1).astype(F32)

    wt = w_in[l].T
    wall = _pack_w_in(wt)
    uq = w_uq[l].reshape(Q_LORA, A_HEADS, A_DQK)
    wqn = bf(uq[:, :, :A_NOPE].reshape(Q_LORA, A_HEADS * A_NOPE))
    half = A_ROPE // 2
    wqr = bf(uq[:, :, A_NOPE:].reshape(Q_LORA, A_HEADS // 2, 2, 2, half)
             .transpose(0, 1, 3, 2, 4).reshape(Q_LORA, (A_HEADS // 2) * LANES))
    ukv = w_ukv[l].reshape(KV_LORA, A_HEADS, A_NOPE + A_DV)
    wkn = bf(ukv[:, :, :A_NOPE].reshape(KV_LORA, A_HEADS * A_NOPE))
    wvvt = bf(ukv[:, :, A_NOPE:].reshape(KV_LORA, A_HEADS * A_DV).T)
    bias_row = jnp.pad(jnp.concatenate([b_igate[l], b_fgate[l]]).astype(F32),
                       (IF_LANE, LANES - IF_LANE - 2 * M_HEADS)).reshape(1, LANES)
    invf = (ROPE_THETA ** (-jnp.arange(0, A_ROPE, 2, dtype=F32) / A_ROPE)).reshape(-1, 1)

    lane_rep = lambda a: jnp.broadcast_to(a.astype(F32)[..., None], a.shape + (LANES,))
    h, mqt, mk, mvt, mgate = _mlstm_proj(x2, row(norm[l]), wall, lane_rep(conv_w[l]),
                                         lane_rep(conv_b[l]), batch, seq)
    aq, ak, avt, agate, pre_if = _mla_proj(h, pos_row, invf, row(cq_norm[l]), row(ckv_norm[l]),
                                           wall, wqn, wqr, wkn, wvvt, batch, seq)
    rowf, colf = _gate_scan(pre_if, bias_row, batch, seq)
    hm = _decay_attn(mqt, mk, mvt, mgate, rowf, colf, row(mh_norm[l]), batch, seq)
    ha = _mla_attn(aq, ak, avt, agate, batch, seq)

    kv_mem = _mem_kv(kv_mem_in, row(mem_norm[l]), w_mem_kv[l])
    hc = _mem_attn(h, kv_mem, wt, seq)

    merged = _merge(h, hm, ha, hc, wall, bf(w_br_m[l]), bf(w_br_a[l]), bf(w_br_c[l]))
    return _out_proj(x2, merged, w_out[l], row(final_norm), final)


def kernel(x, mem, positions, w_in, b_igate, b_fgate, conv_w, conv_b, mh_norm, cq_norm, w_uq,
           ckv_norm, w_ukv, mem_norm, w_mem_kv, w_br_m, w_br_a, w_br_c, w_out, norm, final_norm):
    batch, seq, d = x.shape
    depth = w_in.shape[0]
    assert d == D_MODEL and seq % MG_TM == 0 and w_in.shape[2] == sum(IN_SPLITS)
    x2 = x.reshape(batch * seq, d)
    pos_row = positions.astype(F32).reshape(batch * seq // SEQ_TILE, 1, SEQ_TILE)
    for l in range(depth):
        x2 = _layer(x2, pos_row, mem, l, l == depth - 1, batch, seq, w_in, b_igate, b_fgate,
                    conv_w, conv_b, mh_norm, cq_norm, w_uq, ckv_norm, w_ukv, mem_norm, w_mem_kv,
                    w_br_m, w_br_a, w_br_c, w_out, norm, final_norm)
    return x2.reshape(batch, seq, d)
```

```python
import functools
import math

import jax
import jax.numpy as jnp
from jax import lax
from jax.experimental import pallas as pl
from jax.experimental.pallas import tpu as pltpu

F32 = jnp.float32
BF16 = jnp.bfloat16

D_MODEL = 2048
M_HEADS, M_DH = 4, 256
M_W = M_HEADS * M_DH
CONV_K = 4
A_HEADS, A_NOPE, A_ROPE, A_DV = 8, 128, 64, 128
A_DQK = A_NOPE + A_ROPE
A_W = A_HEADS * A_DV
Q_LORA = KV_LORA = 512
ROPE_THETA = 10000.0
N_MEM = 256
C_HEADS, C_DH = 4, 256
C_W = C_HEADS * C_DH
EPS = 1e-6

LANES = 128
SUBLANES = 8
A_QK_PAD = 2 * LANES
VMEM_LIMIT = 56 * 1024 * 1024

SEQ_TILE = 512
HEADS_PER_STEP = 2
MLA_HEADS_PER_STEP = 4

IN_SPLITS = (2 * M_W, M_W, M_W, M_W, M_HEADS, M_HEADS, Q_LORA, KV_LORA, A_ROPE, A_W, C_W, C_W,
             3 * D_MODEL)


def _const_spec(shape):
    nd = len(shape)
    return pl.BlockSpec(shape, lambda *_: (0,) * nd, pipeline_mode=pl.Buffered(1))


IN_OFFS = [sum(IN_SPLITS[:i]) for i in range(len(IN_SPLITS) + 1)]

W_QK, W_V, W_O, W_Z, W_AZ = 0, 2 * M_W, 3 * M_W, 4 * M_W, 5 * M_W
W_CQ = W_AZ + A_W
W_CKV = W_CQ + Q_LORA
W_KR = W_CKV + KV_LORA
IF_LANE = A_ROPE // 2


def _w_rows(offset, rows):
    assert offset % rows == 0
    return pl.BlockSpec((rows, D_MODEL), lambda *_: (offset // rows, 0),
                        pipeline_mode=pl.Buffered(1))


def _w_in_rows(first, rows):
    assert first % SUBLANES == 0
    return pl.BlockSpec((pl.Element(rows), pl.Element(D_MODEL)), lambda *_: (first, 0),
                        pipeline_mode=pl.Buffered(1))


def _params(*sem):
    return pltpu.CompilerParams(dimension_semantics=sem, vmem_limit_bytes=VMEM_LIMIT)


def _dot(a, b):
    return jnp.dot(a, b, preferred_element_type=F32)


def _dot_nt(a, b):
    return lax.dot_general(a, b, (((1,), (1,)), ((), ())), preferred_element_type=F32)


def _rms(x, g):
    return x * lax.rsqrt(jnp.mean(x * x, axis=-1, keepdims=True) + EPS) * g


def _sigmoid(x):
    return 1.0 / (1.0 + jnp.exp(-x))


def _silu(x):
    return x * _sigmoid(x)


def _keys_le_queries(nk, nq):
    shape = (nk, nq)
    return lax.broadcasted_iota(jnp.int32, shape, 0) <= lax.broadcasted_iota(jnp.int32, shape, 1)


def _causal_pieces(qi, t, nk_full):
    half = t // 2
    return ([(k0, nk_full, 0, False) for k0 in range(0, qi * t, nk_full)]
            + [(qi * t, half, 0, True), (qi * t + half, half, half, True)])


def _two_phase(chains, first, second, ahead):
    pending = [first(*c) for c in chains[:ahead]]
    for n, chain in enumerate(chains):
        if n + ahead < len(chains):
            pending.append(first(*chains[n + ahead]))
        second(*chain, pending.pop(0))


def _left_pad(x, n, fill):
    if n == 0:
        return x
    return jnp.concatenate([jnp.full((x.shape[0], n), fill, x.dtype), x], axis=1)


MP_CH = 1024


def _mlstm_proj_kernel(tiles_per_seq, x_ref, ng_ref, wqk_ref, cw_ref, cb_ref, wvt_ref, wo_ref,
                       wz_ref, h_ref, qt_ref, k_ref, vt_ref, g_ref, xbuf_ref):
    tm = x_ref.shape[0]

    @pl.when(pl.program_id(0) % tiles_per_seq == 0)
    def _():
        xbuf_ref[...] = jnp.zeros_like(xbuf_ref)

    h = _rms(x_ref[...], ng_ref[...]).astype(BF16)
    h_ref[...] = h
    widen = lambda a: jnp.concatenate([a] * (tm // LANES), axis=1)

    def conv_chunk(c):
        fs = slice(c * MP_CH, (c + 1) * MP_CH)
        acc = _dot_nt(wqk_ref[fs, :], h)
        prev = xbuf_ref[fs, :]
        lane = lax.broadcasted_iota(jnp.int32, prev.shape, 1)
        y = widen(cb_ref[fs, :]) + widen(cw_ref[CONV_K - 1, fs, :]) * acc
        for j in range(CONV_K - 1):
            back = CONV_K - 1 - j
            rolled = pltpu.roll(acc, back, 1)
            head = jnp.where(lane < back, pltpu.roll(prev, back, 1), rolled[:, :LANES])
            tap = jnp.concatenate([head, rolled[:, LANES:]], axis=1)
            y = y + widen(cw_ref[j, fs, :]) * tap
        xbuf_ref[fs, :] = acc[:, tm - LANES:]
        y = _silu(y)
        if c < M_W // MP_CH:
            qt_ref[0, 0, fs, :] = y.astype(BF16)
        else:
            ks = slice(c * MP_CH - M_W, (c + 1) * MP_CH - M_W)
            k_ref[:, ks] = (y.T * (M_DH ** -0.5)).astype(BF16)

    def value_chunk(c):
        cs = slice(c * MP_CH, (c + 1) * MP_CH)
        vt_ref[0, 0, cs, :] = _dot_nt(wvt_ref[cs, :], h).astype(BF16)

    def gate_chunk(c):
        cs = slice(c * MP_CH, (c + 1) * MP_CH)
        o = _dot_nt(h, wo_ref[cs, :])
        z = _dot_nt(h, wz_ref[cs, :])
        g_ref[:, cs] = (_sigmoid(o) * _silu(z)).astype(BF16)

    light = [functools.partial(f, c) for c in range(M_W // MP_CH) for f in (value_chunk, gate_chunk)]
    for c in range(2 * M_W // MP_CH):
        conv_chunk(c)
        if c < len(light):
            light[c]()
    for f in light[2 * M_W // MP_CH:]:
        f()


def _mlstm_proj(x2, norm_g, wall, conv_w, conv_b, batch, seq):
    T = x2.shape[0]
    tm = SEQ_TILE
    per_seq = seq // tm
    row = lambda w: pl.BlockSpec((tm, w), lambda i: (i, 0))
    out_shape = (
        jax.ShapeDtypeStruct((T, D_MODEL), BF16),
        jax.ShapeDtypeStruct((batch, per_seq, M_W, tm), BF16),
        jax.ShapeDtypeStruct((T, M_W), BF16),
        jax.ShapeDtypeStruct((batch, per_seq, M_W, tm), BF16),
        jax.ShapeDtypeStruct((T, M_W), BF16),
    )
    vt_spec = pl.BlockSpec((1, 1, M_W, tm), lambda i: (i // per_seq, i % per_seq, 0, 0))
    return pl.pallas_call(
        functools.partial(_mlstm_proj_kernel, per_seq),
        grid=(T // tm,),
        in_specs=[row(D_MODEL), _const_spec((1, D_MODEL)), _w_rows(W_QK, 2 * M_W),
                  _const_spec(conv_w.shape), _const_spec(conv_b.shape), _w_rows(W_V, M_W),
                  _w_rows(W_O, M_W), _w_rows(W_Z, M_W)],
        out_specs=(row(D_MODEL), vt_spec, row(M_W), vt_spec, row(M_W)),
        out_shape=out_shape,
        scratch_shapes=[pltpu.VMEM((2 * M_W, LANES), F32)],
        compiler_params=_params("arbitrary"),
        name="mlstm_proj",
    )(x2, norm_g, wall, conv_w, conv_b, wall, wall, wall)


def _lane_scan(x, op, fill):
    n = x.shape[-1]
    lane = lax.broadcasted_iota(jnp.int32, x.shape, x.ndim - 1)
    d = 1
    while d < n:
        shifted = pltpu.roll(x, d, x.ndim - 1)
        x = op(x, jnp.where(lane >= d, shifted, fill))
        d *= 2
    return x


def _gate_scan_kernel(if_ref, bias_ref, row_ref, col_ref):
    n_seq, S = if_ref.shape[0], if_ref.shape[1]
    t = SEQ_TILE
    rows8 = lambda x, s: x[s * SUBLANES:(s + 1) * SUBLANES]
    t8 = jnp.concatenate([(if_ref[s] + bias_ref[...]).T[IF_LANE:IF_LANE + SUBLANES, :]
                          for s in range(n_seq)], axis=0)
    lf = jnp.minimum(t8, 0.0) - jnp.log(1.0 + jnp.exp(-jnp.abs(t8)))
    b_all = _lane_scan(lf, jnp.add, 0.0)
    b_all = jnp.concatenate([pltpu.roll(rows8(b_all, s), M_HEADS, 0) for s in range(n_seq)],
                            axis=0)
    a_all = t8 - b_all
    mx_all = jnp.maximum(_lane_scan(a_all, jnp.maximum, -jnp.inf), 0.0)
    nb_all = -b_all - mx_all
    sub = lax.broadcasted_iota(jnp.int32, (SUBLANES, S), 0)
    zeros = jnp.zeros((LANES - SUBLANES, S), F32)
    n_h = HEADS_PER_STEP
    for s in range(n_seq):
        a, mx, nb = rows8(a_all, s), rows8(mx_all, s), rows8(nb_all, s)
        for p in range(M_HEADS // n_h):
            up = (SUBLANES - n_h * p) % SUBLANES
            m_grp = pltpu.roll(mx, up, 0) if up else mx
            nb_grp = pltpu.roll(nb, (up + n_h) % SUBLANES, 0)
            stack = jnp.where(sub < n_h, m_grp, nb_grp)
            for j in range(S // t):
                row_ref[s, p, j] = stack[:, j * t:(j + 1) * t]
            a_grp = pltpu.roll(a, up, 0) if up else a
            col_ref[s, p] = jnp.concatenate([a_grp, zeros], axis=0).T


GS_SEQS = 2


def _gate_scan(pre_if, bias_row, batch, seq):
    n_grp = M_HEADS // HEADS_PER_STEP
    n_tile = seq // SEQ_TILE
    ns = GS_SEQS if batch % GS_SEQS == 0 else 1
    return pl.pallas_call(
        _gate_scan_kernel,
        grid=(batch // ns,),
        in_specs=[pl.BlockSpec((ns, seq, LANES), lambda b: (b, 0, 0)), _const_spec((1, LANES))],
        out_specs=(pl.BlockSpec((ns, n_grp, n_tile, SUBLANES, SEQ_TILE), lambda b: (b, 0, 0, 0, 0)),
                   pl.BlockSpec((ns, n_grp, seq, LANES), lambda b: (b, 0, 0, 0))),
        out_shape=(jax.ShapeDtypeStruct((batch, n_grp, n_tile, SUBLANES, SEQ_TILE), F32),
                   jax.ShapeDtypeStruct((batch, n_grp, seq, LANES), F32)),
        compiler_params=_params("parallel"),
        name="gate_scan",
    )(pre_if.reshape(batch, seq, LANES), bias_row)


def _decay_attn_kernel(qt_ref, k_ref, vt_ref, g_ref, row_ref, col_ref, gain_ref, o_ref,
                       arep_ref, fac_ref):
    S = k_ref.shape[1]
    t = SEQ_TILE
    n_head = HEADS_PER_STEP
    hs = [slice(hh * M_DH, (hh + 1) * M_DH) for hh in range(n_head)]
    lane_tiles = t // LANES
    widen = lambda a: jnp.concatenate([a] * lane_tiles, axis=1)

    c_rep = [[None] * (S // t) for _ in range(n_head)]
    for hh in range(n_head):
        arep_ref[hh] = jnp.broadcast_to(col_ref[0, 0, :, hh:hh + 1], (S, LANES))
        for kj in range(S // t - 1):
            a = arep_ref[hh, kj * t:(kj + 1) * t, :]
            c_rep[hh][kj] = jnp.max(a, axis=0, keepdims=True)
            fac_ref[hh, kj * t:(kj + 1) * t, :] = jnp.exp(a - c_rep[hh][kj])

    def scores(qi, hh):
        qt = qt_ref[0, qi, hs[hh], :]
        return [_dot(k_ref[0, k0:k0 + nk, hs[hh]], qt[:, q0:])
                for k0, nk, q0, _ in _causal_pieces(qi, t, t)]

    def weigh_pv(qi, hh, sts):
        qs = slice(qi * t, (qi + 1) * t)
        m_row = row_ref[0, 0, qi, hh:hh + 1, :]
        nb_row = row_ref[0, 0, qi, n_head + hh:n_head + hh + 1, :]
        den = num = None
        for st, (k0, nk, q0, diag) in zip(sts, _causal_pieces(qi, t, t)):
            ks = slice(k0, k0 + nk)
            kc, ko = divmod(k0, t)
            if diag:
                arg = widen(arep_ref[hh, ks, :])[:, q0:] - m_row[:, q0:]
                p = st * jnp.exp(jnp.where(_keys_le_queries(nk, t - q0), arg, -jnp.inf))
                d_blk = jnp.sum(p, axis=0, keepdims=True)
                n_blk = _dot(vt_ref[0, kc, hs[hh], ko:ko + nk], p.astype(BF16))
            else:
                p = st * widen(fac_ref[hh, ks, :])
                qfac = jnp.exp(widen(c_rep[hh][kc]) - m_row)
                d_blk = qfac * jnp.sum(p, axis=0, keepdims=True)
                n_blk = qfac * _dot(vt_ref[0, kc, hs[hh], ko:ko + nk], p.astype(BF16))
            d_blk, n_blk = _left_pad(d_blk, q0, 0.0), _left_pad(n_blk, q0, 0.0)
            den, num = (d_blk, n_blk) if den is None else (den + d_blk, num + n_blk)
        hv = num / jnp.maximum(jnp.abs(den), jnp.exp(nb_row))
        hv = hv * lax.rsqrt(jnp.mean(hv * hv, axis=0, keepdims=True) + EPS)
        o_ref[0, qs, hs[hh]] = (hv.T * gain_ref[:, hs[hh]]
                                * g_ref[0, qs, hs[hh]].astype(F32)).astype(BF16)

    chains = [(qi, hh) for qi in reversed(range(S // t)) for hh in range(n_head)]
    _two_phase(chains, scores, weigh_pv, ahead=1)


def _decay_attn(qt, k, vt, gate, rowf, colf, gain, batch, seq):
    n_grp = M_HEADS // HEADS_PER_STEP
    n_tile = seq // SEQ_TILE
    w = HEADS_PER_STEP * M_DH
    blk = pl.BlockSpec((1, seq, w), lambda b, p: (b, 0, p))
    t_blk = pl.BlockSpec((1, n_tile, w, SEQ_TILE), lambda b, p: (b, 0, p, 0))
    k3, g3 = (a.reshape(batch, seq, M_W) for a in (k, gate))
    return pl.pallas_call(
        _decay_attn_kernel,
        grid=(batch, n_grp),
        in_specs=[t_blk, blk, t_blk, blk,
                  pl.BlockSpec((1, 1, n_tile, SUBLANES, SEQ_TILE), lambda b, p: (b, p, 0, 0, 0)),
                  pl.BlockSpec((1, 1, seq, LANES), lambda b, p: (b, p, 0, 0)),
                  pl.BlockSpec((1, w), lambda b, p: (0, p))],
        out_specs=blk,
        out_shape=jax.ShapeDtypeStruct((batch, seq, M_W), BF16),
        scratch_shapes=[pltpu.VMEM((HEADS_PER_STEP, seq, LANES), F32),
                        pltpu.VMEM((HEADS_PER_STEP, seq - SEQ_TILE, LANES), F32)],
        compiler_params=_params("parallel", "parallel"),
        name="decay_attn",
    )(qt, k3, vt, g3, rowf, colf, gain).reshape(batch * seq, M_W)


def _mla_proj_kernel(h_ref, pos_ref, invf_ref, gq_ref, gkv_ref, wcq_ref, wckv_ref,
                     wkr_ref, waz_ref, wqn_ref, wqr_ref, wkn_ref, wvt_ref,
                     q_ref, k_ref, vt_ref, g_ref, if_ref):
    h = h_ref[...]
    tm = h.shape[0]
    scale = math.log2(math.e) / math.sqrt(A_DQK)
    ang = invf_ref[...] * pos_ref[0]
    c32, s32 = jnp.cos(ang), jnp.sin(ang)
    z32 = jnp.zeros((A_ROPE // 2, tm), F32)
    cos = jnp.concatenate([c32, z32, c32, z32], axis=0).T
    sin = jnp.concatenate([-s32, z32, s32, z32], axis=0).T
    cos_pair = jnp.concatenate([c32, c32, c32, c32], axis=0).T
    sin_pair = jnp.concatenate([-s32, -s32, s32, s32], axis=0).T

    def rope(r, cos=cos, sin=sin):
        return r * cos + pltpu.roll(r, LANES // 2, 1) * sin

    cq = _dot_nt(h, wcq_ref[...])
    ckv = _dot_nt(h, wckv_ref[...])
    g_ref[...] = _silu(_dot_nt(h, waz_ref[...])).astype(BF16)
    kr_if = _dot_nt(h, wkr_ref[...])
    if_ref[...] = kr_if
    kr = rope(kr_if)
    kr_by_parity = (kr.astype(BF16), pltpu.roll(kr, A_ROPE // 2, 1).astype(BF16))

    cqn = _rms(cq, gq_ref[...]).astype(BF16)
    ckvn = _rms(ckv, gkv_ref[...]).astype(BF16)
    qn = _dot(cqn, wqn_ref[...]) * scale
    qr = _dot(cqn, wqr_ref[...]) * scale
    kn = _dot(ckvn, wkn_ref[...])
    for hd in range(A_HEADS):
        base = hd * A_QK_PAD
        ls = slice(hd * LANES, (hd + 1) * LANES)
        pair = slice((hd // 2) * LANES, (hd // 2 + 1) * LANES)
        q_ref[:, base:base + LANES] = qn[:, ls].astype(BF16)
        q_ref[:, base + LANES:base + A_QK_PAD] = rope(qr[:, pair], cos_pair, sin_pair).astype(BF16)
        k_ref[:, base:base + LANES] = kn[:, ls].astype(BF16)
        k_ref[:, base + LANES:base + A_QK_PAD] = kr_by_parity[hd % 2]
    vt_ref[0, 0] = _dot_nt(wvt_ref[...], ckvn).astype(BF16)


def _mla_proj(h, pos_row, invf, gq, gkv, wall, wqn, wqr, wkn, wvt, batch, seq):
    T = h.shape[0]
    tm = SEQ_TILE
    per_seq = seq // tm
    row = lambda w: pl.BlockSpec((tm, w), lambda i: (i, 0))
    small = (invf, gq, gkv)
    ups = (wqn, wqr, wkn, wvt)
    vt_spec = pl.BlockSpec((1, 1, A_W, tm), lambda i: (i // per_seq, i % per_seq, 0, 0))
    return pl.pallas_call(
        _mla_proj_kernel,
        grid=(T // tm,),
        in_specs=([row(D_MODEL), pl.BlockSpec((1, 1, tm), lambda i: (i, 0, 0))]
                  + [_const_spec(c.shape) for c in small]
                  + [_w_rows(W_CQ, Q_LORA), _w_rows(W_CKV, KV_LORA), _w_rows(W_KR, LANES),
                     _w_rows(W_AZ, A_W)]
                  + [_const_spec(c.shape) for c in ups]),
        out_specs=(row(A_HEADS * A_QK_PAD), row(A_HEADS * A_QK_PAD), vt_spec, row(A_W),
                   row(LANES)),
        out_shape=(jax.ShapeDtypeStruct((T, A_HEADS * A_QK_PAD), BF16),
                   jax.ShapeDtypeStruct((T, A_HEADS * A_QK_PAD), BF16),
                   jax.ShapeDtypeStruct((batch, per_seq, A_W, tm), BF16),
                   jax.ShapeDtypeStruct((T, A_W), BF16),
                   jax.ShapeDtypeStruct((T, LANES), F32)),
        compiler_params=_params("parallel"),
        name="mla_proj",
    )(h, pos_row, *small, wall, wall, wall, wall, *ups)


def _mla_attn_kernel(q_ref, k_ref, vt_ref, g_ref, o_ref):
    S = q_ref.shape[1]
    t = SEQ_TILE
    n_head = MLA_HEADS_PER_STEP
    hq = [slice(hh * A_QK_PAD, (hh + 1) * A_QK_PAD) for hh in range(n_head)]
    hv = [slice(hh * A_DV, (hh + 1) * A_DV) for hh in range(n_head)]

    def scores(qi, hh):
        q = q_ref[0, qi * t:(qi + 1) * t, hq[hh]]
        st = []
        for k0, nk, q0, diag in _causal_pieces(qi, t, t):
            s = _dot_nt(k_ref[0, k0:k0 + nk, hq[hh]], q[q0:, :])
            st.append(jnp.where(_keys_le_queries(nk, t - q0), s, -jnp.inf) if diag else s)
        return st

    def softmax_pv(qi, hh, st):
        qs = slice(qi * t, (qi + 1) * t)
        pieces = _causal_pieces(qi, t, t)
        m = functools.reduce(jnp.maximum, [
            _left_pad(jnp.max(s, axis=0, keepdims=True), q0, -jnp.inf)
            for s, (_, _, q0, _) in zip(st, pieces)])
        l = acc = None
        for s, (k0, nk, q0, _) in zip(st, pieces):
            p = jnp.exp2(s - m[:, q0:])
            l_blk = _left_pad(jnp.sum(p, axis=0, keepdims=True), q0, 0.0)
            kc, ko = divmod(k0, t)
            pv = _left_pad(_dot(vt_ref[0, kc, hv[hh], ko:ko + nk], p.astype(BF16)), q0, 0.0)
            l, acc = (l_blk, pv) if l is None else (l + l_blk, acc + pv)
        o_ref[0, qs, hv[hh]] = ((acc / l).T * g_ref[0, qs, hv[hh]].astype(F32)).astype(BF16)

    chains = [(qi, hh) for qi in reversed(range(S // t)) for hh in range(n_head)]
    _two_phase(chains, scores, softmax_pv, ahead=2)


def _mla_attn(q, k, vt, gate, batch, seq):
    n_tile = seq // SEQ_TILE
    n_h = MLA_HEADS_PER_STEP
    qk_blk = pl.BlockSpec((1, seq, n_h * A_QK_PAD), lambda b, p: (b, 0, p))
    v_blk = pl.BlockSpec((1, seq, n_h * A_DV), lambda b, p: (b, 0, p))
    vt_blk = pl.BlockSpec((1, n_tile, n_h * A_DV, SEQ_TILE), lambda b, p: (b, 0, p, 0))
    q3 = q.reshape(batch, seq, A_HEADS * A_QK_PAD)
    k3 = k.reshape(batch, seq, A_HEADS * A_QK_PAD)
    g3 = gate.reshape(batch, seq, A_W)
    return pl.pallas_call(
        _mla_attn_kernel,
        grid=(batch, A_HEADS // n_h),
        in_specs=[qk_blk, qk_blk, vt_blk, v_blk],
        out_specs=v_blk,
        out_shape=jax.ShapeDtypeStruct((batch, seq, A_W), BF16),
        compiler_params=_params("parallel", "parallel"),
        name="mla_attn",
    )(q3, k3, vt, g3).reshape(batch * seq, A_W)


def _cast_once(w_ref, wbf_ref):
    @pl.when(pl.program_id(0) == 0)
    def _():
        wbf_ref[...] = w_ref[...].astype(BF16)


def _mem_kv_kernel(mem_ref, g_ref, w_ref, kv_ref, wbf_ref):
    _cast_once(w_ref, wbf_ref)
    nb, nm, d = mem_ref.shape
    m = _rms(mem_ref[...].reshape(nb * nm, d), g_ref[...]).astype(BF16)
    kv_ref[...] = _dot(m, wbf_ref[...]).astype(BF16).reshape(nb, nm, 2 * C_W)


MKV_BATCH = 4


def _mem_kv(mem, gain, w):
    batch = mem.shape[0]
    nb = MKV_BATCH if batch % MKV_BATCH == 0 else 1
    return pl.pallas_call(
        _mem_kv_kernel,
        grid=(batch // nb,),
        in_specs=[pl.BlockSpec((nb, N_MEM, D_MODEL), lambda b: (b, 0, 0)),
                  _const_spec((1, D_MODEL)), _const_spec(w.shape)],
        out_specs=pl.BlockSpec((nb, N_MEM, 2 * C_W), lambda b: (b, 0, 0)),
        out_shape=jax.ShapeDtypeStruct((batch, N_MEM, 2 * C_W), BF16),
        scratch_shapes=[pltpu.VMEM(w.shape, BF16)],
        compiler_params=_params("arbitrary"),
        name="mem_kv",
    )(mem, gain, w)


def _mem_attn_kernel(h_ref, kv_ref, w_ref, o_ref, wbf_ref):
    _cast_once(w_ref, wbf_ref)
    h = h_ref[...]
    cq = (_dot_nt(h, wbf_ref[0:C_W, :]) * (C_DH ** -0.5)).astype(BF16)
    heads = [slice(hd * C_DH, (hd + 1) * C_DH) for hd in range(C_HEADS)]

    def matmuls(hs):
        gate_rows = slice(C_W + hs.start, C_W + hs.stop)
        return _dot_nt(cq[:, hs], kv_ref[0, :, hs]), _dot_nt(h, wbf_ref[gate_rows, :])

    s, cz = matmuls(heads[0])
    for hd, hs in enumerate(heads):
        nxt = matmuls(heads[hd + 1]) if hd + 1 < C_HEADS else None
        p = jnp.exp(s - jnp.max(s, axis=-1, keepdims=True))
        l = jnp.sum(p, axis=-1, keepdims=True)
        o = _dot(p.astype(BF16), kv_ref[0, :, C_W + hd * C_DH:C_W + (hd + 1) * C_DH]) / l
        o_ref[:, hs] = (o * _silu(cz)).astype(BF16)
        if nxt is not None:
            s, cz = nxt


MA_TM = 1024


def _mem_attn(h, kv, wt, seq):
    T = h.shape[0]
    tm = MA_TM
    per_seq = seq // tm
    return pl.pallas_call(
        _mem_attn_kernel,
        grid=(T // tm,),
        in_specs=[pl.BlockSpec((tm, D_MODEL), lambda i: (i, 0)),
                  pl.BlockSpec((1, N_MEM, 2 * C_W), lambda i: (i // per_seq, 0, 0)),
                  _w_in_rows(IN_OFFS[10], 2 * C_W)],
        out_specs=pl.BlockSpec((tm, C_W), lambda i: (i, 0)),
        out_shape=jax.ShapeDtypeStruct((T, C_W), BF16),
        scratch_shapes=[pltpu.VMEM((2 * C_W, D_MODEL), BF16)],
        compiler_params=_params("arbitrary"),
        name="mem_attn",
    )(h, kv, wt)


MG_TM = 1024
MG_TN = 512


def _merge_kernel(h_ref, hm_ref, ha_ref, hc_ref, wgm_ref, wga_ref, wgc_ref,
                  wbm_ref, wba_ref, wbc_ref, o_ref):
    h = h_ref[...]
    gate = lambda w_ref: _sigmoid(_dot_nt(h, w_ref[...]))
    branch = lambda a_ref, w_ref: _dot(a_ref[...], w_ref[...])
    acc = gate(wgm_ref) * branch(hm_ref, wbm_ref)
    acc = acc + gate(wga_ref) * branch(ha_ref, wba_ref)
    acc = acc + gate(wgc_ref) * branch(hc_ref, wbc_ref)
    o_ref[...] = acc.astype(BF16)


def _merge(h, hm, ha, hc, wt, wbm, wba, wbc):
    T = h.shape[0]
    tm, tn = MG_TM, MG_TN
    row = lambda w: pl.BlockSpec((tm, w), lambda i, j: (i, 0))
    col = lambda kdim: pl.BlockSpec((kdim, tn), lambda i, j: (0, j))

    def gate(branch):
        first = (W_GATE + branch * D_MODEL) // tn
        return pl.BlockSpec((tn, D_MODEL), lambda i, j: (first + j, 0))

    return pl.pallas_call(
        _merge_kernel,
        grid=(T // tm, D_MODEL // tn),
        in_specs=[row(D_MODEL), row(M_W), row(A_W), row(C_W),
                  gate(0), gate(1), gate(2), col(M_W), col(A_W), col(C_W)],
        out_specs=pl.BlockSpec((tm, tn), lambda i, j: (i, j)),
        out_shape=jax.ShapeDtypeStruct((T, D_MODEL), BF16),
        compiler_params=_params("parallel", "arbitrary"),
        name="merge",
    )(h, hm, ha, hc, wt, wt, wt, wbm, wba, wbc)


def _out_proj_kernel(final, x_ref, m_ref, w_ref, g_ref, o_ref, wbf_ref):
    _cast_once(w_ref, wbf_ref)
    y = x_ref[...] + _dot(m_ref[...], wbf_ref[...])
    o_ref[...] = _rms(y, g_ref[...]) if final else y


def _out_proj(x2, merged, w, gain, final):
    T = x2.shape[0]
    tm = SEQ_TILE
    row = pl.BlockSpec((tm, D_MODEL), lambda i: (i, 0))
    return pl.pallas_call(
        functools.partial(_out_proj_kernel, final),
        grid=(T // tm,),
        in_specs=[row, row, _const_spec(w.shape), _const_spec((1, D_MODEL))],
        out_specs=row,
        out_shape=jax.ShapeDtypeStruct((T, D_MODEL), F32),
        scratch_shapes=[pltpu.VMEM(w.shape, BF16)],
        compiler_params=_params("arbitrary"),
        name="out_proj",
    )(x2, merged, w, gain)


PK_ROWS = 1024
W_GATE = W_KR + PK_ROWS
W_ROWS = W_GATE + 3 * D_MODEL
PK_SRC = ([b * PK_ROWS for b in range(IN_OFFS[4] // PK_ROWS)]
          + [IN_OFFS[9], IN_OFFS[6], IN_OFFS[8]]
          + [IN_OFFS[12] + b * PK_ROWS for b in range(3 * D_MODEL // PK_ROWS)])


def _pack_w_kernel(w_ref, wif_ref, o_ref):
    b = pl.program_id(0)
    x = w_ref[...]
    o_ref[...] = x.astype(BF16)

    @pl.when(b == W_KR // PK_ROWS)
    def _():
        half = A_ROPE // 2
        n_if = 2 * M_HEADS
        zeros = lambda n: jnp.zeros((n, D_MODEL), F32)
        o_ref[...] = jnp.concatenate(
            [x[:half], wif_ref[...], zeros(half - n_if), x[half:2 * half], zeros(half),
             zeros(PK_ROWS - 4 * half)], axis=0).astype(BF16)


def _pack_w_in(wt):
    assert len(PK_SRC) * PK_ROWS == W_ROWS and all(s % SUBLANES == 0 for s in PK_SRC)

    def src(b):
        first = sum(jnp.where(b == i, s, 0) for i, s in enumerate(PK_SRC))
        return pl.multiple_of(first, SUBLANES), 0

    return pl.pallas_call(
        _pack_w_kernel,
        grid=(len(PK_SRC),),
        in_specs=[pl.BlockSpec((pl.Element(PK_ROWS), pl.Element(D_MODEL)), src),
                  pl.BlockSpec((pl.Element(2 * M_HEADS), pl.Element(D_MODEL)),
                               lambda b: (IN_OFFS[4], 0))],
        out_specs=pl.BlockSpec((PK_ROWS, D_MODEL), lambda b: (b, 0)),
        out_shape=jax.ShapeDtypeStruct((W_ROWS, D_MODEL), BF16),
        compiler_params=_params("parallel"),
        name="pack_w",
    )(wt, wt)


def _layer(x2, pos_row, kv_mem_in, l, final, batch, seq, w_in, b_igate, b_fgate, conv_w, conv_b,
           mh_norm, cq_norm, w_uq, ckv_norm, w_ukv, mem_norm, w_mem_kv, w_br_m, w_br_a, w_br_c,
           w_out, norm, final_norm):
    bf = lambda a: a.astype(BF16)
    row = lambda a: a.reshape(1, -1).astype(F32)

    wt = w_in[l].T
    wall = _pack_w_in(wt)
    uq = w_uq[l].reshape(Q_LORA, A_HEADS, A_DQK)
    wqn = bf(uq[:, :, :A_NOPE].reshape(Q_LORA, A_HEADS * A_NOPE))
    half = A_ROPE // 2
    wqr = bf(uq[:, :, A_NOPE:].reshape(Q_LORA, A_HEADS // 2, 2, 2, half)
             .transpose(0, 1, 3, 2, 4).reshape(Q_LORA, (A_HEADS // 2) * LANES))
    ukv = w_ukv[l].reshape(KV_LORA, A_HEADS, A_NOPE + A_DV)
    wkn = bf(ukv[:, :, :A_NOPE].reshape(KV_LORA, A_HEADS * A_NOPE))
    wvvt = bf(ukv[:, :, A_NOPE:].reshape(KV_LORA, A_HEADS * A_DV).T)
    bias_row = jnp.pad(jnp.concatenate([b_igate[l], b_fgate[l]]).astype(F32),
                       (IF_LANE, LANES - IF_LANE - 2 * M_HEADS)).reshape(1, LANES)
    invf = (ROPE_THETA ** (-jnp.arange(0, A_ROPE, 2, dtype=F32) / A_ROPE)).reshape(-1, 1)

    lane_rep = lambda a: jnp.broadcast_to(a.astype(F32)[..., None], a.shape + (LANES,))
    h, mqt, mk, mvt, mgate = _mlstm_proj(x2, row(norm[l]), wall, lane_rep(conv_w[l]),
                                         lane_rep(conv_b[l]), batch, seq)
    aq, ak, avt, agate, pre_if = _mla_proj(h, pos_row, invf, row(cq_norm[l]), row(ckv_norm[l]),
                                           wall, wqn, wqr, wkn, wvvt, batch, seq)
    rowf, colf = _gate_scan(pre_if, bias_row, batch, seq)
    hm = _decay_attn(mqt, mk, mvt, mgate, rowf, colf, row(mh_norm[l]), batch, seq)
    ha = _mla_attn(aq, ak, avt, agate, batch, seq)

    kv_mem = _mem_kv(kv_mem_in, row(mem_norm[l]), w_mem_kv[l])
    hc = _mem_attn(h, kv_mem, wt, seq)

    merged = _merge(h, hm, ha, hc, wall, bf(w_br_m[l]), bf(w_br_a[l]), bf(w_br_c[l]))
    return _out_proj(x2, merged, w_out[l], row(final_norm), final)


def kernel(x, mem, positions, w_in, b_igate, b_fgate, conv_w, conv_b, mh_norm, cq_norm, w_uq,
           ckv_norm, w_ukv, mem_norm, w_mem_kv, w_br_m, w_br_a, w_br_c, w_out, norm, final_norm):
    batch, seq, d = x.shape
    depth = w_in.shape[0]
    assert d == D_MODEL and seq % MG_TM == 0 and w_in.shape[2] == sum(IN_SPLITS)
    x2 = x.reshape(batch * seq, d)
    pos_row = positions.astype(F32).reshape(batch * seq // SEQ_TILE, 1, SEQ_TILE)
    for l in range(depth):
        x2 = _layer(x2, pos_row, mem, l, l == depth - 1, batch, seq, w_in, b_igate, b_fgate,
                    conv_w, conv_b, mh_norm, cq_norm, w_uq, ckv_norm, w_ukv, mem_norm, w_mem_kv,
                    w_br_m, w_br_a, w_br_c, w_out, norm, final_norm)
    return x2.reshape(batch, seq, d)
```

```python
import functools
import math

import jax
import jax.numpy as jnp
from jax import lax
from jax.experimental import pallas as pl
from jax.experimental.pallas import tpu as pltpu

F32 = jnp.float32
BF16 = jnp.bfloat16

D_MODEL = 2048
M_HEADS, M_DH = 4, 256
M_W = M_HEADS * M_DH
CONV_K = 4
A_HEADS, A_NOPE, A_ROPE, A_DV = 8, 128, 64, 128
A_DQK = A_NOPE + A_ROPE
A_W = A_HEADS * A_DV
Q_LORA = KV_LORA = 512
ROPE_THETA = 10000.0
N_MEM = 256
C_HEADS, C_DH = 4, 256
C_W = C_HEADS * C_DH
EPS = 1e-6

LANES = 128
SUBLANES = 8
A_QK_PAD = 2 * LANES
VMEM_LIMIT = 56 * 1024 * 1024

SEQ_TILE = 512
HEADS_PER_STEP = 2
MLA_HEADS_PER_STEP = 4

IN_SPLITS = (2 * M_W, M_W, M_W, M_W, M_HEADS, M_HEADS, Q_LORA, KV_LORA, A_ROPE, A_W, C_W, C_W,
             3 * D_MODEL)


def _const_spec(shape):
    nd = len(shape)
    return pl.BlockSpec(shape, lambda *_: (0,) * nd, pipeline_mode=pl.Buffered(1))


IN_OFFS = [sum(IN_SPLITS[:i]) for i in range(len(IN_SPLITS) + 1)]

W_QK, W_V, W_O, W_Z, W_AZ = 0, 2 * M_W, 3 * M_W, 4 * M_W, 5 * M_W
W_CQ = W_AZ + A_W
W_CKV = W_CQ + Q_LORA
W_KR = W_CKV + KV_LORA
IF_LANE = A_ROPE // 2


def _w_rows(offset, rows):
    assert offset % rows == 0
    return pl.BlockSpec((rows, D_MODEL), lambda *_: (offset // rows, 0),
                        pipeline_mode=pl.Buffered(1))


def _w_in_rows(first, rows):
    assert first % SUBLANES == 0
    return pl.BlockSpec((pl.Element(rows), pl.Element(D_MODEL)), lambda *_: (first, 0),
                        pipeline_mode=pl.Buffered(1))


def _params(*sem):
    return pltpu.CompilerParams(dimension_semantics=sem, vmem_limit_bytes=VMEM_LIMIT)


def _dot(a, b):
    return jnp.dot(a, b, preferred_element_type=F32)


def _dot_nt(a, b):
    return lax.dot_general(a, b, (((1,), (1,)), ((), ())), preferred_element_type=F32)


def _rms(x, g):
    return x * lax.rsqrt(jnp.mean(x * x, axis=-1, keepdims=True) + EPS) * g


def _sigmoid(x):
    return 1.0 / (1.0 + jnp.exp(-x))


def _silu(x):
    return x * _sigmoid(x)


def _keys_le_queries(nk, nq):
    shape = (nk, nq)
    return lax.broadcasted_iota(jnp.int32, shape, 0) <= lax.broadcasted_iota(jnp.int32, shape, 1)


def _causal_pieces(qi, t, nk_full):
    half = t // 2
    return ([(k0, nk_full, 0, False) for k0 in range(0, qi * t, nk_full)]
            + [(qi * t, half, 0, True), (qi * t + half, half, half, True)])


def _two_phase(chains, first, second, ahead):
    pending = [first(*c) for c in chains[:ahead]]
    for n, chain in enumerate(chains):
        if n + ahead < len(chains):
            pending.append(first(*chains[n + ahead]))
        second(*chain, pending.pop(0))


def _left_pad(x, n, fill):
    if n == 0:
        return x
    return jnp.concatenate([jnp.full((x.shape[0], n), fill, x.dtype), x], axis=1)


MP_CH = 1024


def _mlstm_proj_kernel(tiles_per_seq, x_ref, ng_ref, wqk_ref, cw_ref, cb_ref, wvt_ref, wo_ref,
                       wz_ref, h_ref, qt_ref, k_ref, vt_ref, g_ref, xbuf_ref):
    tm = x_ref.shape[0]

    @pl.when(pl.program_id(0) % tiles_per_seq == 0)
    def _():
        xbuf_ref[...] = jnp.zeros_like(xbuf_ref)

    h = _rms(x_ref[...], ng_ref[...]).astype(BF16)
    h_ref[...] = h
    widen = lambda a: jnp.concatenate([a] * (tm // LANES), axis=1)

    def conv_chunk(c):
        fs = slice(c * MP_CH, (c + 1) * MP_CH)
        acc = _dot_nt(wqk_ref[fs, :], h)
        prev = xbuf_ref[fs, :]
        lane = lax.broadcasted_iota(jnp.int32, prev.shape, 1)
        y = widen(cb_ref[fs, :]) + widen(cw_ref[CONV_K - 1, fs, :]) * acc
        for j in range(CONV_K - 1):
            back = CONV_K - 1 - j
            rolled = pltpu.roll(acc, back, 1)
            head = jnp.where(lane < back, pltpu.roll(prev, back, 1), rolled[:, :LANES])
            tap = jnp.concatenate([head, rolled[:, LANES:]], axis=1)
            y = y + widen(cw_ref[j, fs, :]) * tap
        xbuf_ref[fs, :] = acc[:, tm - LANES:]
        y = _silu(y)
        if c < M_W // MP_CH:
            qt_ref[0, 0, fs, :] = y.astype(BF16)
        else:
            ks = slice(c * MP_CH - M_W, (c + 1) * MP_CH - M_W)
            k_ref[:, ks] = (y.T * (M_DH ** -0.5)).astype(BF16)

    def value_chunk(c):
        cs = slice(c * MP_CH, (c + 1) * MP_CH)
        vt_ref[0, 0, cs, :] = _dot_nt(wvt_ref[cs, :], h).astype(BF16)

    def gate_chunk(c):
        cs = slice(c * MP_CH, (c + 1) * MP_CH)
        o = _dot_nt(h, wo_ref[cs, :])
        z = _dot_nt(h, wz_ref[cs, :])
        g_ref[:, cs] = (_sigmoid(o) * _silu(z)).astype(BF16)

    light = [functools.partial(f, c) for c in range(M_W // MP_CH) for f in (value_chunk, gate_chunk)]
    for c in range(2 * M_W // MP_CH):
        conv_chunk(c)
        if c < len(light):
            light[c]()
    for f in light[2 * M_W // MP_CH:]:
        f()


def _mlstm_proj(x2, norm_g, wall, conv_w, conv_b, batch, seq):
    T = x2.shape[0]
    tm = SEQ_TILE
    per_seq = seq // tm
    row = lambda w: pl.BlockSpec((tm, w), lambda i: (i, 0))
    out_shape = (
        jax.ShapeDtypeStruct((T, D_MODEL), BF16),
        jax.ShapeDtypeStruct((batch, per_seq, M_W, tm), BF16),
        jax.ShapeDtypeStruct((T, M_W), BF16),
        jax.ShapeDtypeStruct((batch, per_seq, M_W, tm), BF16),
        jax.ShapeDtypeStruct((T, M_W), BF16),
    )
    vt_spec = pl.BlockSpec((1, 1, M_W, tm), lambda i: (i // per_seq, i % per_seq, 0, 0))
    return pl.pallas_call(
        functools.partial(_mlstm_proj_kernel, per_seq),
        grid=(T // tm,),
        in_specs=[row(D_MODEL), _const_spec((1, D_MODEL)), _w_rows(W_QK, 2 * M_W),
                  _const_spec(conv_w.shape), _const_spec(conv_b.shape), _w_rows(W_V, M_W),
                  _w_rows(W_O, M_W), _w_rows(W_Z, M_W)],
        out_specs=(row(D_MODEL), vt_spec, row(M_W), vt_spec, row(M_W)),
        out_shape=out_shape,
        scratch_shapes=[pltpu.VMEM((2 * M_W, LANES), F32)],
        compiler_params=_params("arbitrary"),
        name="mlstm_proj",
    )(x2, norm_g, wall, conv_w, conv_b, wall, wall, wall)


def _lane_scan(x, op, fill):
    n = x.shape[-1]
    lane = lax.broadcasted_iota(jnp.int32, x.shape, x.ndim - 1)
    d = 1
    while d < n:
        shifted = pltpu.roll(x, d, x.ndim - 1)
        x = op(x, jnp.where(lane >= d, shifted, fill))
        d *= 2
    return x


def _gate_scan_kernel(if_ref, bias_ref, row_ref, col_ref):
    n_seq, S = if_ref.shape[0], if_ref.shape[1]
    t = SEQ_TILE
    rows8 = lambda x, s: x[s * SUBLANES:(s + 1) * SUBLANES]
    t8 = jnp.concatenate([(if_ref[s] + bias_ref[...]).T[IF_LANE:IF_LANE + SUBLANES, :]
                          for s in range(n_seq)], axis=0)
    lf = jnp.minimum(t8, 0.0) - jnp.log(1.0 + jnp.exp(-jnp.abs(t8)))
    b_all = _lane_scan(lf, jnp.add, 0.0)
    b_all = jnp.concatenate([pltpu.roll(rows8(b_all, s), M_HEADS, 0) for s in range(n_seq)],
                            axis=0)
    a_all = t8 - b_all
    mx_all = jnp.maximum(_lane_scan(a_all, jnp.maximum, -jnp.inf), 0.0)
    nb_all = -b_all - mx_all
    sub = lax.broadcasted_iota(jnp.int32, (SUBLANES, S), 0)
    zeros = jnp.zeros((LANES - SUBLANES, S), F32)
    n_h = HEADS_PER_STEP
    for s in range(n_seq):
        a, mx, nb = rows8(a_all, s), rows8(mx_all, s), rows8(nb_all, s)
        for p in range(M_HEADS // n_h):
            up = (SUBLANES - n_h * p) % SUBLANES
            m_grp = pltpu.roll(mx, up, 0) if up else mx
            nb_grp = pltpu.roll(nb, (up + n_h) % SUBLANES, 0)
            stack = jnp.where(sub < n_h, m_grp, nb_grp)
            for j in range(S // t):
                row_ref[s, p, j] = stack[:, j * t:(j + 1) * t]
            a_grp = pltpu.roll(a, up, 0) if up else a
            col_ref[s, p] = jnp.concatenate([a_grp, zeros], axis=0).T


GS_SEQS = 2


def _gate_scan(pre_if, bias_row, batch, seq):
    n_grp = M_HEADS // HEADS_PER_STEP
    n_tile = seq // SEQ_TILE
    ns = GS_SEQS if batch % GS_SEQS == 0 else 1
    return pl.pallas_call(
        _gate_scan_kernel,
        grid=(batch // ns,),
        in_specs=[pl.BlockSpec((ns, seq, LANES), lambda b: (b, 0, 0)), _const_spec((1, LANES))],
        out_specs=(pl.BlockSpec((ns, n_grp, n_tile, SUBLANES, SEQ_TILE), lambda b: (b, 0, 0, 0, 0)),
                   pl.BlockSpec((ns, n_grp, seq, LANES), lambda b: (b, 0, 0, 0))),
        out_shape=(jax.ShapeDtypeStruct((batch, n_grp, n_tile, SUBLANES, SEQ_TILE), F32),
                   jax.ShapeDtypeStruct((batch, n_grp, seq, LANES), F32)),
        compiler_params=_params("parallel"),
        name="gate_scan",
    )(pre_if.reshape(batch, seq, LANES), bias_row)


def _decay_attn_kernel(qt_ref, k_ref, vt_ref, g_ref, row_ref, col_ref, gain_ref, o_ref,
                       arep_ref, fac_ref):
    S = k_ref.shape[1]
    t = SEQ_TILE
    n_head = HEADS_PER_STEP
    hs = [slice(hh * M_DH, (hh + 1) * M_DH) for hh in range(n_head)]
    lane_tiles = t // LANES
    widen = lambda a: jnp.concatenate([a] * lane_tiles, axis=1)

    c_rep = [[None] * (S // t) for _ in range(n_head)]
    for hh in range(n_head):
        arep_ref[hh] = jnp.broadcast_to(col_ref[0, 0, :, hh:hh + 1], (S, LANES))
        for kj in range(S // t - 1):
            a = arep_ref[hh, kj * t:(kj + 1) * t, :]
            c_rep[hh][kj] = jnp.max(a, axis=0, keepdims=True)
            fac_ref[hh, kj * t:(kj + 1) * t, :] = jnp.exp(a - c_rep[hh][kj])

    def scores(qi, hh):
        qt = qt_ref[0, qi, hs[hh], :]
        return [_dot(k_ref[0, k0:k0 + nk, hs[hh]], qt[:, q0:])
                for k0, nk, q0, _ in _causal_pieces(qi, t, t)]

    def weigh_pv(qi, hh, sts):
        qs = slice(qi * t, (qi + 1) * t)
        m_row = row_ref[0, 0, qi, hh:hh + 1, :]
        nb_row = row_ref[0, 0, qi, n_head + hh:n_head + hh + 1, :]
        den = num = None
        for st, (k0, nk, q0, diag) in zip(sts, _causal_pieces(qi, t, t)):
            ks = slice(k0, k0 + nk)
            kc, ko = divmod(k0, t)
            if diag:
                arg = widen(arep_ref[hh, ks, :])[:, q0:] - m_row[:, q0:]
                p = st * jnp.exp(jnp.where(_keys_le_queries(nk, t - q0), arg, -jnp.inf))
                d_blk = jnp.sum(p, axis=0, keepdims=True)
                n_blk = _dot(vt_ref[0, kc, hs[hh], ko:ko + nk], p.astype(BF16))
            else:
                p = st * widen(fac_ref[hh, ks, :])
                qfac = jnp.exp(widen(c_rep[hh][kc]) - m_row)
                d_blk = qfac * jnp.sum(p, axis=0, keepdims=True)
                n_blk = qfac * _dot(vt_ref[0, kc, hs[hh], ko:ko + nk], p.astype(BF16))
            d_blk, n_blk = _left_pad(d_blk, q0, 0.0), _left_pad(n_blk, q0, 0.0)
            den, num = (d_blk, n_blk) if den is None else (den + d_blk, num + n_blk)
        hv = num / jnp.maximum(jnp.abs(den), jnp.exp(nb_row))
        hv = hv * lax.rsqrt(jnp.mean(hv * hv, axis=0, keepdims=True) + EPS)
        o_ref[0, qs, hs[hh]] = (hv.T * gain_ref[:, hs[hh]]
                                * g_ref[0, qs, hs[hh]].astype(F32)).astype(BF16)

    chains = [(qi, hh) for qi in reversed(range(S // t)) for hh in range(n_head)]
    _two_phase(chains, scores, weigh_pv, ahead=1)


def _decay_attn(qt, k, vt, gate, rowf, colf, gain, batch, seq):
    n_grp = M_HEADS // HEADS_PER_STEP
    n_tile = seq // SEQ_TILE
    w = HEADS_PER_STEP * M_DH
    blk = pl.BlockSpec((1, seq, w), lambda b, p: (b, 0, p))
    t_blk = pl.BlockSpec((1, n_tile, w, SEQ_TILE), lambda b, p: (b, 0, p, 0))
    k3, g3 = (a.reshape(batch, seq, M_W) for a in (k, gate))
    return pl.pallas_call(
        _decay_attn_kernel,
        grid=(batch, n_grp),
        in_specs=[t_blk, blk, t_blk, blk,
                  pl.BlockSpec((1, 1, n_tile, SUBLANES, SEQ_TILE), lambda b, p: (b, p, 0, 0, 0)),
                  pl.BlockSpec((1, 1, seq, LANES), lambda b, p: (b, p, 0, 0)),
                  pl.BlockSpec((1, w), lambda b, p: (0, p))],
        out_specs=blk,
        out_shape=jax.ShapeDtypeStruct((batch, seq, M_W), BF16),
        scratch_shapes=[pltpu.VMEM((HEADS_PER_STEP, seq, LANES), F32),
                        pltpu.VMEM((HEADS_PER_STEP, seq - SEQ_TILE, LANES), F32)],
        compiler_params=_params("parallel", "parallel"),
        name="decay_attn",
    )(qt, k3, vt, g3, rowf, colf, gain).reshape(batch * seq, M_W)


def _mla_proj_kernel(h_ref, pos_ref, invf_ref, gq_ref, gkv_ref, wcq_ref, wckv_ref,
                     wkr_ref, waz_ref, wqn_ref, wqr_ref, wkn_ref, wvt_ref,
                     q_ref, k_ref, vt_ref, g_ref, if_ref):
    h = h_ref[...]
    tm = h.shape[0]
    scale = math.log2(math.e) / math.sqrt(A_DQK)
    ang = invf_ref[...] * pos_ref[0]
    c32, s32 = jnp.cos(ang), jnp.sin(ang)
    z32 = jnp.zeros((A_ROPE // 2, tm), F32)
    cos = jnp.concatenate([c32, z32, c32, z32], axis=0).T
    sin = jnp.concatenate([-s32, z32, s32, z32], axis=0).T
    cos_pair = jnp.concatenate([c32, c32, c32, c32], axis=0).T
    sin_pair = jnp.concatenate([-s32, -s32, s32, s32], axis=0).T

    def rope(r, cos=cos, sin=sin):
        return r * cos + pltpu.roll(r, LANES // 2, 1) * sin

    cq = _dot_nt(h, wcq_ref[...])
    ckv = _dot_nt(h, wckv_ref[...])
    g_ref[...] = _silu(_dot_nt(h, waz_ref[...])).astype(BF16)
    kr_if = _dot_nt(h, wkr_ref[...])
    if_ref[...] = kr_if
    kr = rope(kr_if)
    kr_by_parity = (kr.astype(BF16), pltpu.roll(kr, A_ROPE // 2, 1).astype(BF16))

    cqn = _rms(cq, gq_ref[...]).astype(BF16)
    ckvn = _rms(ckv, gkv_ref[...]).astype(BF16)
    qn = _dot(cqn, wqn_ref[...]) * scale
    qr = _dot(cqn, wqr_ref[...]) * scale
    kn = _dot(ckvn, wkn_ref[...])
    for hd in range(A_HEADS):
        base = hd * A_QK_PAD
        ls = slice(hd * LANES, (hd + 1) * LANES)
        pair = slice((hd // 2) * LANES, (hd // 2 + 1) * LANES)
        q_ref[:, base:base + LANES] = qn[:, ls].astype(BF16)
        q_ref[:, base + LANES:base + A_QK_PAD] = rope(qr[:, pair], cos_pair, sin_pair).astype(BF16)
        k_ref[:, base:base + LANES] = kn[:, ls].astype(BF16)
        k_ref[:, base + LANES:base + A_QK_PAD] = kr_by_parity[hd % 2]
    vt_ref[0, 0] = _dot_nt(wvt_ref[...], ckvn).astype(BF16)


def _mla_proj(h, pos_row, invf, gq, gkv, wall, wqn, wqr, wkn, wvt, batch, seq):
    T = h.shape[0]
    tm = SEQ_TILE
    per_seq = seq // tm
    row = lambda w: pl.BlockSpec((tm, w), lambda i: (i, 0))
    small = (invf, gq, gkv)
    ups = (wqn, wqr, wkn, wvt)
    vt_spec = pl.BlockSpec((1, 1, A_W, tm), lambda i: (i // per_seq, i % per_seq, 0, 0))
    return pl.pallas_call(
        _mla_proj_kernel,
        grid=(T // tm,),
        in_specs=([row(D_MODEL), pl.BlockSpec((1, 1, tm), lambda i: (i, 0, 0))]
                  + [_const_spec(c.shape) for c in small]
                  + [_w_rows(W_CQ, Q_LORA), _w_rows(W_CKV, KV_LORA), _w_rows(W_KR, LANES),
                     _w_rows(W_AZ, A_W)]
                  + [_const_spec(c.shape) for c in ups]),
        out_specs=(row(A_HEADS * A_QK_PAD), row(A_HEADS * A_QK_PAD), vt_spec, row(A_W),
                   row(LANES)),
        out_shape=(jax.ShapeDtypeStruct((T, A_HEADS * A_QK_PAD), BF16),
                   jax.ShapeDtypeStruct((T, A_HEADS * A_QK_PAD), BF16),
                   jax.ShapeDtypeStruct((batch, per_seq, A_W, tm), BF16),
                   jax.ShapeDtypeStruct((T, A_W), BF16),
                   jax.ShapeDtypeStruct((T, LANES), F32)),
        compiler_params=_params("parallel"),
        name="mla_proj",
    )(h, pos_row, *small, wall, wall, wall, wall, *ups)


def _mla_attn_kernel(q_ref, k_ref, vt_ref, g_ref, o_ref):
    S = q_ref.shape[1]
    t = SEQ_TILE
    n_head = MLA_HEADS_PER_STEP
    hq = [slice(hh * A_QK_PAD, (hh + 1) * A_QK_PAD) for hh in range(n_head)]
    hv = [slice(hh * A_DV, (hh + 1) * A_DV) for hh in range(n_head)]

    def scores(qi, hh):
        q = q_ref[0, qi * t:(qi + 1) * t, hq[hh]]
        st = []
        for k0, nk, q0, diag in _causal_pieces(qi, t, t):
            s = _dot_nt(k_ref[0, k0:k0 + nk, hq[hh]], q[q0:, :])
            st.append(jnp.where(_keys_le_queries(nk, t - q0), s, -jnp.inf) if diag else s)
        return st

    def softmax_pv(qi, hh, st):
        qs = slice(qi * t, (qi + 1) * t)
        pieces = _causal_pieces(qi, t, t)
        m = functools.reduce(jnp.maximum, [
            _left_pad(jnp.max(s, axis=0, keepdims=True), q0, -jnp.inf)
            for s, (_, _, q0, _) in zip(st, pieces)])
        l = acc = None
        for s, (k0, nk, q0, _) in zip(st, pieces):
            p = jnp.exp2(s - m[:, q0:])
            l_blk = _left_pad(jnp.sum(p, axis=0, keepdims=True), q0, 0.0)
            kc, ko = divmod(k0, t)
            pv = _left_pad(_dot(vt_ref[0, kc, hv[hh], ko:ko + nk], p.astype(BF16)), q0, 0.0)
            l, acc = (l_blk, pv) if l is None else (l + l_blk, acc + pv)
        o_ref[0, qs, hv[hh]] = ((acc / l).T * g_ref[0, qs, hv[hh]].astype(F32)).astype(BF16)

    chains = [(qi, hh) for qi in range(S // t) for hh in range(n_head)]
    _two_phase(chains, scores, softmax_pv, ahead=2)


def _mla_attn(q, k, vt, gate, batch, seq):
    n_tile = seq // SEQ_TILE
    n_h = MLA_HEADS_PER_STEP
    qk_blk = pl.BlockSpec((1, seq, n_h * A_QK_PAD), lambda b, p: (b, 0, p))
    v_blk = pl.BlockSpec((1, seq, n_h * A_DV), lambda b, p: (b, 0, p))
    vt_blk = pl.BlockSpec((1, n_tile, n_h * A_DV, SEQ_TILE), lambda b, p: (b, 0, p, 0))
    q3 = q.reshape(batch, seq, A_HEADS * A_QK_PAD)
    k3 = k.reshape(batch, seq, A_HEADS * A_QK_PAD)
    g3 = gate.reshape(batch, seq, A_W)
    return pl.pallas_call(
        _mla_attn_kernel,
        grid=(batch, A_HEADS // n_h),
        in_specs=[qk_blk, qk_blk, vt_blk, v_blk],
        out_specs=v_blk,
        out_shape=jax.ShapeDtypeStruct((batch, seq, A_W), BF16),
        compiler_params=_params("parallel", "parallel"),
        name="mla_attn",
    )(q3, k3, vt, g3).reshape(batch * seq, A_W)


def _cast_once(w_ref, wbf_ref):
    @pl.when(pl.program_id(0) == 0)
    def _():
        wbf_ref[...] = w_ref[...].astype(BF16)


def _mem_kv_kernel(mem_ref, g_ref, w_ref, kv_ref, wbf_ref):
    _cast_once(w_ref, wbf_ref)
    nb, nm, d = mem_ref.shape
    m = _rms(mem_ref[...].reshape(nb * nm, d), g_ref[...]).astype(BF16)
    kv_ref[...] = _dot(m, wbf_ref[...]).astype(BF16).reshape(nb, nm, 2 * C_W)


MKV_BATCH = 4


def _mem_kv(mem, gain, w):
    batch = mem.shape[0]
    nb = MKV_BATCH if batch % MKV_BATCH == 0 else 1
    return pl.pallas_call(
        _mem_kv_kernel,
        grid=(batch // nb,),
        in_specs=[pl.BlockSpec((nb, N_MEM, D_MODEL), lambda b: (b, 0, 0)),
                  _const_spec((1, D_MODEL)), _const_spec(w.shape)],
        out_specs=pl.BlockSpec((nb, N_MEM, 2 * C_W), lambda b: (b, 0, 0)),
        out_shape=jax.ShapeDtypeStruct((batch, N_MEM, 2 * C_W), BF16),
        scratch_shapes=[pltpu.VMEM(w.shape, BF16)],
        compiler_params=_params("arbitrary"),
        name="mem_kv",
    )(mem, gain, w)


def _mem_attn_kernel(h_ref, kv_ref, w_ref, o_ref, wbf_ref):
    _cast_once(w_ref, wbf_ref)
    h = h_ref[...]
    cq = (_dot_nt(h, wbf_ref[0:C_W, :]) * (C_DH ** -0.5)).astype(BF16)
    heads = [slice(hd * C_DH, (hd + 1) * C_DH) for hd in range(C_HEADS)]

    def matmuls(hs):
        gate_rows = slice(C_W + hs.start, C_W + hs.stop)
        return _dot_nt(cq[:, hs], kv_ref[0, :, hs]), _dot_nt(h, wbf_ref[gate_rows, :])

    s, cz = matmuls(heads[0])
    for hd, hs in enumerate(heads):
        nxt = matmuls(heads[hd + 1]) if hd + 1 < C_HEADS else None
        p = jnp.exp(s - jnp.max(s, axis=-1, keepdims=True))
        l = jnp.sum(p, axis=-1, keepdims=True)
        o = _dot(p.astype(BF16), kv_ref[0, :, C_W + hd * C_DH:C_W + (hd + 1) * C_DH]) / l
        o_ref[:, hs] = (o * _silu(cz)).astype(BF16)
        if nxt is not None:
            s, cz = nxt


MA_TM = 1024


def _mem_attn(h, kv, wt, seq):
    T = h.shape[0]
    tm = MA_TM
    per_seq = seq // tm
    return pl.pallas_call(
        _mem_attn_kernel,
        grid=(T // tm,),
        in_specs=[pl.BlockSpec((tm, D_MODEL), lambda i: (i, 0)),
                  pl.BlockSpec((1, N_MEM, 2 * C_W), lambda i: (i // per_seq, 0, 0)),
                  _w_in_rows(IN_OFFS[10], 2 * C_W)],
        out_specs=pl.BlockSpec((tm, C_W), lambda i: (i, 0)),
        out_shape=jax.ShapeDtypeStruct((T, C_W), BF16),
        scratch_shapes=[pltpu.VMEM((2 * C_W, D_MODEL), BF16)],
        compiler_params=_params("arbitrary"),
        name="mem_attn",
    )(h, kv, wt)


MG_TM = 1024
MG_TN = 512


def _merge_kernel(h_ref, hm_ref, ha_ref, hc_ref, wgm_ref, wga_ref, wgc_ref,
                  wbm_ref, wba_ref, wbc_ref, o_ref):
    h = h_ref[...]
    gate = lambda w_ref: _sigmoid(_dot_nt(h, w_ref[...]))
    branch = lambda a_ref, w_ref: _dot(a_ref[...], w_ref[...])
    acc = gate(wgm_ref) * branch(hm_ref, wbm_ref)
    acc = acc + gate(wga_ref) * branch(ha_ref, wba_ref)
    acc = acc + gate(wgc_ref) * branch(hc_ref, wbc_ref)
    o_ref[...] = acc.astype(BF16)


def _merge(h, hm, ha, hc, wt, wbm, wba, wbc):
    T = h.shape[0]
    tm, tn = MG_TM, MG_TN
    row = lambda w: pl.BlockSpec((tm, w), lambda i, j: (i, 0))
    col = lambda kdim: pl.BlockSpec((kdim, tn), lambda i, j: (0, j))

    def gate(branch):
        first = (W_GATE + branch * D_MODEL) // tn
        return pl.BlockSpec((tn, D_MODEL), lambda i, j: (first + j, 0))

    return pl.pallas_call(
        _merge_kernel,
        grid=(T // tm, D_MODEL // tn),
        in_specs=[row(D_MODEL), row(M_W), row(A_W), row(C_W),
                  gate(0), gate(1), gate(2), col(M_W), col(A_W), col(C_W)],
        out_specs=pl.BlockSpec((tm, tn), lambda i, j: (i, j)),
        out_shape=jax.ShapeDtypeStruct((T, D_MODEL), BF16),
        compiler_params=_params("parallel", "arbitrary"),
        name="merge",
    )(h, hm, ha, hc, wt, wt, wt, wbm, wba, wbc)


def _out_proj_kernel(final, x_ref, m_ref, w_ref, g_ref, o_ref, wbf_ref):
    _cast_once(w_ref, wbf_ref)
    y = x_ref[...] + _dot(m_ref[...], wbf_ref[...])
    o_ref[...] = _rms(y, g_ref[...]) if final else y


def _out_proj(x2, merged, w, gain, final):
    T = x2.shape[0]
    tm = SEQ_TILE
    row = pl.BlockSpec((tm, D_MODEL), lambda i: (i, 0))
    return pl.pallas_call(
        functools.partial(_out_proj_kernel, final),
        grid=(T // tm,),
        in_specs=[row, row, _const_spec(w.shape), _const_spec((1, D_MODEL))],
        out_specs=row,
        out_shape=jax.ShapeDtypeStruct((T, D_MODEL), F32),
        scratch_shapes=[pltpu.VMEM(w.shape, BF16)],
        compiler_params=_params("arbitrary"),
        name="out_proj",
    )(x2, merged, w, gain)


PK_ROWS = 1024
W_GATE = W_KR + PK_ROWS
W_ROWS = W_GATE + 3 * D_MODEL
PK_SRC = ([b * PK_ROWS for b in range(IN_OFFS[4] // PK_ROWS)]
          + [IN_OFFS[9], IN_OFFS[6], IN_OFFS[8]]
          + [IN_OFFS[12] + b * PK_ROWS for b in range(3 * D_MODEL // PK_ROWS)])


def _pack_w_kernel(w_ref, wif_ref, o_ref):
    b = pl.program_id(0)
    x = w_ref[...]
    o_ref[...] = x.astype(BF16)

    @pl.when(b == W_KR // PK_ROWS)
    def _():
        half = A_ROPE // 2
        n_if = 2 * M_HEADS
        zeros = lambda n: jnp.zeros((n, D_MODEL), F32)
        o_ref[...] = jnp.concatenate(
            [x[:half], wif_ref[...], zeros(half - n_if), x[half:2 * half], zeros(half),
             zeros(PK_ROWS - 4 * half)], axis=0).astype(BF16)


def _pack_w_in(wt):
    assert len(PK_SRC) * PK_ROWS == W_ROWS and all(s % SUBLANES == 0 for s in PK_SRC)

    def src(b):
        first = sum(jnp.where(b == i, s, 0) for i, s in enumerate(PK_SRC))
        return pl.multiple_of(first, SUBLANES), 0

    return pl.pallas_call(
        _pack_w_kernel,
        grid=(len(PK_SRC),),
        in_specs=[pl.BlockSpec((pl.Element(PK_ROWS), pl.Element(D_MODEL)), src),
                  pl.BlockSpec((pl.Element(2 * M_HEADS), pl.Element(D_MODEL)),
                               lambda b: (IN_OFFS[4], 0))],
        out_specs=pl.BlockSpec((PK_ROWS, D_MODEL), lambda b: (b, 0)),
        out_shape=jax.ShapeDtypeStruct((W_ROWS, D_MODEL), BF16),
        compiler_params=_params("parallel"),
        name="pack_w",
    )(wt, wt)


def _layer(x2, pos_row, kv_mem_in, l, final, batch, seq, w_in, b_igate, b_fgate, conv_w, conv_b,
           mh_norm, cq_norm, w_uq, ckv_norm, w_ukv, mem_norm, w_mem_kv, w_br_m, w_br_a, w_br_c,
           w_out, norm, final_norm):
    bf = lambda a: a.astype(BF16)
    row = lambda a: a.reshape(1, -1).astype(F32)

    wt = w_in[l].T
    wall = _pack_w_in(wt)
    uq = w_uq[l].reshape(Q_LORA, A_HEADS, A_DQK)
    wqn = bf(uq[:, :, :A_NOPE].reshape(Q_LORA, A_HEADS * A_NOPE))
    half = A_ROPE // 2
    wqr = bf(uq[:, :, A_NOPE:].reshape(Q_LORA, A_HEADS // 2, 2, 2, half)
             .transpose(0, 1, 3, 2, 4).reshape(Q_LORA, (A_HEADS // 2) * LANES))
    ukv = w_ukv[l].reshape(KV_LORA, A_HEADS, A_NOPE + A_DV)
    wkn = bf(ukv[:, :, :A_NOPE].reshape(KV_LORA, A_HEADS * A_NOPE))
    wvvt = bf(ukv[:, :, A_NOPE:].reshape(KV_LORA, A_HEADS * A_DV).T)
    bias_row = jnp.pad(jnp.concatenate([b_igate[l], b_fgate[l]]).astype(F32),
                       (IF_LANE, LANES - IF_LANE - 2 * M_HEADS)).reshape(1, LANES)
    invf = (ROPE_THETA ** (-jnp.arange(0, A_ROPE, 2, dtype=F32) / A_ROPE)).reshape(-1, 1)

    lane_rep = lambda a: jnp.broadcast_to(a.astype(F32)[..., None], a.shape + (LANES,))
    h, mqt, mk, mvt, mgate = _mlstm_proj(x2, row(norm[l]), wall, lane_rep(conv_w[l]),
                                         lane_rep(conv_b[l]), batch, seq)
    aq, ak, avt, agate, pre_if = _mla_proj(h, pos_row, invf, row(cq_norm[l]), row(ckv_norm[l]),
                                           wall, wqn, wqr, wkn, wvvt, batch, seq)
    rowf, colf = _gate_scan(pre_if, bias_row, batch, seq)
    hm = _decay_attn(mqt, mk, mvt, mgate, rowf, colf, row(mh_norm[l]), batch, seq)
    ha = _mla_attn(aq, ak, avt, agate, batch, seq)

    kv_mem = _mem_kv(kv_mem_in, row(mem_norm[l]), w_mem_kv[l])
    hc = _mem_attn(h, kv_mem, wt, seq)

    merged = _merge(h, hm, ha, hc, wall, bf(w_br_m[l]), bf(w_br_a[l]), bf(w_br_c[l]))
    return _out_proj(x2, merged, w_out[l], row(final_norm), final)


def kernel(x, mem, positions, w_in, b_igate, b_fgate, conv_w, conv_b, mh_norm, cq_norm, w_uq,
           ckv_norm, w_ukv, mem_norm, w_mem_kv, w_br_m, w_br_a, w_br_c, w_out, norm, final_norm):
    batch, seq, d = x.shape
    depth = w_in.shape[0]
    assert d == D_MODEL and seq % MG_TM == 0 and w_in.shape[2] == sum(IN_SPLITS)
    x2 = x.reshape(batch * seq, d)
    pos_row = positions.astype(F32).reshape(batch * seq // SEQ_TILE, 1, SEQ_TILE)
    for l in range(depth):
        x2 = _layer(x2, pos_row, mem, l, l == depth - 1, batch, seq, w_in, b_igate, b_fgate,
                    conv_w, conv_b, mh_norm, cq_norm, w_uq, ckv_norm, w_ukv, mem_norm, w_mem_kv,
                    w_br_m, w_br_a, w_br_c, w_out, norm, final_norm)
    return x2.reshape(batch, seq, d)
```

```python
import functools
import math

import jax
import jax.numpy as jnp
from jax import lax
from jax.experimental import pallas as pl
from jax.experimental.pallas import tpu as pltpu

F32 = jnp.float32
BF16 = jnp.bfloat16

D_MODEL = 2048
M_HEADS, M_DH = 4, 256
M_W = M_HEADS * M_DH
CONV_K = 4
A_HEADS, A_NOPE, A_ROPE, A_DV = 8, 128, 64, 128
A_DQK = A_NOPE + A_ROPE
A_W = A_HEADS * A_DV
Q_LORA = KV_LORA = 512
ROPE_THETA = 10000.0
N_MEM = 256
C_HEADS, C_DH = 4, 256
C_W = C_HEADS * C_DH
EPS = 1e-6

LANES = 128
SUBLANES = 8
A_QK_PAD = 2 * LANES
VMEM_LIMIT = 56 * 1024 * 1024

SEQ_TILE = 512
HEADS_PER_STEP = 4
MLA_HEADS_PER_STEP = 4

IN_SPLITS = (2 * M_W, M_W, M_W, M_W, M_HEADS, M_HEADS, Q_LORA, KV_LORA, A_ROPE, A_W, C_W, C_W,
             3 * D_MODEL)


def _const_spec(shape):
    nd = len(shape)
    return pl.BlockSpec(shape, lambda *_: (0,) * nd, pipeline_mode=pl.Buffered(1))


IN_OFFS = [sum(IN_SPLITS[:i]) for i in range(len(IN_SPLITS) + 1)]

W_QK, W_V, W_O, W_Z, W_AZ = 0, 2 * M_W, 3 * M_W, 4 * M_W, 5 * M_W
W_CQ = W_AZ + A_W
W_CKV = W_CQ + Q_LORA
W_KR = W_CKV + KV_LORA
IF_LANE = A_ROPE // 2


def _w_rows(offset, rows):
    assert offset % rows == 0
    return pl.BlockSpec((rows, D_MODEL), lambda *_: (offset // rows, 0),
                        pipeline_mode=pl.Buffered(1))


def _w_in_rows(first, rows):
    assert first % SUBLANES == 0
    return pl.BlockSpec((pl.Element(rows), pl.Element(D_MODEL)), lambda *_: (first, 0),
                        pipeline_mode=pl.Buffered(1))


def _params(*sem):
    return pltpu.CompilerParams(dimension_semantics=sem, vmem_limit_bytes=VMEM_LIMIT)


def _dot(a, b):
    return jnp.dot(a, b, preferred_element_type=F32)


def _dot_nt(a, b):
    return lax.dot_general(a, b, (((1,), (1,)), ((), ())), preferred_element_type=F32)


def _rms(x, g):
    return x * lax.rsqrt(jnp.mean(x * x, axis=-1, keepdims=True) + EPS) * g


def _sigmoid(x):
    return 1.0 / (1.0 + jnp.exp(-x))


def _silu(x):
    return x * _sigmoid(x)


def _keys_le_queries(nk, nq):
    shape = (nk, nq)
    return lax.broadcasted_iota(jnp.int32, shape, 0) <= lax.broadcasted_iota(jnp.int32, shape, 1)


def _causal_pieces(qi, t, nk_full):
    half = t // 2
    return ([(k0, nk_full, 0, False) for k0 in range(0, qi * t, nk_full)]
            + [(qi * t, half, 0, True), (qi * t + half, half, half, True)])


def _two_phase(chains, first, second, ahead):
    pending = [first(*c) for c in chains[:ahead]]
    for n, chain in enumerate(chains):
        if n + ahead < len(chains):
            pending.append(first(*chains[n + ahead]))
        second(*chain, pending.pop(0))


def _left_pad(x, n, fill):
    if n == 0:
        return x
    return jnp.concatenate([jnp.full((x.shape[0], n), fill, x.dtype), x], axis=1)


MP_CH = 1024


def _mlstm_proj_kernel(tiles_per_seq, x_ref, ng_ref, wqk_ref, cw_ref, cb_ref, wvt_ref, wo_ref,
                       wz_ref, h_ref, qt_ref, k_ref, vt_ref, g_ref, xbuf_ref):
    tm = x_ref.shape[0]

    @pl.when(pl.program_id(0) % tiles_per_seq == 0)
    def _():
        xbuf_ref[...] = jnp.zeros_like(xbuf_ref)

    h = _rms(x_ref[...], ng_ref[...]).astype(BF16)
    h_ref[...] = h
    widen = lambda a: jnp.concatenate([a] * (tm // LANES), axis=1)

    def conv_chunk(c):
        fs = slice(c * MP_CH, (c + 1) * MP_CH)
        acc = _dot_nt(wqk_ref[fs, :], h)
        prev = xbuf_ref[fs, :]
        lane = lax.broadcasted_iota(jnp.int32, prev.shape, 1)
        y = widen(cb_ref[fs, :]) + widen(cw_ref[CONV_K - 1, fs, :]) * acc
        for j in range(CONV_K - 1):
            back = CONV_K - 1 - j
            rolled = pltpu.roll(acc, back, 1)
            head = jnp.where(lane < back, pltpu.roll(prev, back, 1), rolled[:, :LANES])
            tap = jnp.concatenate([head, rolled[:, LANES:]], axis=1)
            y = y + widen(cw_ref[j, fs, :]) * tap
        xbuf_ref[fs, :] = acc[:, tm - LANES:]
        y = _silu(y)
        if c < M_W // MP_CH:
            qt_ref[0, 0, fs, :] = y.astype(BF16)
        else:
            ks = slice(c * MP_CH - M_W, (c + 1) * MP_CH - M_W)
            k_ref[:, ks] = (y.T * (M_DH ** -0.5)).astype(BF16)

    def value_chunk(c):
        cs = slice(c * MP_CH, (c + 1) * MP_CH)
        vt_ref[0, 0, cs, :] = _dot_nt(wvt_ref[cs, :], h).astype(BF16)

    def gate_chunk(c):
        cs = slice(c * MP_CH, (c + 1) * MP_CH)
        o = _dot_nt(h, wo_ref[cs, :])
        z = _dot_nt(h, wz_ref[cs, :])
        g_ref[:, cs] = (_sigmoid(o) * _silu(z)).astype(BF16)

    light = [functools.partial(f, c) for c in range(M_W // MP_CH) for f in (value_chunk, gate_chunk)]
    for c in range(2 * M_W // MP_CH):
        conv_chunk(c)
        if c < len(light):
            light[c]()
    for f in light[2 * M_W // MP_CH:]:
        f()


def _mlstm_proj(x2, norm_g, wall, conv_w, conv_b, batch, seq):
    T = x2.shape[0]
    tm = SEQ_TILE
    per_seq = seq // tm
    row = lambda w: pl.BlockSpec((tm, w), lambda i: (i, 0))
    out_shape = (
        jax.ShapeDtypeStruct((T, D_MODEL), BF16),
        jax.ShapeDtypeStruct((batch, per_seq, M_W, tm), BF16),
        jax.ShapeDtypeStruct((T, M_W), BF16),
        jax.ShapeDtypeStruct((batch, per_seq, M_W, tm), BF16),
        jax.ShapeDtypeStruct((T, M_W), BF16),
    )
    vt_spec = pl.BlockSpec((1, 1, M_W, tm), lambda i: (i // per_seq, i % per_seq, 0, 0))
    return pl.pallas_call(
        functools.partial(_mlstm_proj_kernel, per_seq),
        grid=(T // tm,),
        in_specs=[row(D_MODEL), _const_spec((1, D_MODEL)), _w_rows(W_QK, 2 * M_W),
                  _const_spec(conv_w.shape), _const_spec(conv_b.shape), _w_rows(W_V, M_W),
                  _w_rows(W_O, M_W), _w_rows(W_Z, M_W)],
        out_specs=(row(D_MODEL), vt_spec, row(M_W), vt_spec, row(M_W)),
        out_shape=out_shape,
        scratch_shapes=[pltpu.VMEM((2 * M_W, LANES), F32)],
        compiler_params=_params("arbitrary"),
        name="mlstm_proj",
    )(x2, norm_g, wall, conv_w, conv_b, wall, wall, wall)


def _lane_scan(x, op, fill):
    n = x.shape[-1]
    lane = lax.broadcasted_iota(jnp.int32, x.shape, x.ndim - 1)
    d = 1
    while d < n:
        shifted = pltpu.roll(x, d, x.ndim - 1)
        x = op(x, jnp.where(lane >= d, shifted, fill))
        d *= 2
    return x


def _gate_scan_kernel(if_ref, bias_ref, row_ref, col_ref):
    n_seq, S = if_ref.shape[0], if_ref.shape[1]
    t = SEQ_TILE
    rows8 = lambda x, s: x[s * SUBLANES:(s + 1) * SUBLANES]
    t8 = jnp.concatenate([(if_ref[s] + bias_ref[...]).T[IF_LANE:IF_LANE + SUBLANES, :]
                          for s in range(n_seq)], axis=0)
    lf = jnp.minimum(t8, 0.0) - jnp.log(1.0 + jnp.exp(-jnp.abs(t8)))
    b_all = _lane_scan(lf, jnp.add, 0.0)
    b_all = jnp.concatenate([pltpu.roll(rows8(b_all, s), M_HEADS, 0) for s in range(n_seq)],
                            axis=0)
    a_all = t8 - b_all
    mx_all = jnp.maximum(_lane_scan(a_all, jnp.maximum, -jnp.inf), 0.0)
    nb_all = -b_all - mx_all
    sub = lax.broadcasted_iota(jnp.int32, (SUBLANES, S), 0)
    zeros = jnp.zeros((LANES - SUBLANES, S), F32)
    n_h = HEADS_PER_STEP
    for s in range(n_seq):
        a, mx, nb = rows8(a_all, s), rows8(mx_all, s), rows8(nb_all, s)
        for p in range(M_HEADS // n_h):
            up = (SUBLANES - n_h * p) % SUBLANES
            m_grp = pltpu.roll(mx, up, 0) if up else mx
            nb_grp = pltpu.roll(nb, (up + n_h) % SUBLANES, 0)
            stack = jnp.where(sub < n_h, m_grp, nb_grp)
            for j in range(S // t):
                row_ref[s, p, j] = stack[:, j * t:(j + 1) * t]
            a_grp = pltpu.roll(a, up, 0) if up else a
            col_ref[s, p] = jnp.concatenate([a_grp, zeros], axis=0).T


GS_SEQS = 2


def _gate_scan(pre_if, bias_row, batch, seq):
    n_grp = M_HEADS // HEADS_PER_STEP
    n_tile = seq // SEQ_TILE
    ns = GS_SEQS if batch % GS_SEQS == 0 else 1
    return pl.pallas_call(
        _gate_scan_kernel,
        grid=(batch // ns,),
        in_specs=[pl.BlockSpec((ns, seq, LANES), lambda b: (b, 0, 0)), _const_spec((1, LANES))],
        out_specs=(pl.BlockSpec((ns, n_grp, n_tile, SUBLANES, SEQ_TILE), lambda b: (b, 0, 0, 0, 0)),
                   pl.BlockSpec((ns, n_grp, seq, LANES), lambda b: (b, 0, 0, 0))),
        out_shape=(jax.ShapeDtypeStruct((batch, n_grp, n_tile, SUBLANES, SEQ_TILE), F32),
                   jax.ShapeDtypeStruct((batch, n_grp, seq, LANES), F32)),
        compiler_params=_params("parallel"),
        name="gate_scan",
    )(pre_if.reshape(batch, seq, LANES), bias_row)


def _decay_attn_kernel(qt_ref, k_ref, vt_ref, g_ref, row_ref, col_ref, gain_ref, o_ref,
                       arep_ref, fac_ref):
    S = k_ref.shape[1]
    t = SEQ_TILE
    n_head = HEADS_PER_STEP
    hs = [slice(hh * M_DH, (hh + 1) * M_DH) for hh in range(n_head)]
    lane_tiles = t // LANES
    widen = lambda a: jnp.concatenate([a] * lane_tiles, axis=1)

    c_rep = [[None] * (S // t) for _ in range(n_head)]
    for hh in range(n_head):
        arep_ref[hh] = jnp.broadcast_to(col_ref[0, 0, :, hh:hh + 1], (S, LANES))
        for kj in range(S // t - 1):
            a = arep_ref[hh, kj * t:(kj + 1) * t, :]
            c_rep[hh][kj] = jnp.max(a, axis=0, keepdims=True)
            fac_ref[hh, kj * t:(kj + 1) * t, :] = jnp.exp(a - c_rep[hh][kj])

    def scores(qi, hh):
        qt = qt_ref[0, qi, hs[hh], :]
        return [_dot(k_ref[0, k0:k0 + nk, hs[hh]], qt[:, q0:])
                for k0, nk, q0, _ in _causal_pieces(qi, t, t)]

    def weigh_pv(qi, hh, sts):
        qs = slice(qi * t, (qi + 1) * t)
        m_row = row_ref[0, 0, qi, hh:hh + 1, :]
        nb_row = row_ref[0, 0, qi, n_head + hh:n_head + hh + 1, :]
        den = num = None
        for st, (k0, nk, q0, diag) in zip(sts, _causal_pieces(qi, t, t)):
            ks = slice(k0, k0 + nk)
            kc, ko = divmod(k0, t)
            if diag:
                arg = widen(arep_ref[hh, ks, :])[:, q0:] - m_row[:, q0:]
                p = st * jnp.exp(jnp.where(_keys_le_queries(nk, t - q0), arg, -jnp.inf))
                d_blk = jnp.sum(p, axis=0, keepdims=True)
                n_blk = _dot(vt_ref[0, kc, hs[hh], ko:ko + nk], p.astype(BF16))
            else:
                p = st * widen(fac_ref[hh, ks, :])
                qfac = jnp.exp(widen(c_rep[hh][kc]) - m_row)
                d_blk = qfac * jnp.sum(p, axis=0, keepdims=True)
                n_blk = qfac * _dot(vt_ref[0, kc, hs[hh], ko:ko + nk], p.astype(BF16))
            d_blk, n_blk = _left_pad(d_blk, q0, 0.0), _left_pad(n_blk, q0, 0.0)
            den, num = (d_blk, n_blk) if den is None else (den + d_blk, num + n_blk)
        hv = num / jnp.maximum(jnp.abs(den), jnp.exp(nb_row))
        hv = hv * lax.rsqrt(jnp.mean(hv * hv, axis=0, keepdims=True) + EPS)
        o_ref[0, qs, hs[hh]] = (hv.T * gain_ref[:, hs[hh]]
                                * g_ref[0, qs, hs[hh]].astype(F32)).astype(BF16)

    chains = [(qi, hh) for qi in reversed(range(S // t)) for hh in range(n_head)]
    _two_phase(chains, scores, weigh_pv, ahead=1)


def _decay_attn(qt, k, vt, gate, rowf, colf, gain, batch, seq):
    n_grp = M_HEADS // HEADS_PER_STEP
    n_tile = seq // SEQ_TILE
    w = HEADS_PER_STEP * M_DH
    blk = pl.BlockSpec((1, seq, w), lambda b, p: (b, 0, p))
    t_blk = pl.BlockSpec((1, n_tile, w, SEQ_TILE), lambda b, p: (b, 0, p, 0))
    k3, g3 = (a.reshape(batch, seq, M_W) for a in (k, gate))
    return pl.pallas_call(
        _decay_attn_kernel,
        grid=(batch, n_grp),
        in_specs=[t_blk, blk, t_blk, blk,
                  pl.BlockSpec((1, 1, n_tile, SUBLANES, SEQ_TILE), lambda b, p: (b, p, 0, 0, 0)),
                  pl.BlockSpec((1, 1, seq, LANES), lambda b, p: (b, p, 0, 0)),
                  pl.BlockSpec((1, w), lambda b, p: (0, p))],
        out_specs=blk,
        out_shape=jax.ShapeDtypeStruct((batch, seq, M_W), BF16),
        scratch_shapes=[pltpu.VMEM((HEADS_PER_STEP, seq, LANES), F32),
                        pltpu.VMEM((HEADS_PER_STEP, seq - SEQ_TILE, LANES), F32)],
        compiler_params=_params("parallel", "parallel"),
        name="decay_attn",
    )(qt, k3, vt, g3, rowf, colf, gain).reshape(batch * seq, M_W)


def _mla_proj_kernel(h_ref, pos_ref, invf_ref, gq_ref, gkv_ref, wcq_ref, wckv_ref,
                     wkr_ref, waz_ref, wqn_ref, wqr_ref, wkn_ref, wvt_ref,
                     q_ref, k_ref, vt_ref, g_ref, if_ref):
    h = h_ref[...]
    tm = h.shape[0]
    scale = math.log2(math.e) / math.sqrt(A_DQK)
    ang = invf_ref[...] * pos_ref[0]
    c32, s32 = jnp.cos(ang), jnp.sin(ang)
    z32 = jnp.zeros((A_ROPE // 2, tm), F32)
    cos = jnp.concatenate([c32, z32, c32, z32], axis=0).T
    sin = jnp.concatenate([-s32, z32, s32, z32], axis=0).T
    cos_pair = jnp.concatenate([c32, c32, c32, c32], axis=0).T
    sin_pair = jnp.concatenate([-s32, -s32, s32, s32], axis=0).T

    def rope(r, cos=cos, sin=sin):
        return r * cos + pltpu.roll(r, LANES // 2, 1) * sin

    cq = _dot_nt(h, wcq_ref[...])
    ckv = _dot_nt(h, wckv_ref[...])
    g_ref[...] = _silu(_dot_nt(h, waz_ref[...])).astype(BF16)
    kr_if = _dot_nt(h, wkr_ref[...])
    if_ref[...] = kr_if
    kr = rope(kr_if)
    kr_by_parity = (kr.astype(BF16), pltpu.roll(kr, A_ROPE // 2, 1).astype(BF16))

    cqn = _rms(cq, gq_ref[...]).astype(BF16)
    ckvn = _rms(ckv, gkv_ref[...]).astype(BF16)
    qn = _dot(cqn, wqn_ref[...]) * scale
    qr = _dot(cqn, wqr_ref[...]) * scale
    kn = _dot(ckvn, wkn_ref[...])
    for hd in range(A_HEADS):
        base = hd * A_QK_PAD
        ls = slice(hd * LANES, (hd + 1) * LANES)
        pair = slice((hd // 2) * LANES, (hd // 2 + 1) * LANES)
        q_ref[:, base:base + LANES] = qn[:, ls].astype(BF16)
        q_ref[:, base + LANES:base + A_QK_PAD] = rope(qr[:, pair], cos_pair, sin_pair).astype(BF16)
        k_ref[:, base:base + LANES] = kn[:, ls].astype(BF16)
        k_ref[:, base + LANES:base + A_QK_PAD] = kr_by_parity[hd % 2]
    vt_ref[0, 0] = _dot_nt(wvt_ref[...], ckvn).astype(BF16)


def _mla_proj(h, pos_row, invf, gq, gkv, wall, wqn, wqr, wkn, wvt, batch, seq):
    T = h.shape[0]
    tm = SEQ_TILE
    per_seq = seq // tm
    row = lambda w: pl.BlockSpec((tm, w), lambda i: (i, 0))
    small = (invf, gq, gkv)
    ups = (wqn, wqr, wkn, wvt)
    vt_spec = pl.BlockSpec((1, 1, A_W, tm), lambda i: (i // per_seq, i % per_seq, 0, 0))
    return pl.pallas_call(
        _mla_proj_kernel,
        grid=(T // tm,),
        in_specs=([row(D_MODEL), pl.BlockSpec((1, 1, tm), lambda i: (i, 0, 0))]
                  + [_const_spec(c.shape) for c in small]
                  + [_w_rows(W_CQ, Q_LORA), _w_rows(W_CKV, KV_LORA), _w_rows(W_KR, LANES),
                     _w_rows(W_AZ, A_W)]
                  + [_const_spec(c.shape) for c in ups]),
        out_specs=(row(A_HEADS * A_QK_PAD), row(A_HEADS * A_QK_PAD), vt_spec, row(A_W),
                   row(LANES)),
        out_shape=(jax.ShapeDtypeStruct((T, A_HEADS * A_QK_PAD), BF16),
                   jax.ShapeDtypeStruct((T, A_HEADS * A_QK_PAD), BF16),
                   jax.ShapeDtypeStruct((batch, per_seq, A_W, tm), BF16),
                   jax.ShapeDtypeStruct((T, A_W), BF16),
                   jax.ShapeDtypeStruct((T, LANES), F32)),
        compiler_params=_params("parallel"),
        name="mla_proj",
    )(h, pos_row, *small, wall, wall, wall, wall, *ups)


def _mla_attn_kernel(q_ref, k_ref, vt_ref, g_ref, o_ref):
    S = q_ref.shape[1]
    t = SEQ_TILE
    n_head = MLA_HEADS_PER_STEP
    hq = [slice(hh * A_QK_PAD, (hh + 1) * A_QK_PAD) for hh in range(n_head)]
    hv = [slice(hh * A_DV, (hh + 1) * A_DV) for hh in range(n_head)]

    def scores(qi, hh):
        q = q_ref[0, qi * t:(qi + 1) * t, hq[hh]]
        st = []
        for k0, nk, q0, diag in _causal_pieces(qi, t, t):
            s = _dot_nt(k_ref[0, k0:k0 + nk, hq[hh]], q[q0:, :])
            st.append(jnp.where(_keys_le_queries(nk, t - q0), s, -jnp.inf) if diag else s)
        return st

    def softmax_pv(qi, hh, st):
        qs = slice(qi * t, (qi + 1) * t)
        pieces = _causal_pieces(qi, t, t)
        m = functools.reduce(jnp.maximum, [
            _left_pad(jnp.max(s, axis=0, keepdims=True), q0, -jnp.inf)
            for s, (_, _, q0, _) in zip(st, pieces)])
        l = acc = None
        for s, (k0, nk, q0, _) in zip(st, pieces):
            p = jnp.exp2(s - m[:, q0:])
            l_blk = _left_pad(jnp.sum(p, axis=0, keepdims=True), q0, 0.0)
            kc, ko = divmod(k0, t)
            pv = _left_pad(_dot(vt_ref[0, kc, hv[hh], ko:ko + nk], p.astype(BF16)), q0, 0.0)
            l, acc = (l_blk, pv) if l is None else (l + l_blk, acc + pv)
        o_ref[0, qs, hv[hh]] = ((acc / l).T * g_ref[0, qs, hv[hh]].astype(F32)).astype(BF16)

    chains = [(qi, hh) for qi in range(S // t) for hh in range(n_head)]
    _two_phase(chains, scores, softmax_pv, ahead=2)


def _mla_attn(q, k, vt, gate, batch, seq):
    n_tile = seq // SEQ_TILE
    n_h = MLA_HEADS_PER_STEP
    qk_blk = pl.BlockSpec((1, seq, n_h * A_QK_PAD), lambda b, p: (b, 0, p))
    v_blk = pl.BlockSpec((1, seq, n_h * A_DV), lambda b, p: (b, 0, p))
    vt_blk = pl.BlockSpec((1, n_tile, n_h * A_DV, SEQ_TILE), lambda b, p: (b, 0, p, 0))
    q3 = q.reshape(batch, seq, A_HEADS * A_QK_PAD)
    k3 = k.reshape(batch, seq, A_HEADS * A_QK_PAD)
    g3 = gate.reshape(batch, seq, A_W)
    return pl.pallas_call(
        _mla_attn_kernel,
        grid=(batch, A_HEADS // n_h),
        in_specs=[qk_blk, qk_blk, vt_blk, v_blk],
        out_specs=v_blk,
        out_shape=jax.ShapeDtypeStruct((batch, seq, A_W), BF16),
        compiler_params=_params("parallel", "parallel"),
        name="mla_attn",
    )(q3, k3, vt, g3).reshape(batch * seq, A_W)


def _cast_once(w_ref, wbf_ref):
    @pl.when(pl.program_id(0) == 0)
    def _():
        wbf_ref[...] = w_ref[...].astype(BF16)


def _mem_kv_kernel(mem_ref, g_ref, w_ref, kv_ref, wbf_ref):
    _cast_once(w_ref, wbf_ref)
    nb, nm, d = mem_ref.shape
    m = _rms(mem_ref[...].reshape(nb * nm, d), g_ref[...]).astype(BF16)
    kv_ref[...] = _dot(m, wbf_ref[...]).astype(BF16).reshape(nb, nm, 2 * C_W)


MKV_BATCH = 4


def _mem_kv(mem, gain, w):
    batch = mem.shape[0]
    nb = MKV_BATCH if batch % MKV_BATCH == 0 else 1
    return pl.pallas_call(
        _mem_kv_kernel,
        grid=(batch // nb,),
        in_specs=[pl.BlockSpec((nb, N_MEM, D_MODEL), lambda b: (b, 0, 0)),
                  _const_spec((1, D_MODEL)), _const_spec(w.shape)],
        out_specs=pl.BlockSpec((nb, N_MEM, 2 * C_W), lambda b: (b, 0, 0)),
        out_shape=jax.ShapeDtypeStruct((batch, N_MEM, 2 * C_W), BF16),
        scratch_shapes=[pltpu.VMEM(w.shape, BF16)],
        compiler_params=_params("arbitrary"),
        name="mem_kv",
    )(mem, gain, w)


def _mem_attn_kernel(h_ref, kv_ref, w_ref, o_ref, wbf_ref):
    _cast_once(w_ref, wbf_ref)
    h = h_ref[...]
    cq = (_dot_nt(h, wbf_ref[0:C_W, :]) * (C_DH ** -0.5)).astype(BF16)
    heads = [slice(hd * C_DH, (hd + 1) * C_DH) for hd in range(C_HEADS)]

    def matmuls(hs):
        gate_rows = slice(C_W + hs.start, C_W + hs.stop)
        return _dot_nt(cq[:, hs], kv_ref[0, :, hs]), _dot_nt(h, wbf_ref[gate_rows, :])

    s, cz = matmuls(heads[0])
    for hd, hs in enumerate(heads):
        nxt = matmuls(heads[hd + 1]) if hd + 1 < C_HEADS else None
        p = jnp.exp(s - jnp.max(s, axis=-1, keepdims=True))
        l = jnp.sum(p, axis=-1, keepdims=True)
        o = _dot(p.astype(BF16), kv_ref[0, :, C_W + hd * C_DH:C_W + (hd + 1) * C_DH]) / l
        o_ref[:, hs] = (o * _silu(cz)).astype(BF16)
        if nxt is not None:
            s, cz = nxt


MA_TM = 1024


def _mem_attn(h, kv, wt, seq):
    T = h.shape[0]
    tm = MA_TM
    per_seq = seq // tm
    return pl.pallas_call(
        _mem_attn_kernel,
        grid=(T // tm,),
        in_specs=[pl.BlockSpec((tm, D_MODEL), lambda i: (i, 0)),
                  pl.BlockSpec((1, N_MEM, 2 * C_W), lambda i: (i // per_seq, 0, 0)),
                  _w_in_rows(IN_OFFS[10], 2 * C_W)],
        out_specs=pl.BlockSpec((tm, C_W), lambda i: (i, 0)),
        out_shape=jax.ShapeDtypeStruct((T, C_W), BF16),
        scratch_shapes=[pltpu.VMEM((2 * C_W, D_MODEL), BF16)],
        compiler_params=_params("arbitrary"),
        name="mem_attn",
    )(h, kv, wt)


MG_TM = 1024
MG_TN = 512


def _merge_kernel(h_ref, hm_ref, ha_ref, hc_ref, wgm_ref, wga_ref, wgc_ref,
                  wbm_ref, wba_ref, wbc_ref, o_ref):
    h = h_ref[...]
    gate = lambda w_ref: _sigmoid(_dot_nt(h, w_ref[...]))
    branch = lambda a_ref, w_ref: _dot(a_ref[...], w_ref[...])
    acc = gate(wgm_ref) * branch(hm_ref, wbm_ref)
    acc = acc + gate(wga_ref) * branch(ha_ref, wba_ref)
    acc = acc + gate(wgc_ref) * branch(hc_ref, wbc_ref)
    o_ref[...] = acc.astype(BF16)


def _merge(h, hm, ha, hc, wt, wbm, wba, wbc):
    T = h.shape[0]
    tm, tn = MG_TM, MG_TN
    row = lambda w: pl.BlockSpec((tm, w), lambda i, j: (i, 0))
    col = lambda kdim: pl.BlockSpec((kdim, tn), lambda i, j: (0, j))

    def gate(branch):
        first = (W_GATE + branch * D_MODEL) // tn
        return pl.BlockSpec((tn, D_MODEL), lambda i, j: (first + j, 0))

    return pl.pallas_call(
        _merge_kernel,
        grid=(T // tm, D_MODEL // tn),
        in_specs=[row(D_MODEL), row(M_W), row(A_W), row(C_W),
                  gate(0), gate(1), gate(2), col(M_W), col(A_W), col(C_W)],
        out_specs=pl.BlockSpec((tm, tn), lambda i, j: (i, j)),
        out_shape=jax.ShapeDtypeStruct((T, D_MODEL), BF16),
        compiler_params=_params("parallel", "arbitrary"),
        name="merge",
    )(h, hm, ha, hc, wt, wt, wt, wbm, wba, wbc)


def _out_proj_kernel(final, x_ref, m_ref, w_ref, g_ref, o_ref, wbf_ref):
    _cast_once(w_ref, wbf_ref)
    y = x_ref[...] + _dot(m_ref[...], wbf_ref[...])
    o_ref[...] = _rms(y, g_ref[...]) if final else y


def _out_proj(x2, merged, w, gain, final):
    T = x2.shape[0]
    tm = SEQ_TILE
    row = pl.BlockSpec((tm, D_MODEL), lambda i: (i, 0))
    return pl.pallas_call(
        functools.partial(_out_proj_kernel, final),
        grid=(T // tm,),
        in_specs=[row, row, _const_spec(w.shape), _const_spec((1, D_MODEL))],
        out_specs=row,
        out_shape=jax.ShapeDtypeStruct((T, D_MODEL), F32),
        scratch_shapes=[pltpu.VMEM(w.shape, BF16)],
        compiler_params=_params("arbitrary"),
        name="out_proj",
    )(x2, merged, w, gain)


PK_ROWS = 1024
W_GATE = W_KR + PK_ROWS
W_ROWS = W_GATE + 3 * D_MODEL
PK_SRC = ([b * PK_ROWS for b in range(IN_OFFS[4] // PK_ROWS)]
          + [IN_OFFS[9], IN_OFFS[6], IN_OFFS[8]]
          + [IN_OFFS[12] + b * PK_ROWS for b in range(3 * D_MODEL // PK_ROWS)])


def _pack_w_kernel(w_ref, wif_ref, o_ref):
    b = pl.program_id(0)
    x = w_ref[...]
    o_ref[...] = x.astype(BF16)

    @pl.when(b == W_KR // PK_ROWS)
    def _():
        half = A_ROPE // 2
        n_if = 2 * M_HEADS
        zeros = lambda n: jnp.zeros((n, D_MODEL), F32)
        o_ref[...] = jnp.concatenate(
            [x[:half], wif_ref[...], zeros(half - n_if), x[half:2 * half], zeros(half),
             zeros(PK_ROWS - 4 * half)], axis=0).astype(BF16)


def _pack_w_in(wt):
    assert len(PK_SRC) * PK_ROWS == W_ROWS and all(s % SUBLANES == 0 for s in PK_SRC)

    def src(b):
        first = sum(jnp.where(b == i, s, 0) for i, s in enumerate(PK_SRC))
        return pl.multiple_of(first, SUBLANES), 0

    return pl.pallas_call(
        _pack_w_kernel,
        grid=(len(PK_SRC),),
        in_specs=[pl.BlockSpec((pl.Element(PK_ROWS), pl.Element(D_MODEL)), src),
                  pl.BlockSpec((pl.Element(2 * M_HEADS), pl.Element(D_MODEL)),
                               lambda b: (IN_OFFS[4], 0))],
        out_specs=pl.BlockSpec((PK_ROWS, D_MODEL), lambda b: (b, 0)),
        out_shape=jax.ShapeDtypeStruct((W_ROWS, D_MODEL), BF16),
        compiler_params=_params("parallel"),
        name="pack_w",
    )(wt, wt)


def _layer(x2, pos_row, kv_mem_in, l, final, batch, seq, w_in, b_igate, b_fgate, conv_w, conv_b,
           mh_norm, cq_norm, w_uq, ckv_norm, w_ukv, mem_norm, w_mem_kv, w_br_m, w_br_a, w_br_c,
           w_out, norm, final_norm):
    bf = lambda a: a.astype(BF16)
    row = lambda a: a.reshape(1, -1).astype(F32)

    wt = w_in[l].T
    wall = _pack_w_in(wt)
    uq = w_uq[l].reshape(Q_LORA, A_HEADS, A_DQK)
    wqn = bf(uq[:, :, :A_NOPE].reshape(Q_LORA, A_HEADS * A_NOPE))
    half = A_ROPE // 2
    wqr = bf(uq[:, :, A_NOPE:].reshape(Q_LORA, A_HEADS // 2, 2, 2, half)
             .transpose(0, 1, 3, 2, 4).reshape(Q_LORA, (A_HEADS // 2) * LANES))
    ukv = w_ukv[l].reshape(KV_LORA, A_HEADS, A_NOPE + A_DV)
    wkn = bf(ukv[:, :, :A_NOPE].reshape(KV_LORA, A_HEADS * A_NOPE))
    wvvt = bf(ukv[:, :, A_NOPE:].reshape(KV_LORA, A_HEADS * A_DV).T)
    bias_row = jnp.pad(jnp.concatenate([b_igate[l], b_fgate[l]]).astype(F32),
                       (IF_LANE, LANES - IF_LANE - 2 * M_HEADS)).reshape(1, LANES)
    invf = (ROPE_THETA ** (-jnp.arange(0, A_ROPE, 2, dtype=F32) / A_ROPE)).reshape(-1, 1)

    lane_rep = lambda a: jnp.broadcast_to(a.astype(F32)[..., None], a.shape + (LANES,))
    h, mqt, mk, mvt, mgate = _mlstm_proj(x2, row(norm[l]), wall, lane_rep(conv_w[l]),
                                         lane_rep(conv_b[l]), batch, seq)
    aq, ak, avt, agate, pre_if = _mla_proj(h, pos_row, invf, row(cq_norm[l]), row(ckv_norm[l]),
                                           wall, wqn, wqr, wkn, wvvt, batch, seq)
    rowf, colf = _gate_scan(pre_if, bias_row, batch, seq)
    hm = _decay_attn(mqt, mk, mvt, mgate, rowf, colf, row(mh_norm[l]), batch, seq)
    ha = _mla_attn(aq, ak, avt, agate, batch, seq)

    kv_mem = _mem_kv(kv_mem_in, row(mem_norm[l]), w_mem_kv[l])
    hc = _mem_attn(h, kv_mem, wt, seq)

    merged = _merge(h, hm, ha, hc, wall, bf(w_br_m[l]), bf(w_br_a[l]), bf(w_br_c[l]))
    return _out_proj(x2, merged, w_out[l], row(final_norm), final)


def kernel(x, mem, positions, w_in, b_igate, b_fgate, conv_w, conv_b, mh_norm, cq_norm, w_uq,
           ckv_norm, w_ukv, mem_norm, w_mem_kv, w_br_m, w_br_a, w_br_c, w_out, norm, final_norm):
    batch, seq, d = x.shape
    depth = w_in.shape[0]
    assert d == D_MODEL and seq % MG_TM == 0 and w_in.shape[2] == sum(IN_SPLITS)
    x2 = x.reshape(batch * seq, d)
    pos_row = positions.astype(F32).reshape(batch * seq // SEQ_TILE, 1, SEQ_TILE)
    for l in range(depth):
        x2 = _layer(x2, pos_row, mem, l, l == depth - 1, batch, seq, w_in, b_igate, b_fgate,
                    conv_w, conv_b, mh_norm, cq_norm, w_uq, ckv_norm, w_ukv, mem_norm, w_mem_kv,
                    w_br_m, w_br_a, w_br_c, w_out, norm, final_norm)
    return x2.reshape(batch, seq, d)
```

```python
import functools
import math

import jax
import jax.numpy as jnp
from jax import lax
from jax.experimental import pallas as pl
from jax.experimental.pallas import tpu as pltpu

F32 = jnp.float32
BF16 = jnp.bfloat16

D_MODEL = 2048
M_HEADS, M_DH = 4, 256
M_W = M_HEADS * M_DH
CONV_K = 4
A_HEADS, A_NOPE, A_ROPE, A_DV = 8, 128, 64, 128
A_DQK = A_NOPE + A_ROPE
A_W = A_HEADS * A_DV
Q_LORA = KV_LORA = 512
ROPE_THETA = 10000.0
N_MEM = 256
C_HEADS, C_DH = 4, 256
C_W = C_HEADS * C_DH
EPS = 1e-6

LANES = 128
SUBLANES = 8
A_QK_PAD = 2 * LANES
VMEM_LIMIT = 56 * 1024 * 1024

SEQ_TILE = 512
HEADS_PER_STEP = 2
MLA_HEADS_PER_STEP = 4

IN_SPLITS = (2 * M_W, M_W, M_W, M_W, M_HEADS, M_HEADS, Q_LORA, KV_LORA, A_ROPE, A_W, C_W, C_W,
             3 * D_MODEL)


def _const_spec(shape):
    nd = len(shape)
    return pl.BlockSpec(shape, lambda *_: (0,) * nd, pipeline_mode=pl.Buffered(1))


IN_OFFS = [sum(IN_SPLITS[:i]) for i in range(len(IN_SPLITS) + 1)]

W_QK, W_V, W_O, W_Z, W_AZ = 0, 2 * M_W, 3 * M_W, 4 * M_W, 5 * M_W
W_CQ = W_AZ + A_W
W_CKV = W_CQ + Q_LORA
W_KR = W_CKV + KV_LORA
IF_LANE = A_ROPE // 2


def _w_rows(offset, rows):
    assert offset % rows == 0
    return pl.BlockSpec((rows, D_MODEL), lambda *_: (offset // rows, 0),
                        pipeline_mode=pl.Buffered(1))


def _w_in_rows(first, rows):
    assert first % SUBLANES == 0
    return pl.BlockSpec((pl.Element(rows), pl.Element(D_MODEL)), lambda *_: (first, 0),
                        pipeline_mode=pl.Buffered(1))


def _params(*sem):
    return pltpu.CompilerParams(dimension_semantics=sem, vmem_limit_bytes=VMEM_LIMIT)


def _dot(a, b):
    return jnp.dot(a, b, preferred_element_type=F32)


def _dot_nt(a, b):
    return lax.dot_general(a, b, (((1,), (1,)), ((), ())), preferred_element_type=F32)


def _rms(x, g):
    return x * lax.rsqrt(jnp.mean(x * x, axis=-1, keepdims=True) + EPS) * g


def _sigmoid(x):
    return 1.0 / (1.0 + jnp.exp(-x))


def _silu(x):
    return x * _sigmoid(x)


def _keys_le_queries(nk, nq):
    shape = (nk, nq)
    return lax.broadcasted_iota(jnp.int32, shape, 0) <= lax.broadcasted_iota(jnp.int32, shape, 1)


def _causal_pieces(qi, t, nk_full):
    half = t // 2
    return ([(k0, nk_full, 0, False) for k0 in range(0, qi * t, nk_full)]
            + [(qi * t, half, 0, True), (qi * t + half, half, half, True)])


def _two_phase(chains, first, second, ahead):
    pending = [first(*c) for c in chains[:ahead]]
    for n, chain in enumerate(chains):
        if n + ahead < len(chains):
            pending.append(first(*chains[n + ahead]))
        second(*chain, pending.pop(0))


def _left_pad(x, n, fill):
    if n == 0:
        return x
    return jnp.concatenate([jnp.full((x.shape[0], n), fill, x.dtype), x], axis=1)


MP_CH = 1024


def _mlstm_proj_kernel(tiles_per_seq, x_ref, ng_ref, wqk_ref, cw_ref, cb_ref, wvt_ref, wo_ref,
                       wz_ref, h_ref, qt_ref, k_ref, vt_ref, g_ref, xbuf_ref):
    tm = x_ref.shape[0]

    @pl.when(pl.program_id(0) % tiles_per_seq == 0)
    def _():
        xbuf_ref[...] = jnp.zeros_like(xbuf_ref)

    h = _rms(x_ref[...], ng_ref[...]).astype(BF16)
    h_ref[...] = h
    widen = lambda a: jnp.concatenate([a] * (tm // LANES), axis=1)

    def conv_chunk(c):
        fs = slice(c * MP_CH, (c + 1) * MP_CH)
        acc = _dot_nt(wqk_ref[fs, :], h)
        prev = xbuf_ref[fs, :]
        lane = lax.broadcasted_iota(jnp.int32, prev.shape, 1)
        y = widen(cb_ref[fs, :]) + widen(cw_ref[CONV_K - 1, fs, :]) * acc
        for j in range(CONV_K - 1):
            back = CONV_K - 1 - j
            rolled = pltpu.roll(acc, back, 1)
            head = jnp.where(lane < back, pltpu.roll(prev, back, 1), rolled[:, :LANES])
            tap = jnp.concatenate([head, rolled[:, LANES:]], axis=1)
            y = y + widen(cw_ref[j, fs, :]) * tap
        xbuf_ref[fs, :] = acc[:, tm - LANES:]
        y = _silu(y)
        if c < M_W // MP_CH:
            qt_ref[0, 0, fs, :] = y.astype(BF16)
        else:
            ks = slice(c * MP_CH - M_W, (c + 1) * MP_CH - M_W)
            k_ref[:, ks] = (y.T * (M_DH ** -0.5)).astype(BF16)

    def value_chunk(c):
        cs = slice(c * MP_CH, (c + 1) * MP_CH)
        vt_ref[0, 0, cs, :] = _dot_nt(wvt_ref[cs, :], h).astype(BF16)

    def gate_chunk(c):
        cs = slice(c * MP_CH, (c + 1) * MP_CH)
        o = _dot_nt(h, wo_ref[cs, :])
        z = _dot_nt(h, wz_ref[cs, :])
        g_ref[:, cs] = (_sigmoid(o) * _silu(z)).astype(BF16)

    light = [functools.partial(f, c) for c in range(M_W // MP_CH) for f in (value_chunk, gate_chunk)]
    for c in range(2 * M_W // MP_CH):
        conv_chunk(c)
        if c < len(light):
            light[c]()
    for f in light[2 * M_W // MP_CH:]:
        f()


def _mlstm_proj(x2, norm_g, wall, conv_w, conv_b, batch, seq):
    T = x2.shape[0]
    tm = SEQ_TILE
    per_seq = seq // tm
    row = lambda w: pl.BlockSpec((tm, w), lambda i: (i, 0))
    out_shape = (
        jax.ShapeDtypeStruct((T, D_MODEL), BF16),
        jax.ShapeDtypeStruct((batch, per_seq, M_W, tm), BF16),
        jax.ShapeDtypeStruct((T, M_W), BF16),
        jax.ShapeDtypeStruct((batch, per_seq, M_W, tm), BF16),
        jax.ShapeDtypeStruct((T, M_W), BF16),
    )
    vt_spec = pl.BlockSpec((1, 1, M_W, tm), lambda i: (i // per_seq, i % per_seq, 0, 0))
    return pl.pallas_call(
        functools.partial(_mlstm_proj_kernel, per_seq),
        grid=(T // tm,),
        in_specs=[row(D_MODEL), _const_spec((1, D_MODEL)), _w_rows(W_QK, 2 * M_W),
                  _const_spec(conv_w.shape), _const_spec(conv_b.shape), _w_rows(W_V, M_W),
                  _w_rows(W_O, M_W), _w_rows(W_Z, M_W)],
        out_specs=(row(D_MODEL), vt_spec, row(M_W), vt_spec, row(M_W)),
        out_shape=out_shape,
        scratch_shapes=[pltpu.VMEM((2 * M_W, LANES), F32)],
        compiler_params=_params("arbitrary"),
        name="mlstm_proj",
    )(x2, norm_g, wall, conv_w, conv_b, wall, wall, wall)


def _lane_scan(x, op, fill):
    n = x.shape[-1]
    lane = lax.broadcasted_iota(jnp.int32, x.shape, x.ndim - 1)
    d = 1
    while d < n:
        shifted = pltpu.roll(x, d, x.ndim - 1)
        x = op(x, jnp.where(lane >= d, shifted, fill))
        d *= 2
    return x


def _gate_scan_kernel(if_ref, bias_ref, row_ref, col_ref):
    n_seq, S = if_ref.shape[0], if_ref.shape[1]
    t = SEQ_TILE
    rows8 = lambda x, s: x[s * SUBLANES:(s + 1) * SUBLANES]
    t8 = jnp.concatenate([(if_ref[s] + bias_ref[...]).T[IF_LANE:IF_LANE + SUBLANES, :]
                          for s in range(n_seq)], axis=0)
    lf = jnp.minimum(t8, 0.0) - jnp.log(1.0 + jnp.exp(-jnp.abs(t8)))
    b_all = _lane_scan(lf, jnp.add, 0.0)
    b_all = jnp.concatenate([pltpu.roll(rows8(b_all, s), M_HEADS, 0) for s in range(n_seq)],
                            axis=0)
    a_all = t8 - b_all
    mx_all = jnp.maximum(_lane_scan(a_all, jnp.maximum, -jnp.inf), 0.0)
    nb_all = -b_all - mx_all
    sub = lax.broadcasted_iota(jnp.int32, (SUBLANES, S), 0)
    zeros = jnp.zeros((LANES - SUBLANES, S), F32)
    n_h = HEADS_PER_STEP
    for s in range(n_seq):
        a, mx, nb = rows8(a_all, s), rows8(mx_all, s), rows8(nb_all, s)
        for p in range(M_HEADS // n_h):
            up = (SUBLANES - n_h * p) % SUBLANES
            m_grp = pltpu.roll(mx, up, 0) if up else mx
            nb_grp = pltpu.roll(nb, (up + n_h) % SUBLANES, 0)
            stack = jnp.where(sub < n_h, m_grp, nb_grp)
            for j in range(S // t):
                row_ref[s, p, j] = stack[:, j * t:(j + 1) * t]
            a_grp = pltpu.roll(a, up, 0) if up else a
            col_ref[s, p] = jnp.concatenate([a_grp, zeros], axis=0).T


GS_SEQS = 2


def _gate_scan(pre_if, bias_row, batch, seq):
    n_grp = M_HEADS // HEADS_PER_STEP
    n_tile = seq // SEQ_TILE
    ns = GS_SEQS if batch % GS_SEQS == 0 else 1
    return pl.pallas_call(
        _gate_scan_kernel,
        grid=(batch // ns,),
        in_specs=[pl.BlockSpec((ns, seq, LANES), lambda b: (b, 0, 0)), _const_spec((1, LANES))],
        out_specs=(pl.BlockSpec((ns, n_grp, n_tile, SUBLANES, SEQ_TILE), lambda b: (b, 0, 0, 0, 0)),
                   pl.BlockSpec((ns, n_grp, seq, LANES), lambda b: (b, 0, 0, 0))),
        out_shape=(jax.ShapeDtypeStruct((batch, n_grp, n_tile, SUBLANES, SEQ_TILE), F32),
                   jax.ShapeDtypeStruct((batch, n_grp, seq, LANES), F32)),
        compiler_params=_params("parallel"),
        name="gate_scan",
    )(pre_if.reshape(batch, seq, LANES), bias_row)


def _decay_attn_kernel(qt_ref, k_ref, vt_ref, g_ref, row_ref, col_ref, gain_ref, o_ref,
                       arep_ref, fac_ref):
    S = k_ref.shape[1]
    t = SEQ_TILE
    n_head = HEADS_PER_STEP
    hs = [slice(hh * M_DH, (hh + 1) * M_DH) for hh in range(n_head)]
    lane_tiles = t // LANES
    widen = lambda a: jnp.concatenate([a] * lane_tiles, axis=1)

    c_rep = [[None] * (S // t) for _ in range(n_head)]
    for hh in range(n_head):
        arep_ref[hh] = jnp.broadcast_to(col_ref[0, 0, :, hh:hh + 1], (S, LANES))
        for kj in range(S // t - 1):
            a = arep_ref[hh, kj * t:(kj + 1) * t, :]
            c_rep[hh][kj] = jnp.max(a, axis=0, keepdims=True)
            fac_ref[hh, kj * t:(kj + 1) * t, :] = jnp.exp(a - c_rep[hh][kj])

    def scores(qi, hh):
        qt = qt_ref[0, qi, hs[hh], :]
        return [_dot(k_ref[0, k0:k0 + nk, hs[hh]], qt[:, q0:])
                for k0, nk, q0, _ in _causal_pieces(qi, t, t)]

    def weigh_pv(qi, hh, sts):
        qs = slice(qi * t, (qi + 1) * t)
        m_row = row_ref[0, 0, qi, hh:hh + 1, :]
        nb_row = row_ref[0, 0, qi, n_head + hh:n_head + hh + 1, :]
        den = num = None
        for st, (k0, nk, q0, diag) in zip(sts, _causal_pieces(qi, t, t)):
            ks = slice(k0, k0 + nk)
            kc, ko = divmod(k0, t)
            if diag:
                arg = widen(arep_ref[hh, ks, :])[:, q0:] - m_row[:, q0:]
                p = st * jnp.exp(jnp.where(_keys_le_queries(nk, t - q0), arg, -jnp.inf))
                d_blk = jnp.sum(p, axis=0, keepdims=True)
                n_blk = _dot(vt_ref[0, kc, hs[hh], ko:ko + nk], p.astype(BF16))
            else:
                p = st * widen(fac_ref[hh, ks, :])
                qfac = jnp.exp(widen(c_rep[hh][kc]) - m_row)
                d_blk = qfac * jnp.sum(p, axis=0, keepdims=True)
                n_blk = qfac * _dot(vt_ref[0, kc, hs[hh], ko:ko + nk], p.astype(BF16))
            d_blk, n_blk = _left_pad(d_blk, q0, 0.0), _left_pad(n_blk, q0, 0.0)
            den, num = (d_blk, n_blk) if den is None else (den + d_blk, num + n_blk)
        hv = num / jnp.maximum(jnp.abs(den), jnp.exp(nb_row))
        hv = hv * lax.rsqrt(jnp.mean(hv * hv, axis=0, keepdims=True) + EPS)
        o_ref[0, qs, hs[hh]] = (hv.T * gain_ref[:, hs[hh]]
                                * g_ref[0, qs, hs[hh]].astype(F32)).astype(BF16)

    chains = [(qi, hh) for qi in reversed(range(S // t)) for hh in range(n_head)]
    _two_phase(chains, scores, weigh_pv, ahead=1)


def _decay_attn(qt, k, vt, gate, rowf, colf, gain, batch, seq):
    n_grp = M_HEADS // HEADS_PER_STEP
    n_tile = seq // SEQ_TILE
    w = HEADS_PER_STEP * M_DH
    blk = pl.BlockSpec((1, seq, w), lambda b, p: (b, 0, p))
    t_blk = pl.BlockSpec((1, n_tile, w, SEQ_TILE), lambda b, p: (b, 0, p, 0))
    k3, g3 = (a.reshape(batch, seq, M_W) for a in (k, gate))
    return pl.pallas_call(
        _decay_attn_kernel,
        grid=(batch, n_grp),
        in_specs=[t_blk, blk, t_blk, blk,
                  pl.BlockSpec((1, 1, n_tile, SUBLANES, SEQ_TILE), lambda b, p: (b, p, 0, 0, 0)),
                  pl.BlockSpec((1, 1, seq, LANES), lambda b, p: (b, p, 0, 0)),
                  pl.BlockSpec((1, w), lambda b, p: (0, p))],
        out_specs=blk,
        out_shape=jax.ShapeDtypeStruct((batch, seq, M_W), BF16),
        scratch_shapes=[pltpu.VMEM((HEADS_PER_STEP, seq, LANES), F32),
                        pltpu.VMEM((HEADS_PER_STEP, seq - SEQ_TILE, LANES), F32)],
        compiler_params=_params("parallel", "parallel"),
        name="decay_attn",
    )(qt, k3, vt, g3, rowf, colf, gain).reshape(batch * seq, M_W)


def _mla_proj_kernel(h_ref, pos_ref, invf_ref, gq_ref, gkv_ref, wcq_ref, wckv_ref,
                     wkr_ref, waz_ref, wqn_ref, wqr_ref, wkn_ref, wvt_ref,
                     q_ref, k_ref, vt_ref, g_ref, if_ref):
    h = h_ref[...]
    tm = h.shape[0]
    scale = math.log2(math.e) / math.sqrt(A_DQK)
    ang = invf_ref[...] * pos_ref[0]
    c32, s32 = jnp.cos(ang), jnp.sin(ang)
    z32 = jnp.zeros((A_ROPE // 2, tm), F32)
    cos = jnp.concatenate([c32, z32, c32, z32], axis=0).T
    sin = jnp.concatenate([-s32, z32, s32, z32], axis=0).T
    cos_pair = jnp.concatenate([c32, c32, c32, c32], axis=0).T
    sin_pair = jnp.concatenate([-s32, -s32, s32, s32], axis=0).T

    def rope(r, cos=cos, sin=sin):
        return r * cos + pltpu.roll(r, LANES // 2, 1) * sin

    cq = _dot_nt(h, wcq_ref[...])
    ckv = _dot_nt(h, wckv_ref[...])
    g_ref[...] = _silu(_dot_nt(h, waz_ref[...])).astype(BF16)
    kr_if = _dot_nt(h, wkr_ref[...])
    if_ref[...] = kr_if
    kr = rope(kr_if)
    kr_by_parity = (kr.astype(BF16), pltpu.roll(kr, A_ROPE // 2, 1).astype(BF16))

    cqn = _rms(cq, gq_ref[...]).astype(BF16)
    ckvn = _rms(ckv, gkv_ref[...]).astype(BF16)
    qn = _dot(cqn, wqn_ref[...]) * scale
    qr = _dot(cqn, wqr_ref[...]) * scale
    kn = _dot(ckvn, wkn_ref[...])
    for hd in range(A_HEADS):
        base = hd * A_QK_PAD
        ls = slice(hd * LANES, (hd + 1) * LANES)
        pair = slice((hd // 2) * LANES, (hd // 2 + 1) * LANES)
        q_ref[:, base:base + LANES] = qn[:, ls].astype(BF16)
        q_ref[:, base + LANES:base + A_QK_PAD] = rope(qr[:, pair], cos_pair, sin_pair).astype(BF16)
        k_ref[:, base:base + LANES] = kn[:, ls].astype(BF16)
        k_ref[:, base + LANES:base + A_QK_PAD] = kr_by_parity[hd % 2]
    vt_ref[0, 0] = _dot_nt(wvt_ref[...], ckvn).astype(BF16)


def _mla_proj(h, pos_row, invf, gq, gkv, wall, wqn, wqr, wkn, wvt, batch, seq):
    T = h.shape[0]
    tm = SEQ_TILE
    per_seq = seq // tm
    row = lambda w: pl.BlockSpec((tm, w), lambda i: (i, 0))
    small = (invf, gq, gkv)
    ups = (wqn, wqr, wkn, wvt)
    vt_spec = pl.BlockSpec((1, 1, A_W, tm), lambda i: (i // per_seq, i % per_seq, 0, 0))
    return pl.pallas_call(
        _mla_proj_kernel,
        grid=(T // tm,),
        in_specs=([row(D_MODEL), pl.BlockSpec((1, 1, tm), lambda i: (i, 0, 0))]
                  + [_const_spec(c.shape) for c in small]
                  + [_w_rows(W_CQ, Q_LORA), _w_rows(W_CKV, KV_LORA), _w_rows(W_KR, LANES),
                     _w_rows(W_AZ, A_W)]
                  + [_const_spec(c.shape) for c in ups]),
        out_specs=(row(A_HEADS * A_QK_PAD), row(A_HEADS * A_QK_PAD), vt_spec, row(A_W),
                   row(LANES)),
        out_shape=(jax.ShapeDtypeStruct((T, A_HEADS * A_QK_PAD), BF16),
                   jax.ShapeDtypeStruct((T, A_HEADS * A_QK_PAD), BF16),
                   jax.ShapeDtypeStruct((batch, per_seq, A_W, tm), BF16),
                   jax.ShapeDtypeStruct((T, A_W), BF16),
                   jax.ShapeDtypeStruct((T, LANES), F32)),
        compiler_params=_params("parallel"),
        name="mla_proj",
    )(h, pos_row, *small, wall, wall, wall, wall, *ups)


def _mla_attn_kernel(q_ref, k_ref, vt_ref, g_ref, o_ref):
    S = q_ref.shape[1]
    t = SEQ_TILE
    n_head = MLA_HEADS_PER_STEP
    hq = [slice(hh * A_QK_PAD, (hh + 1) * A_QK_PAD) for hh in range(n_head)]
    hv = [slice(hh * A_DV, (hh + 1) * A_DV) for hh in range(n_head)]

    tq = t // 2

    def pieces_of(qj):
        before = qj * tq
        out = [(k0, t, False) for k0 in range(0, before - before % t, t)]
        if before % t:
            out.append((before - tq, tq, False))
        return out + [(before, tq, True)]

    def scores(qj, hh):
        q = q_ref[0, qj * tq:(qj + 1) * tq, hq[hh]]
        st = []
        for k0, nk, diag in pieces_of(qj):
            s = _dot_nt(k_ref[0, k0:k0 + nk, hq[hh]], q)
            st.append(jnp.where(_keys_le_queries(nk, tq), s, -jnp.inf) if diag else s)
        return st

    def softmax_pv(qj, hh, st):
        qs = slice(qj * tq, (qj + 1) * tq)
        m = functools.reduce(jnp.maximum, [jnp.max(s, axis=0, keepdims=True) for s in st])
        l = acc = None
        for s, (k0, nk, _) in zip(st, pieces_of(qj)):
            p = jnp.exp2(s - m)
            l_blk = jnp.sum(p, axis=0, keepdims=True)
            kc, ko = divmod(k0, t)
            pv = _dot(vt_ref[0, kc, hv[hh], ko:ko + nk], p.astype(BF16))
            l, acc = (l_blk, pv) if l is None else (l + l_blk, acc + pv)
        o_ref[0, qs, hv[hh]] = ((acc / l).T * g_ref[0, qs, hv[hh]].astype(F32)).astype(BF16)

    chains = [(qj, hh) for qj in range(S // tq) for hh in range(n_head)]
    _two_phase(chains, scores, softmax_pv, ahead=2)


def _mla_attn(q, k, vt, gate, batch, seq):
    n_tile = seq // SEQ_TILE
    n_h = MLA_HEADS_PER_STEP
    qk_blk = pl.BlockSpec((1, seq, n_h * A_QK_PAD), lambda b, p: (b, 0, p))
    v_blk = pl.BlockSpec((1, seq, n_h * A_DV), lambda b, p: (b, 0, p))
    vt_blk = pl.BlockSpec((1, n_tile, n_h * A_DV, SEQ_TILE), lambda b, p: (b, 0, p, 0))
    q3 = q.reshape(batch, seq, A_HEADS * A_QK_PAD)
    k3 = k.reshape(batch, seq, A_HEADS * A_QK_PAD)
    g3 = gate.reshape(batch, seq, A_W)
    return pl.pallas_call(
        _mla_attn_kernel,
        grid=(batch, A_HEADS // n_h),
        in_specs=[qk_blk, qk_blk, vt_blk, v_blk],
        out_specs=v_blk,
        out_shape=jax.ShapeDtypeStruct((batch, seq, A_W), BF16),
        compiler_params=_params("parallel", "parallel"),
        name="mla_attn",
    )(q3, k3, vt, g3).reshape(batch * seq, A_W)


def _cast_once(w_ref, wbf_ref):
    @pl.when(pl.program_id(0) == 0)
    def _():
        wbf_ref[...] = w_ref[...].astype(BF16)


def _mem_kv_kernel(mem_ref, g_ref, w_ref, kv_ref, wbf_ref):
    _cast_once(w_ref, wbf_ref)
    nb, nm, d = mem_ref.shape
    m = _rms(mem_ref[...].reshape(nb * nm, d), g_ref[...]).astype(BF16)
    kv_ref[...] = _dot(m, wbf_ref[...]).astype(BF16).reshape(nb, nm, 2 * C_W)


MKV_BATCH = 4


def _mem_kv(mem, gain, w):
    batch = mem.shape[0]
    nb = MKV_BATCH if batch % MKV_BATCH == 0 else 1
    return pl.pallas_call(
        _mem_kv_kernel,
        grid=(batch // nb,),
        in_specs=[pl.BlockSpec((nb, N_MEM, D_MODEL), lambda b: (b, 0, 0)),
                  _const_spec((1, D_MODEL)), _const_spec(w.shape)],
        out_specs=pl.BlockSpec((nb, N_MEM, 2 * C_W), lambda b: (b, 0, 0)),
        out_shape=jax.ShapeDtypeStruct((batch, N_MEM, 2 * C_W), BF16),
        scratch_shapes=[pltpu.VMEM(w.shape, BF16)],
        compiler_params=_params("arbitrary"),
        name="mem_kv",
    )(mem, gain, w)


def _mem_attn_kernel(h_ref, kv_ref, w_ref, o_ref, wbf_ref):
    _cast_once(w_ref, wbf_ref)
    h = h_ref[...]
    cq = (_dot_nt(h, wbf_ref[0:C_W, :]) * (C_DH ** -0.5)).astype(BF16)
    heads = [slice(hd * C_DH, (hd + 1) * C_DH) for hd in range(C_HEADS)]

    def matmuls(hs):
        gate_rows = slice(C_W + hs.start, C_W + hs.stop)
        return _dot_nt(cq[:, hs], kv_ref[0, :, hs]), _dot_nt(h, wbf_ref[gate_rows, :])

    s, cz = matmuls(heads[0])
    for hd, hs in enumerate(heads):
        nxt = matmuls(heads[hd + 1]) if hd + 1 < C_HEADS else None
        p = jnp.exp(s - jnp.max(s, axis=-1, keepdims=True))
        l = jnp.sum(p, axis=-1, keepdims=True)
        o = _dot(p.astype(BF16), kv_ref[0, :, C_W + hd * C_DH:C_W + (hd + 1) * C_DH]) / l
        o_ref[:, hs] = (o * _silu(cz)).astype(BF16)
        if nxt is not None:
            s, cz = nxt


MA_TM = 1024


def _mem_attn(h, kv, wt, seq):
    T = h.shape[0]
    tm = MA_TM
    per_seq = seq // tm
    return pl.pallas_call(
        _mem_attn_kernel,
        grid=(T // tm,),
        in_specs=[pl.BlockSpec((tm, D_MODEL), lambda i: (i, 0)),
                  pl.BlockSpec((1, N_MEM, 2 * C_W), lambda i: (i // per_seq, 0, 0)),
                  _w_in_rows(IN_OFFS[10], 2 * C_W)],
        out_specs=pl.BlockSpec((tm, C_W), lambda i: (i, 0)),
        out_shape=jax.ShapeDtypeStruct((T, C_W), BF16),
        scratch_shapes=[pltpu.VMEM((2 * C_W, D_MODEL), BF16)],
        compiler_params=_params("arbitrary"),
        name="mem_attn",
    )(h, kv, wt)


MG_TM = 1024
MG_TN = 512


def _merge_kernel(h_ref, hm_ref, ha_ref, hc_ref, wgm_ref, wga_ref, wgc_ref,
                  wbm_ref, wba_ref, wbc_ref, o_ref):
    h = h_ref[...]
    gate = lambda w_ref: _sigmoid(_dot_nt(h, w_ref[...]))
    branch = lambda a_ref, w_ref: _dot(a_ref[...], w_ref[...])
    acc = gate(wgm_ref) * branch(hm_ref, wbm_ref)
    acc = acc + gate(wga_ref) * branch(ha_ref, wba_ref)
    acc = acc + gate(wgc_ref) * branch(hc_ref, wbc_ref)
    o_ref[...] = acc.astype(BF16)


def _merge(h, hm, ha, hc, wt, wbm, wba, wbc):
    T = h.shape[0]
    tm, tn = MG_TM, MG_TN
    row = lambda w: pl.BlockSpec((tm, w), lambda i, j: (i, 0))
    col = lambda kdim: pl.BlockSpec((kdim, tn), lambda i, j: (0, j))

    def gate(branch):
        first = (W_GATE + branch * D_MODEL) // tn
        return pl.BlockSpec((tn, D_MODEL), lambda i, j: (first + j, 0))

    return pl.pallas_call(
        _merge_kernel,
        grid=(T // tm, D_MODEL // tn),
        in_specs=[row(D_MODEL), row(M_W), row(A_W), row(C_W),
                  gate(0), gate(1), gate(2), col(M_W), col(A_W), col(C_W)],
        out_specs=pl.BlockSpec((tm, tn), lambda i, j: (i, j)),
        out_shape=jax.ShapeDtypeStruct((T, D_MODEL), BF16),
        compiler_params=_params("parallel", "arbitrary"),
        name="merge",
    )(h, hm, ha, hc, wt, wt, wt, wbm, wba, wbc)


def _out_proj_kernel(final, x_ref, m_ref, w_ref, g_ref, o_ref, wbf_ref):
    _cast_once(w_ref, wbf_ref)
    y = x_ref[...] + _dot(m_ref[...], wbf_ref[...])
    o_ref[...] = _rms(y, g_ref[...]) if final else y


def _out_proj(x2, merged, w, gain, final):
    T = x2.shape[0]
    tm = SEQ_TILE
    row = pl.BlockSpec((tm, D_MODEL), lambda i: (i, 0))
    return pl.pallas_call(
        functools.partial(_out_proj_kernel, final),
        grid=(T // tm,),
        in_specs=[row, row, _const_spec(w.shape), _const_spec((1, D_MODEL))],
        out_specs=row,
        out_shape=jax.ShapeDtypeStruct((T, D_MODEL), F32),
        scratch_shapes=[pltpu.VMEM(w.shape, BF16)],
        compiler_params=_params("arbitrary"),
        name="out_proj",
    )(x2, merged, w, gain)


PK_ROWS = 1024
W_GATE = W_KR + PK_ROWS
W_ROWS = W_GATE + 3 * D_MODEL
PK_SRC = ([b * PK_ROWS for b in range(IN_OFFS[4] // PK_ROWS)]
          + [IN_OFFS[9], IN_OFFS[6], IN_OFFS[8]]
          + [IN_OFFS[12] + b * PK_ROWS for b in range(3 * D_MODEL // PK_ROWS)])


def _pack_w_kernel(w_ref, wif_ref, o_ref):
    b = pl.program_id(0)
    x = w_ref[...]
    o_ref[...] = x.astype(BF16)

    @pl.when(b == W_KR // PK_ROWS)
    def _():
        half = A_ROPE // 2
        n_if = 2 * M_HEADS
        zeros = lambda n: jnp.zeros((n, D_MODEL), F32)
        o_ref[...] = jnp.concatenate(
            [x[:half], wif_ref[...], zeros(half - n_if), x[half:2 * half], zeros(half),
             zeros(PK_ROWS - 4 * half)], axis=0).astype(BF16)


def _pack_w_in(wt):
    assert len(PK_SRC) * PK_ROWS == W_ROWS and all(s % SUBLANES == 0 for s in PK_SRC)

    def src(b):
        first = sum(jnp.where(b == i, s, 0) for i, s in enumerate(PK_SRC))
        return pl.multiple_of(first, SUBLANES), 0

    return pl.pallas_call(
        _pack_w_kernel,
        grid=(len(PK_SRC),),
        in_specs=[pl.BlockSpec((pl.Element(PK_ROWS), pl.Element(D_MODEL)), src),
                  pl.BlockSpec((pl.Element(2 * M_HEADS), pl.Element(D_MODEL)),
                               lambda b: (IN_OFFS[4], 0))],
        out_specs=pl.BlockSpec((PK_ROWS, D_MODEL), lambda b: (b, 0)),
        out_shape=jax.ShapeDtypeStruct((W_ROWS, D_MODEL), BF16),
        compiler_params=_params("parallel"),
        name="pack_w",
    )(wt, wt)


def _layer(x2, pos_row, kv_mem_in, l, final, batch, seq, w_in, b_igate, b_fgate, conv_w, conv_b,
           mh_norm, cq_norm, w_uq, ckv_norm, w_ukv, mem_norm, w_mem_kv, w_br_m, w_br_a, w_br_c,
           w_out, norm, final_norm):
    bf = lambda a: a.astype(BF16)
    row = lambda a: a.reshape(1, -1).astype(F32)

    wt = w_in[l].T
    wall = _pack_w_in(wt)
    uq = w_uq[l].reshape(Q_LORA, A_HEADS, A_DQK)
    wqn = bf(uq[:, :, :A_NOPE].reshape(Q_LORA, A_HEADS * A_NOPE))
    half = A_ROPE // 2
    wqr = bf(uq[:, :, A_NOPE:].reshape(Q_LORA, A_HEADS // 2, 2, 2, half)
             .transpose(0, 1, 3, 2, 4).reshape(Q_LORA, (A_HEADS // 2) * LANES))
    ukv = w_ukv[l].reshape(KV_LORA, A_HEADS, A_NOPE + A_DV)
    wkn = bf(ukv[:, :, :A_NOPE].reshape(KV_LORA, A_HEADS * A_NOPE))
    wvvt = bf(ukv[:, :, A_NOPE:].reshape(KV_LORA, A_HEADS * A_DV).T)
    bias_row = jnp.pad(jnp.concatenate([b_igate[l], b_fgate[l]]).astype(F32),
                       (IF_LANE, LANES - IF_LANE - 2 * M_HEADS)).reshape(1, LANES)
    invf = (ROPE_THETA ** (-jnp.arange(0, A_ROPE, 2, dtype=F32) / A_ROPE)).reshape(-1, 1)

    lane_rep = lambda a: jnp.broadcast_to(a.astype(F32)[..., None], a.shape + (LANES,))
    h, mqt, mk, mvt, mgate = _mlstm_proj(x2, row(norm[l]), wall, lane_rep(conv_w[l]),
                                         lane_rep(conv_b[l]), batch, seq)
    aq, ak, avt, agate, pre_if = _mla_proj(h, pos_row, invf, row(cq_norm[l]), row(ckv_norm[l]),
                                           wall, wqn, wqr, wkn, wvvt, batch, seq)
    rowf, colf = _gate_scan(pre_if, bias_row, batch, seq)
    hm = _decay_attn(mqt, mk, mvt, mgate, rowf, colf, row(mh_norm[l]), batch, seq)
    ha = _mla_attn(aq, ak, avt, agate, batch, seq)

    kv_mem = _mem_kv(kv_mem_in, row(mem_norm[l]), w_mem_kv[l])
    hc = _mem_attn(h, kv_mem, wt, seq)

    merged = _merge(h, hm, ha, hc, wall, bf(w_br_m[l]), bf(w_br_a[l]), bf(w_br_c[l]))
    return _out_proj(x2, merged, w_out[l], row(final_norm), final)


def kernel(x, mem, positions, w_in, b_igate, b_fgate, conv_w, conv_b, mh_norm, cq_norm, w_uq,
           ckv_norm, w_ukv, mem_norm, w_mem_kv, w_br_m, w_br_a, w_br_c, w_out, norm, final_norm):
    batch, seq, d = x.shape
    depth = w_in.shape[0]
    assert d == D_MODEL and seq % MG_TM == 0 and w_in.shape[2] == sum(IN_SPLITS)
    x2 = x.reshape(batch * seq, d)
    pos_row = positions.astype(F32).reshape(batch * seq // SEQ_TILE, 1, SEQ_TILE)
    for l in range(depth):
        x2 = _layer(x2, pos_row, mem, l, l == depth - 1, batch, seq, w_in, b_igate, b_fgate,
                    conv_w, conv_b, mh_norm, cq_norm, w_uq, ckv_norm, w_ukv, mem_norm, w_mem_kv,
                    w_br_m, w_br_a, w_br_c, w_out, norm, final_norm)
    return x2.reshape(batch, seq, d)
```
